```python
import math
import jax
import jax.numpy as jnp
from jax import lax
import numpy as np

D_MODEL = 2048
BATCH = 2
SEQ = 4096
DEPTH = 4

GRID_W = 64
CTX_LEN = 256
HEAD_DIM = 128
N_HEADS_A = D_MODEL // (2 * HEAD_DIM)
WIDTH_A = N_HEADS_A * HEAD_DIM
WIDTH_B = D_MODEL - WIDTH_A
WIN_ROWS = 8
WIN_COLS = 16
D_FF = 128 * ((8 * D_MODEL // 3 + 127) // 128)
HYENA_SHORT = 3
HYENA_EMB = 33
HYENA_ORDER = 64
HYENA_FAST_DECAY = 0.3
HYENA_SLOW_DECAY = 1.5
HYENA_TARGET = 1e-2
CONV_WIDTH = 31
CONV_INNER = D_MODEL
N_EVEN = (DEPTH + 1) // 2
N_ODD = DEPTH // 2
N_MOD = 9
RMS_EPS = 1e-6
LN_EPS = 1e-5

kernel_name = 'hybrid_natten_hyena_conformer_dit'


def rmsnorm(x, g):
    x32 = x.astype(jnp.float32)
    y = x32 * lax.rsqrt(jnp.mean(x32 * x32, axis=-1, keepdims=True) + RMS_EPS)
    return y.astype(x.dtype) * g


def layernorm(x, g, b):
    x32 = x.astype(jnp.float32)
    mu = jnp.mean(x32, axis=-1, keepdims=True)
    var = jnp.mean(jnp.square(x32 - mu), axis=-1, keepdims=True)
    return ((x32 - mu) * lax.rsqrt(var + LN_EPS)).astype(x.dtype) * g + b


def modulate(h, shift, scale):
    return h * (1 + scale) + shift


def swiglu(h, w1, w3, w2):
    return (jax.nn.silu(h @ w1) * (h @ w3)) @ w2


def depthwise_conv(x, w, b):
    k, ch = w.shape
    y = lax.conv_general_dilated(x, w[:, None, :].astype(x.dtype), (1,), [(k // 2, k // 2)],
                                 dimension_numbers=('NWC', 'WIO', 'NWC'), feature_group_count=ch)
    return y + b


def split_heads(t):
    return t.reshape(t.shape[0], t.shape[1], N_HEADS_A, HEAD_DIM)


def context_attention(q, k, v):
    s = jnp.einsum('bqhd,bkhd->bhqk', q, k).astype(jnp.float32)
    p = jax.nn.softmax(s, axis=-1).astype(v.dtype)
    o = jnp.einsum('bhqk,bkhd->bqhd', p, v)
    return o.reshape(o.shape[0], o.shape[1], -1)


def neighbourhood_attention(q, k, v, kc, vc, rpb):
    b, s, h, dh = q.shape
    rows = s // GRID_W
    kh = min(WIN_ROWS, rows)
    kw = WIN_COLS
    qg = q.reshape(b, rows, GRID_W, h, dh)
    kg = k.reshape(b, rows, GRID_W, h, dh)
    vg = v.reshape(b, rows, GRID_W, h, dh)
    cols = jnp.arange(GRID_W)
    col_start = jnp.clip(cols - kw // 2, 0, GRID_W - kw)
    col_idx = col_start[:, None] + jnp.arange(kw)
    dc_idx = col_idx - cols[:, None] + WIN_COLS - 1
    bias_cols = rpb[:, :, dc_idx]
    n_loc = kh * kw

    def row_block(r):
        rs = jnp.clip(r - kh // 2, 0, rows - kh)
        q_r = lax.dynamic_index_in_dim(qg, r, axis=1, keepdims=False)
        k_rows = lax.dynamic_slice_in_dim(kg, rs, kh, axis=1)
        v_rows = lax.dynamic_slice_in_dim(vg, rs, kh, axis=1)
        k_win = k_rows[:, :, col_idx]
        v_win = v_rows[:, :, col_idx]
        dr_idx = rs + jnp.arange(kh) - r + WIN_ROWS - 1
        bias = bias_cols[:, dr_idx].transpose(0, 2, 1, 3)
        s_loc = jnp.einsum('bchd,brcjhd->bhcrj', q_r, k_win) + bias[None]
        s_ctx = jnp.einsum('bchd,bnhd->bhcn', q_r, kc)
        sc = jnp.concatenate([s_loc.reshape(b, h, GRID_W, n_loc), s_ctx], axis=-1)
        p = jax.nn.softmax(sc.astype(jnp.float32), axis=-1).astype(v.dtype)
        p_loc = p[..., :n_loc].reshape(b, h, GRID_W, kh, kw)
        p_ctx = p[..., n_loc:]
        return (jnp.einsum('bhcrj,brcjhd->bchd', p_loc, v_win)
                + jnp.einsum('bhcn,bnhd->bchd', p_ctx, vc))

    out = lax.map(row_block, jnp.arange(rows))
    return out.transpose(1, 0, 2, 3, 4).reshape(b, s, h * dh)


def hyena_filter(length, w0, b0, w1, b1, w2, b2, w3, freq):
    t = jnp.linspace(0.0, 1.0, length, dtype=jnp.float32)[:, None]
    bands = (HYENA_EMB - 1) // 2
    f = jnp.linspace(1e-4, bands - 1, bands, dtype=jnp.float32)
    w = 2 * math.pi * jnp.arange(length, dtype=jnp.float32)[:, None] / length
    z = jnp.concatenate([t, jnp.cos(f * w), -jnp.sin(f * w)], axis=-1)
    hh = jnp.sin(freq * (z @ w0 + b0))
    hh = jnp.sin(freq * (hh @ w1 + b1))
    hh = jnp.sin(freq * (hh @ w2 + b2))
    hh = (hh @ w3).astype(jnp.float32).reshape(length, 2, -1)
    ch = hh.shape[-1]
    max_decay = math.log(HYENA_TARGET) / HYENA_FAST_DECAY
    min_decay = math.log(HYENA_TARGET) / HYENA_SLOW_DECAY
    deltas = jnp.abs(jnp.linspace(min_decay, max_decay, ch, dtype=jnp.float32))
    window = jnp.exp(-t * deltas)
    return hh * window[:, None, :]


def long_conv_bidir(u, hf):
    length = u.shape[1]
    n = 2 * length
    k = jnp.concatenate([hf[:, 0], jnp.zeros_like(hf[:1, 1]), hf[:0:-1, 1]], axis=0)
    u_f = jnp.fft.rfft(u.astype(jnp.float32), n=n, axis=1)
    k_f = jnp.fft.rfft(k, n=n, axis=0)
    y = jnp.fft.irfft(u_f * k_f[None], n=n, axis=1)[:, :length]
    return y.astype(u.dtype)


def hyena_mix(z, short_w, short_b, w0, b0, w1, b1, w2, b2, w3, freq, bias):
    z = depthwise_conv(z, short_w, short_b)
    x0, x1, v = jnp.split(z, 3, axis=-1)
    hf = hyena_filter(z.shape[1], w0, b0, w1, b1, w2, b2, w3, freq)
    u = v * x1
    y = long_conv_bidir(u, hf) + u * bias
    return y * x0


def even_mixer(hl, hc, ctx_out, w_in, w_out, rpb, hy):
    scale = HEAD_DIM ** -0.5
    pl = hl @ w_in
    ql = split_heads(pl[..., :WIDTH_A]) * scale
    kl = split_heads(pl[..., WIDTH_A:2 * WIDTH_A])
    vl = split_heads(pl[..., 2 * WIDTH_A:3 * WIDTH_A])
    zl = pl[..., 3 * WIDTH_A:]
    if ctx_out:
        pc = hc @ w_in
        qc = split_heads(pc[..., :WIDTH_A]) * scale
        kc = split_heads(pc[..., WIDTH_A:2 * WIDTH_A])
        vc = split_heads(pc[..., 2 * WIDTH_A:3 * WIDTH_A])
        zc = pc[..., 3 * WIDTH_A:]
    else:
        kvc = hc @ w_in[:, WIDTH_A:3 * WIDTH_A]
        kc = split_heads(kvc[..., :WIDTH_A])
        vc = split_heads(kvc[..., WIDTH_A:])
    al = neighbourhood_attention(ql, kl, vl, kc, vc, rpb)
    yl = hyena_mix(zl, *hy)
    out_l = jnp.concatenate([al, yl], axis=-1) @ w_out
    out_c = None
    if ctx_out:
        ac = context_attention(qc, kc, vc)
        yc = hyena_mix(zc, *hy)
        out_c = jnp.concatenate([ac, yc], axis=-1) @ w_out
    return out_l, out_c


def conformer_conv(h, w_pw1, b_pw1, w_dw, b_dw, ln_g, ln_b, w_pw2, b_pw2):
    a, g = jnp.split(h @ w_pw1 + b_pw1, 2, axis=-1)
    u = a * jax.nn.sigmoid(g)
    u = depthwise_conv(u, w_dw, b_dw)
    u = jax.nn.silu(layernorm(u, ln_g, ln_b))
    return u @ w_pw2 + b_pw2


def setup_inputs(seed: int = 0) -> dict:
    key = jax.random.key(seed)
    ks = iter(jax.random.split(key, 40))

    def nrm(shape, scale):
        return jax.random.normal(next(ks), shape, jnp.float32) * scale

    D = D_MODEL
    proj_w = 3 * WIDTH_A + 3 * WIDTH_B
    return {
        'x': nrm((BATCH, SEQ, D), 1.0),
        'c': nrm((BATCH, D), 1.0),
        'ctx': nrm((BATCH, CTX_LEN, D), 1.0),
        'c_ctx': nrm((D,), 1.0),
        'ada_w': nrm((DEPTH, D, N_MOD * D), D ** -0.5),
        'ada_b': nrm((DEPTH, N_MOD * D), 0.02),
        'norm_g': 1.0 + nrm((DEPTH, 3, D), 0.02),
        'ff1_w1': nrm((DEPTH, D, D_FF), D ** -0.5),
        'ff1_w3': nrm((DEPTH, D, D_FF), D ** -0.5),
        'ff1_w2': nrm((DEPTH, D_FF, D), D_FF ** -0.5),
        'ff2_w1': nrm((DEPTH, D, D_FF), D ** -0.5),
        'ff2_w3': nrm((DEPTH, D, D_FF), D ** -0.5),
        'ff2_w2': nrm((DEPTH, D_FF, D), D_FF ** -0.5),
        'mix_w_in': nrm((N_EVEN, D, proj_w), D ** -0.5),
        'mix_w_out': nrm((N_EVEN, WIDTH_A + WIDTH_B, D), (WIDTH_A + WIDTH_B) ** -0.5),
        'na_rpb': nrm((N_EVEN, N_HEADS_A, 2 * WIN_ROWS - 1, 2 * WIN_COLS - 1), 0.1),
        'hy_short_w': nrm((N_EVEN, HYENA_SHORT, 3 * WIDTH_B), HYENA_SHORT ** -0.5),
        'hy_short_b': nrm((N_EVEN, 3 * WIDTH_B), 0.02),
        'hy_w0': nrm((N_EVEN, HYENA_EMB, HYENA_ORDER), HYENA_EMB ** -0.5),
        'hy_b0': nrm((N_EVEN, HYENA_ORDER), 0.02),
        'hy_w1': nrm((N_EVEN, HYENA_ORDER, HYENA_ORDER), HYENA_ORDER ** -0.5),
        'hy_b1': nrm((N_EVEN, HYENA_ORDER), 0.02),
        'hy_w2': nrm((N_EVEN, HYENA_ORDER, HYENA_ORDER), HYENA_ORDER ** -0.5),
        'hy_b2': nrm((N_EVEN, HYENA_ORDER), 0.02),
        'hy_w3': nrm((N_EVEN, HYENA_ORDER, 2 * WIDTH_B), 0.1 * HYENA_ORDER ** -0.5),
        'hy_freq': 1.0 + nrm((N_EVEN, HYENA_ORDER), 0.02),
        'hy_bias': nrm((N_EVEN, WIDTH_B), 1.0),
        'cv_w_pw1': nrm((N_ODD, D, 2 * CONV_INNER), D ** -0.5),
        'cv_b_pw1': nrm((N_ODD, 2 * CONV_INNER), 0.02),
        'cv_w_dw': nrm((N_ODD, CONV_WIDTH, CONV_INNER), CONV_WIDTH ** -0.5),
        'cv_b_dw': nrm((N_ODD, CONV_INNER), 0.02),
        'cv_ln_g': 1.0 + nrm((N_ODD, CONV_INNER), 0.02),
        'cv_ln_b': nrm((N_ODD, CONV_INNER), 0.02),
        'cv_w_pw2': nrm((N_ODD, CONV_INNER, D), CONV_INNER ** -0.5),
        'cv_b_pw2': nrm((N_ODD, D), 0.02),
        'final_g': 1.0 + nrm((D,), 0.02),
    }


def reference(x, c, ctx, c_ctx, ada_w, ada_b, norm_g, ff1_w1, ff1_w3, ff1_w2, ff2_w1, ff2_w3, ff2_w2,
              mix_w_in, mix_w_out, na_rpb, hy_short_w, hy_short_b, hy_w0, hy_b0, hy_w1, hy_b1,
              hy_w2, hy_b2, hy_w3, hy_freq, hy_bias, cv_w_pw1, cv_b_pw1, cv_w_dw, cv_b_dw,
              cv_ln_g, cv_ln_b, cv_w_pw2, cv_b_pw2, final_g):
    last_even = (DEPTH - 1) - ((DEPTH - 1) % 2)
    s_lat = jax.nn.silu(c)
    s_ctx = jax.nn.silu(c_ctx)
    xl, xc = x, ctx
    for i in range(DEPTH):
        ctx_full = i < last_even
        ml = (s_lat @ ada_w[i] + ada_b[i]).reshape(-1, N_MOD, 1, D_MODEL)
        ml = [ml[:, j] for j in range(N_MOD)]
        xl = xl + 0.5 * ml[2] * swiglu(modulate(rmsnorm(xl, norm_g[i, 0]), ml[0], ml[1]),
                                       ff1_w1[i], ff1_w3[i], ff1_w2[i])
        hl = modulate(rmsnorm(xl, norm_g[i, 1]), ml[3], ml[4])
        hc = None
        mc = None
        if i <= last_even:
            mc = (s_ctx @ ada_w[i] + ada_b[i]).reshape(N_MOD, D_MODEL)
            xc = xc + 0.5 * mc[2] * swiglu(modulate(rmsnorm(xc, norm_g[i, 0]), mc[0], mc[1]),
                                           ff1_w1[i], ff1_w3[i], ff1_w2[i])
            hc = modulate(rmsnorm(xc, norm_g[i, 1]), mc[3], mc[4])
        if i % 2 == 0:
            e = i // 2
            hy = (hy_short_w[e], hy_short_b[e], hy_w0[e], hy_b0[e], hy_w1[e], hy_b1[e],
                  hy_w2[e], hy_b2[e], hy_w3[e], hy_freq[e], hy_bias[e])
            ol, oc = even_mixer(hl, hc, ctx_full, mix_w_in[e], mix_w_out[e], na_rpb[e], hy)
        else:
            o = i // 2
            cv = (cv_w_pw1[o], cv_b_pw1[o], cv_w_dw[o], cv_b_dw[o], cv_ln_g[o], cv_ln_b[o],
                  cv_w_pw2[o], cv_b_pw2[o])
            ol = conformer_conv(hl, *cv)
            oc = conformer_conv(hc, *cv) if ctx_full else None
        xl = xl + ml[5] * ol
        xl = xl + 0.5 * ml[8] * swiglu(modulate(rmsnorm(xl, norm_g[i, 2]), ml[6], ml[7]),
                                       ff2_w1[i], ff2_w3[i], ff2_w2[i])
        if ctx_full:
            xc = xc + mc[5] * oc
            xc = xc + 0.5 * mc[8] * swiglu(modulate(rmsnorm(xc, norm_g[i, 2]), mc[6], mc[7]),
                                           ff2_w1[i], ff2_w3[i], ff2_w2[i])
    return rmsnorm(xl, final_g)
```

```python
import functools
import math

import numpy as np
import jax
import jax.numpy as jnp
from jax import lax
from jax.experimental import pallas as pl
from jax.experimental.pallas import tpu as pltpu

F32 = jnp.float32
BF16 = jnp.bfloat16

D = 2048
NB = 2
S = 4096
DEPTH = 4
GRID_W = 64
GRID_H = S // GRID_W
LC = 256
HD = 128
NH = 8
WA = NH * HD
WB = D - WA
WIN_R = 8
WIN_C = 16
DFF = 5504
HY_EMB = 33
HY_ORDER = 64
CONV_W = 31
N_MOD = 9
RMS_EPS = 1e-6
LN_EPS = 1e-5
LAST_EVEN = (DEPTH - 1) - ((DEPTH - 1) % 2)

T_LAT = NB * S
T_CTX = NB * LC
T_ALL = T_LAT + T_CTX

LANE = 128
TM = 512
TF = 512
DFF_PAD = TF * (-(-DFF // TF))
NBLK_LAT = T_LAT // TM
NBLK_ALL = T_ALL // TM
BLK_PER_BATCH = S // TM
NEG = -1e30
VMEM_LIMIT = 56 * 1024 * 1024


def _cparams(sem, vmem=VMEM_LIMIT):
    return pltpu.CompilerParams(dimension_semantics=sem, vmem_limit_bytes=vmem)


def _mod_index(i):
    return jnp.minimum(i // BLK_PER_BATCH, NB)


def _norm_mod(x, g, shift, scale):
    ms = jnp.mean(x * x, axis=-1, keepdims=True)
    return (x * lax.rsqrt(ms + RMS_EPS)) * g * (1.0 + scale) + shift


def _ada_kernel(s_ref, w_ref, b_ref, o_ref):
    s = s_ref[...]
    s = (s * jax.nn.sigmoid(s)).astype(BF16)
    o_ref[...] = jnp.dot(s, w_ref[...].astype(BF16), preferred_element_type=F32) + b_ref[...]


def _ada(cond, ada_w, ada_b):
    tn = 1024
    n = N_MOD * D
    return pl.pallas_call(
        _ada_kernel,
        grid=(DEPTH, n // tn),
        in_specs=[
            pl.BlockSpec((8, D), lambda l, j: (0, 0)),
            pl.BlockSpec((None, D, tn), lambda l, j: (l, 0, j)),
            pl.BlockSpec((None, 1, tn), lambda l, j: (l, 0, j)),
        ],
        out_specs=pl.BlockSpec((None, 8, tn), lambda l, j: (l, 0, j)),
        out_shape=jax.ShapeDtypeStruct((DEPTH, 8, n), F32),
        compiler_params=_cparams(("parallel", "parallel")),
        name="ada_mod",
    )(cond, ada_w, ada_b.reshape(DEPTH, 1, n))


def _ffn_kernel(x_ref, mod_ref, g_ref, w1_ref, w3_ref, w2_ref, *rest, jbase, nf, final):
    if final:
        fg_ref, o_ref, h_scr = rest
    else:
        o_ref, h_scr = rest
    f = pl.program_id(1)

    @pl.when(f == 0)
    def _():
        h = _norm_mod(x_ref[...], g_ref[...], mod_ref[jbase:jbase + 1, :],
                      mod_ref[jbase + 1:jbase + 2, :])
        h_scr[...] = h.astype(BF16)

    h = h_scr[...]
    a = jnp.dot(h, w1_ref[...], preferred_element_type=F32)
    b = jnp.dot(h, w3_ref[...], preferred_element_type=F32)
    gate = (a * jax.nn.sigmoid(a) * b).astype(BF16)
    part = jnp.dot(gate, w2_ref[...], preferred_element_type=F32)

    @pl.when(f == 0)
    def _():
        o_ref[...] = part

    @pl.when(f > 0)
    def _():
        o_ref[...] += part

    @pl.when(f == nf - 1)
    def _():
        y = x_ref[...] + (0.5 * mod_ref[jbase + 2:jbase + 3, :]) * o_ref[...]
        if final:
            ms = jnp.mean(y * y, axis=-1, keepdims=True)
            y = (y * lax.rsqrt(ms + RMS_EPS)) * fg_ref[...]
        o_ref[...] = y


def _ffn(xs, mods, g, w1, w3, w2, layer, jbase, nblk, final_g=None):
    nf = DFF_PAD // TF
    final = final_g is not None
    in_specs = [
        pl.BlockSpec((TM, D), lambda i, f: (i, 0)),
        pl.BlockSpec((None, None, N_MOD, D), lambda i, f: (layer, _mod_index(i), 0, 0)),
        pl.BlockSpec((1, D), lambda i, f: (0, 0)),
        pl.BlockSpec((None, D, TF), lambda i, f: (layer, 0, f)),
        pl.BlockSpec((None, D, TF), lambda i, f: (layer, 0, f)),
        pl.BlockSpec((None, TF, D), lambda i, f: (layer, f, 0)),
    ]
    args = [xs, mods, g.reshape(1, D), w1, w3, w2]
    if final:
        in_specs.append(pl.BlockSpec((1, D), lambda i, f: (0, 0)))
        args.append(final_g.reshape(1, D))
        out_shape = jax.ShapeDtypeStruct((nblk * TM, D), F32)
        aliases = {}
    else:
        out_shape = jax.ShapeDtypeStruct(xs.shape, F32)
        aliases = {0: 0}
    return pl.pallas_call(
        functools.partial(_ffn_kernel, jbase=jbase, nf=nf, final=final),
        grid=(nblk, nf),
        in_specs=in_specs,
        out_specs=pl.BlockSpec((TM, D), lambda i, f: (i, 0)),
        out_shape=out_shape,
        scratch_shapes=[pltpu.VMEM((TM, D), BF16)],
        input_output_aliases=aliases,
        compiler_params=_cparams(("parallel", "arbitrary")),
        name="ffn",
    )(*args)


def _inproj_kernel(x_ref, mod_ref, g_ref, w_ref, qkv_ref, z_ref, h_scr, *, nq):
    j = pl.program_id(1)

    @pl.when(j == 0)
    def _():
        h = _norm_mod(x_ref[...], g_ref[...], mod_ref[3:4, :], mod_ref[4:5, :])
        h_scr[...] = h.astype(BF16)

    acc = jnp.dot(h_scr[...], w_ref[...], preferred_element_type=F32)

    @pl.when(j == 0)
    def _():
        qkv_ref[...] = (acc * (HD ** -0.5)).astype(BF16)

    @pl.when((j > 0) & (j < nq))
    def _():
        qkv_ref[...] = acc.astype(BF16)

    @pl.when(j >= nq)
    def _():
        z_ref[...] = acc


def _inproj(xs, mods, g, w_in, layer, e, nblk):
    tn = WA
    nq = 3 * WA // tn
    nz = 3 * WB // tn
    return pl.pallas_call(
        functools.partial(_inproj_kernel, nq=nq),
        grid=(nblk, nq + nz),
        in_specs=[
            pl.BlockSpec((TM, D), lambda i, j: (i, 0)),
            pl.BlockSpec((None, None, N_MOD, D), lambda i, j: (layer, _mod_index(i), 0, 0)),
            pl.BlockSpec((1, D), lambda i, j: (0, 0)),
            pl.BlockSpec((None, D, tn), lambda i, j: (e, 0, j)),
        ],
        out_specs=[
            pl.BlockSpec((TM, tn), lambda i, j: (i, jnp.minimum(j, nq - 1))),
            pl.BlockSpec((TM, tn), lambda i, j: (i, jnp.maximum(j - nq, 0))),
        ],
        out_shape=[
            jax.ShapeDtypeStruct((T_ALL, 3 * WA), BF16),
            jax.ShapeDtypeStruct((T_ALL, 3 * WB), F32),
        ],
        scratch_shapes=[pltpu.VMEM((TM, D), BF16)],
        compiler_params=_cparams(("parallel", "arbitrary")),
        name="mix_inproj",
    )(xs, mods, g.reshape(1, D), w_in)


def _pw1_kernel(x_ref, mod_ref, g_ref, wa_ref, wg_ref, ba_ref, bg_ref, u_ref, h_scr):
    j = pl.program_id(1)

    @pl.when(j == 0)
    def _():
        h = _norm_mod(x_ref[...], g_ref[...], mod_ref[3:4, :], mod_ref[4:5, :])
        h_scr[...] = h.astype(BF16)

    h = h_scr[...]
    a = jnp.dot(h, wa_ref[...], preferred_element_type=F32) + ba_ref[...]
    gt = jnp.dot(h, wg_ref[...], preferred_element_type=F32) + bg_ref[...]
    u_ref[...] = a * jax.nn.sigmoid(gt)


def _pw1(xs, mods, g, w_pw1, b_pw1, layer, o, nblk):
    tn = 512
    nj = D // tn
    b3 = b_pw1.reshape(-1, 1, 2 * D)
    return pl.pallas_call(
        _pw1_kernel,
        grid=(nblk, nj),
        in_specs=[
            pl.BlockSpec((TM, D), lambda i, j: (i, 0)),
            pl.BlockSpec((None, None, N_MOD, D), lambda i, j: (layer, _mod_index(i), 0, 0)),
            pl.BlockSpec((1, D), lambda i, j: (0, 0)),
            pl.BlockSpec((None, D, tn), lambda i, j: (o, 0, j)),
            pl.BlockSpec((None, D, tn), lambda i, j: (o, 0, j + nj)),
            pl.BlockSpec((None, 1, tn), lambda i, j: (o, 0, j)),
            pl.BlockSpec((None, 1, tn), lambda i, j: (o, 0, j + nj)),
        ],
        out_specs=pl.BlockSpec((TM, tn), lambda i, j: (i, j)),
        out_shape=jax.ShapeDtypeStruct((T_ALL, D), F32),
        scratch_shapes=[pltpu.VMEM((TM, D), BF16)],
        compiler_params=_cparams(("parallel", "arbitrary")),
        name="conf_pw1",
    )(xs, mods, g.reshape(1, D), w_pw1, w_pw1, b3, b3)


NAT_G = 1
NAT_KR = {1: 8, 2: 10, 4: 12, 8: 16}[NAT_G]


def _natten_geometry():
    gq, kk = NAT_G * GRID_W, NAT_KR * GRID_W
    qi, ki = np.arange(gq), np.arange(kk)
    kstart, types, type_id = [], [], []
    for g in range(GRID_H // NAT_G):
        r0 = g * NAT_G
        ks = min(max(r0 - WIN_R // 2, 0), GRID_H - NAT_KR)
        qrow = (r0 + qi // GRID_W)[:, None]
        qcol = (qi % GRID_W)[:, None]
        krow = (ks + ki // GRID_W)[None, :]
        kcol = (ki % GRID_W)[None, :]
        rs = np.clip(qrow - WIN_R // 2, 0, GRID_H - WIN_R)
        cs = np.clip(qcol - WIN_C // 2, 0, GRID_W - WIN_C)
        valid = (krow >= rs) & (krow < rs + WIN_R) & (kcol >= cs) & (kcol < cs + WIN_C)
        dr = np.where(valid, krow - qrow + WIN_R - 1, 0).astype(np.int32)
        dc = np.where(valid, kcol - qcol + WIN_C - 1, 0).astype(np.int32)
        assert int(valid.sum(axis=1).min()) == WIN_R * WIN_C
        sig = (valid.tobytes(), dr.tobytes(), dc.tobytes())
        for t, (s0, _) in enumerate(types):
            if s0 == sig:
                type_id.append(t)
                break
        else:
            type_id.append(len(types))
            types.append((sig, (valid, dr, dc)))
        kstart.append(ks)
    valid = np.stack([t[1][0] for t in types])
    dr = np.stack([t[1][1] for t in types])
    dc = np.stack([t[1][2] for t in types])
    return (np.asarray(kstart, np.int32), np.asarray(type_id, np.int32), valid, dr, dc)


def _natten_kernel(ks_ref, ty_ref, q_ref, k_ref, v_ref, kc_ref, vc_ref, bias_ref, o_ref):
    gq, kk = NAT_G * GRID_W, NAT_KR * GRID_W
    kc = kc_ref[...]
    vc = vc_ref[...]
    nt = (((1,), (1,)), ((), ()))

    def body(g, carry):
        q0 = pl.multiple_of(g * gq, gq)
        k0 = pl.multiple_of(ks_ref[g] * GRID_W, GRID_W)
        q = q_ref[pl.ds(q0, gq), :]
        k = k_ref[pl.ds(k0, kk), :]
        v = v_ref[pl.ds(k0, kk), :]
        s_loc = lax.dot_general(q, k, nt, preferred_element_type=F32) + bias_ref[ty_ref[g]]
        s_ctx = lax.dot_general(q, kc, nt, preferred_element_type=F32)
        m = jnp.maximum(jnp.max(s_loc, axis=-1, keepdims=True),
                        jnp.max(s_ctx, axis=-1, keepdims=True))
        p_loc = jnp.exp(s_loc - m)
        p_ctx = jnp.exp(s_ctx - m)
        den = jnp.sum(p_loc, axis=-1, keepdims=True) + jnp.sum(p_ctx, axis=-1, keepdims=True)
        o = (jnp.dot(p_loc.astype(BF16), v, preferred_element_type=F32)
             + jnp.dot(p_ctx.astype(BF16), vc, preferred_element_type=F32))
        o_ref[pl.ds(q0, gq), :] = (o / den).astype(BF16)
        return carry

    lax.fori_loop(0, GRID_H // NAT_G, body, 0)


def _natten(qkv, rpb):
    kstart, type_id, valid, dr, dc = _natten_geometry()
    bias = jnp.where(valid[None], rpb[:, dr, dc], NEG)
    ntypes, gq, kk = valid.shape
    cb = T_LAT // LC
    grid_spec = pltpu.PrefetchScalarGridSpec(
        num_scalar_prefetch=2,
        grid=(NB, NH),
        in_specs=[
            pl.BlockSpec((S, HD), lambda b, h, *_: (b, h)),
            pl.BlockSpec((S, HD), lambda b, h, *_: (b, NH + h)),
            pl.BlockSpec((S, HD), lambda b, h, *_: (b, 2 * NH + h)),
            pl.BlockSpec((LC, HD), lambda b, h, *_: (cb + b, NH + h)),
            pl.BlockSpec((LC, HD), lambda b, h, *_: (cb + b, 2 * NH + h)),
            pl.BlockSpec((None, ntypes, gq, kk), lambda b, h, *_: (h, 0, 0, 0)),
        ],
        out_specs=pl.BlockSpec((S, HD), lambda b, h, *_: (b, h)),
    )
    return pl.pallas_call(
        _natten_kernel,
        grid_spec=grid_spec,
        out_shape=jax.ShapeDtypeStruct((T_ALL, WA), BF16),
        compiler_params=_cparams(("parallel", "parallel")),
        name="natten",
    )(jnp.asarray(kstart), jnp.asarray(type_id), qkv, qkv, qkv, qkv, qkv, bias)


def _ctx_attn_kernel(q_ref, k_ref, v_ref, att_hbm, o_ref):
    del att_hbm
    s = lax.dot_general(q_ref[...], k_ref[...], (((1,), (1,)), ((), ())),
                        preferred_element_type=F32)
    m = jnp.max(s, axis=-1, keepdims=True)
    p = jnp.exp(s - m)
    den = jnp.sum(p, axis=-1, keepdims=True)
    o = jnp.dot(p.astype(BF16), v_ref[...], preferred_element_type=F32)
    o_ref[...] = (o / den).astype(BF16)


def _ctx_attn(qkv, att):
    cb = T_LAT // LC
    return pl.pallas_call(
        _ctx_attn_kernel,
        grid=(NB, NH),
        in_specs=[
            pl.BlockSpec((LC, HD), lambda b, h: (cb + b, h)),
            pl.BlockSpec((LC, HD), lambda b, h: (cb + b, NH + h)),
            pl.BlockSpec((LC, HD), lambda b, h: (cb + b, 2 * NH + h)),
            pl.BlockSpec(memory_space=pl.ANY),
        ],
        out_specs=pl.BlockSpec((LC, HD), lambda b, h: (cb + b, h)),
        out_shape=jax.ShapeDtypeStruct(att.shape, att.dtype),
        input_output_aliases={3: 0},
        compiler_params=_cparams(("parallel", "parallel")),
        name="ctx_attn",
    )(qkv, qkv, qkv, att)


def _dft_tables(length):
    n2 = 4 * length
    r = jnp.arange(length, dtype=jnp.int32)[:, None]
    hi = jnp.arange(length // LANE, dtype=jnp.int32)[None, :]
    lo = jnp.arange(LANE, dtype=jnp.int32)[None, :]

    def cs(phase):
        ang = (phase % n2).astype(F32) * (2.0 * math.pi / n2)
        return jnp.cos(ang), jnp.sin(ang)

    kt = cs((2 * r + 1) * (LANE * hi)) + cs((2 * r + 1) * lo)
    tk = cs(r * (2 * LANE * hi)) + cs(r * (2 * lo + 1))
    return kt, tk


def _dftgen_kernel(ca_ref, sa_ref, cb_ref, sb_ref, c_ref, s_ref, *, nt, sgn):
    cb = cb_ref[...]
    sb = sb_ref[...]
    for t1 in range(nt):
        ca = ca_ref[:, t1:t1 + 1]
        sa = sa_ref[:, t1:t1 + 1]
        sl = slice(t1 * LANE, (t1 + 1) * LANE)
        c_ref[:, sl] = (ca * cb - sa * sb).astype(BF16)
        s_ref[:, sl] = (sgn * (sa * cb + ca * sb)).astype(BF16)


def _dftgen(tables, length, sgn):
    tr = min(256, length)
    nt = length // LANE
    row = lambda i: (i, 0)
    return pl.pallas_call(
        functools.partial(_dftgen_kernel, nt=nt, sgn=sgn),
        grid=(length // tr,),
        in_specs=[pl.BlockSpec((tr, nt), row), pl.BlockSpec((tr, nt), row),
                  pl.BlockSpec((tr, LANE), row), pl.BlockSpec((tr, LANE), row)],
        out_specs=[pl.BlockSpec((tr, length), row), pl.BlockSpec((tr, length), row)],
        out_shape=[jax.ShapeDtypeStruct((length, length), BF16)] * 2,
        compiler_params=_cparams(("parallel",)),
        name="dft_gen",
    )(*tables)


def _hy_pre_kernel(z0_ref, z1_ref, z2_ref, w_ref, b_ref, u16_ref, u32_ref, x0_ref):
    def conv(z_ref, part):
        z = z_ref[...]
        n = z.shape[0]
        row = lax.broadcasted_iota(jnp.int32, z.shape, 0)
        zm = jnp.where(row == 0, 0.0, pltpu.roll(z, 1, 0))
        zp = jnp.where(row == n - 1, 0.0, pltpu.roll(z, n - 1, 0))
        w = w_ref[part]
        return zm * w[0:1, :] + z * w[1:2, :] + zp * w[2:3, :] + b_ref[part]

    x0_ref[...] = conv(z0_ref, 0)
    u = conv(z2_ref, 2) * conv(z1_ref, 1)
    u32_ref[...] = u
    u16_ref[...] = u.astype(BF16)


def _hy_pre(z, short_w, short_b, length, row_off):
    tc = LANE
    nc = WB // tc
    rb = row_off // length
    w = short_w.reshape(3, 3, WB).transpose(1, 0, 2)
    bb = short_b.reshape(3, 1, WB)
    out_spec = pl.BlockSpec((length, tc), lambda b, c: (0, b * nc + c))
    return pl.pallas_call(
        _hy_pre_kernel,
        grid=(NB, nc),
        in_specs=[
            pl.BlockSpec((length, tc), lambda b, c: (rb + b, c)),
            pl.BlockSpec((length, tc), lambda b, c: (rb + b, nc + c)),
            pl.BlockSpec((length, tc), lambda b, c: (rb + b, 2 * nc + c)),
            pl.BlockSpec((3, 3, tc), lambda b, c: (0, 0, c)),
            pl.BlockSpec((3, 1, tc), lambda b, c: (0, 0, c)),
        ],
        out_specs=[out_spec, out_spec, out_spec],
        out_shape=[jax.ShapeDtypeStruct((length, NB * WB), BF16),
                   jax.ShapeDtypeStruct((length, NB * WB), F32),
                   jax.ShapeDtypeStruct((length, NB * WB), F32)],
        compiler_params=_cparams(("parallel", "parallel")),
        name="hyena_pre",
    )(z, z, z, w, bb)


def _hy_filter_kernel(z_ref, w0_ref, b0_ref, w1_ref, b1_ref, w2_ref, b2_ref, w3_ref, fr_ref,
                      dl_ref, fs_ref, fd_ref):
    hp = lax.Precision.HIGHEST
    z = z_ref[...]
    fr = fr_ref[...]
    h = jnp.sin(fr * (jnp.dot(z, w0_ref[...], precision=hp, preferred_element_type=F32) + b0_ref[...]))
    h = jnp.sin(fr * (jnp.dot(h, w1_ref[...], precision=hp, preferred_element_type=F32) + b1_ref[...]))
    h = jnp.sin(fr * (jnp.dot(h, w2_ref[...], precision=hp, preferred_element_type=F32) + b2_ref[...]))
    hh = jnp.dot(h, w3_ref[...], precision=hp, preferred_element_type=F32)
    win = jnp.exp(-z[:, 0:1] * dl_ref[...])
    fwd = hh[:, :WB] * win
    bwd = hh[:, WB:] * win
    row = lax.broadcasted_iota(jnp.int32, bwd.shape, 0) + pl.program_id(0) * z.shape[0]
    bwd = jnp.where(row == 0, 0.0, bwd)
    fs_ref[...] = (fwd + bwd).astype(BF16)
    fd_ref[...] = (bwd - fwd).astype(BF16)


def _hy_filter(length, w0, b0, w1, b1, w2, b2, w3, freq):
    t = jnp.linspace(0.0, 1.0, length, dtype=F32)[:, None]
    bands = (HY_EMB - 1) // 2
    f = jnp.linspace(1e-4, bands - 1, bands, dtype=F32)
    w = 2 * math.pi * jnp.arange(length, dtype=F32)[:, None] / length
    z = jnp.concatenate([t, jnp.cos(f * w), -jnp.sin(f * w)], axis=-1)
    emb = HY_ORDER
    z = jnp.pad(z, ((0, 0), (0, emb - HY_EMB)))
    w0p = jnp.pad(w0, ((0, emb - HY_EMB), (0, 0)))
    max_decay = math.log(1e-2) / 0.3
    min_decay = math.log(1e-2) / 1.5
    deltas = jnp.abs(jnp.linspace(min_decay, max_decay, WB, dtype=F32))[None, :]
    tt = min(256, length)
    full = lambda shape: pl.BlockSpec(shape, lambda i: (0,) * len(shape))
    row = lambda i: (i, 0)
    vec = lambda a: a.reshape(1, -1)
    return pl.pallas_call(
        _hy_filter_kernel,
        grid=(length // tt,),
        in_specs=[pl.BlockSpec((tt, emb), row),
                  full((emb, HY_ORDER)), full((1, HY_ORDER)),
                  full((HY_ORDER, HY_ORDER)), full((1, HY_ORDER)),
                  full((HY_ORDER, HY_ORDER)), full((1, HY_ORDER)),
                  full((HY_ORDER, 2 * WB)), full((1, HY_ORDER)), full((1, WB))],
        out_specs=[pl.BlockSpec((tt, WB), row), pl.BlockSpec((tt, WB), row)],
        out_shape=[jax.ShapeDtypeStruct((length, WB), BF16)] * 2,
        compiler_params=_cparams(("parallel",)),
        name="hyena_filter",
    )(z, w0p, vec(b0), w1, vec(b1), w2, vec(b2), w3, vec(freq), deltas)


def _dft_tiles(length):
    return min(512, length), min(1024, length), 1024


def _dft_spec_kernel(c_ref, s_ref, fs_ref, fd_ref, a_ref, b_ref, *, nk, scale):
    kk = pl.program_id(2)
    pa = jnp.dot(c_ref[...], fs_ref[...], preferred_element_type=F32)
    pb = jnp.dot(s_ref[...], fd_ref[...], preferred_element_type=F32)

    @pl.when(kk == 0)
    def _():
        a_ref[...] = pa
        b_ref[...] = pb

    @pl.when(kk > 0)
    def _():
        a_ref[...] += pa
        b_ref[...] += pb

    @pl.when(kk == nk - 1)
    def _():
        a_ref[...] *= scale
        b_ref[...] *= scale


def _dft_spec(cm, sm, fs, fd, length):
    tr, tk, tc = _dft_tiles(length)
    nk = length // tk
    return pl.pallas_call(
        functools.partial(_dft_spec_kernel, nk=nk, scale=1.0 / length),
        grid=(WB // tc, length // tr, nk),
        in_specs=[pl.BlockSpec((tr, tk), lambda j, i, k: (i, k)),
                  pl.BlockSpec((tr, tk), lambda j, i, k: (i, k)),
                  pl.BlockSpec((tk, tc), lambda j, i, k: (k, j)),
                  pl.BlockSpec((tk, tc), lambda j, i, k: (k, j))],
        out_specs=[pl.BlockSpec((tr, tc), lambda j, i, k: (i, j))] * 2,
        out_shape=[jax.ShapeDtypeStruct((length, WB), F32)] * 2,
        compiler_params=_cparams(("parallel", "parallel", "arbitrary")),
        name="hyena_filter_dft",
    )(cm, sm, fs, fd)


def _dft_fwd_kernel(c_ref, s_ref, u_ref, a_ref, b_ref, re_ref, im_ref, p_scr, q_scr, *, nk):
    kk = pl.program_id(2)
    u = u_ref[...]
    pp = jnp.dot(c_ref[...], u, preferred_element_type=F32)
    qq = jnp.dot(s_ref[...], u, preferred_element_type=F32)

    @pl.when(kk == 0)
    def _():
        p_scr[...] = pp
        q_scr[...] = qq

    @pl.when(kk > 0)
    def _():
        p_scr[...] += pp
        q_scr[...] += qq

    @pl.when(kk == nk - 1)
    def _():
        p = p_scr[...]
        q = q_scr[...]
        a = a_ref[...]
        b = b_ref[...]
        re_ref[...] = (p * a + q * b).astype(BF16)
        im_ref[...] = (p * b - q * a).astype(BF16)


def _dft_fwd(cm, sm, u16, a, bq, length):
    tr, tk, tc = _dft_tiles(length)
    nk = length // tk
    ncj = WB // tc
    return pl.pallas_call(
        functools.partial(_dft_fwd_kernel, nk=nk),
        grid=(NB * ncj, length // tr, nk),
        in_specs=[pl.BlockSpec((tr, tk), lambda j, i, k: (i, k)),
                  pl.BlockSpec((tr, tk), lambda j, i, k: (i, k)),
                  pl.BlockSpec((tk, tc), lambda j, i, k: (k, j)),
                  pl.BlockSpec((tr, tc), lambda j, i, k: (i, j % ncj)),
                  pl.BlockSpec((tr, tc), lambda j, i, k: (i, j % ncj))],
        out_specs=[pl.BlockSpec((tr, tc), lambda j, i, k: (i, j))] * 2,
        out_shape=[jax.ShapeDtypeStruct((length, NB * WB), BF16)] * 2,
        scratch_shapes=[pltpu.VMEM((tr, tc), F32), pltpu.VMEM((tr, tc), F32)],
        compiler_params=_cparams(("parallel", "parallel", "arbitrary")),
        name="hyena_fwd_dft",
    )(cm, sm, u16, a, bq)


def _dft_inv_kernel(c_ref, s_ref, re_ref, im_ref, u_ref, x0_ref, bias_ref, *rest, nk, aliased):
    o_ref, y_scr = rest[-2:]
    kk = pl.program_id(2)
    part = (jnp.dot(c_ref[...], re_ref[...], preferred_element_type=F32)
            + jnp.dot(s_ref[...], im_ref[...], preferred_element_type=F32))

    @pl.when(kk == 0)
    def _():
        y_scr[...] = part

    @pl.when(kk > 0)
    def _():
        y_scr[...] += part

    @pl.when(kk == nk - 1)
    def _():
        o_ref[...] = ((y_scr[...] + u_ref[...] * bias_ref[...]) * x0_ref[...]).astype(BF16)


def _dft_inv(ct, snt, re, im, u32, x0, bias, length, row_off, hy=None):
    tr, tk, tc = _dft_tiles(length)
    nk = length // tk
    ncj = WB // tc
    nri = length // tr
    rb = row_off // tr
    aliased = hy is not None
    in_specs = [pl.BlockSpec((tr, tk), lambda j, i, k: (i, k)),
                pl.BlockSpec((tr, tk), lambda j, i, k: (i, k)),
                pl.BlockSpec((tk, tc), lambda j, i, k: (k, j)),
                pl.BlockSpec((tk, tc), lambda j, i, k: (k, j)),
                pl.BlockSpec((tr, tc), lambda j, i, k: (i, j)),
                pl.BlockSpec((tr, tc), lambda j, i, k: (i, j)),
                pl.BlockSpec((1, tc), lambda j, i, k: (0, j % ncj))]
    args = [ct, snt, re, im, u32, x0, bias.reshape(1, WB)]
    if aliased:
        in_specs.append(pl.BlockSpec(memory_space=pl.ANY))
        args.append(hy)
    return pl.pallas_call(
        functools.partial(_dft_inv_kernel, nk=nk, aliased=aliased),
        grid=(NB * ncj, nri, nk),
        in_specs=in_specs,
        out_specs=pl.BlockSpec((tr, tc), lambda j, i, k: (rb + (j // ncj) * nri + i, j % ncj)),
        out_shape=jax.ShapeDtypeStruct((T_ALL, WB), BF16),
        scratch_shapes=[pltpu.VMEM((tr, tc), F32)],
        input_output_aliases={7: 0} if aliased else {},
        compiler_params=_cparams(("parallel", "parallel", "arbitrary")),
        name="hyena_inv_dft",
    )(*args)


def _hyena(z, hy_params, mats, length, row_off, hy=None):
    short_w, short_b, w0, b0, w1, b1, w2, b2, w3, freq, bias = hy_params
    ck, sk, ct, snt = mats
    u16, u32, x0 = _hy_pre(z, short_w, short_b, length, row_off)
    fs, fd = _hy_filter(length, w0, b0, w1, b1, w2, b2, w3, freq)
    a, bq = _dft_spec(ck, sk, fs, fd, length)
    re, im = _dft_fwd(ck, sk, u16, a, bq, length)
    return _dft_inv(ct, snt, re, im, u32, x0, bias, length, row_off, hy)


def _dft_mats(length):
    kt, tk = _dft_tables(length)
    ck, sk = _dftgen(kt, length, 1.0)
    ct, snt = _dftgen(tk, length, -1.0)
    return ck, sk, ct, snt


def _outproj_kernel(x_ref, mod_ref, a_ref, y_ref, wa_ref, wy_ref, o_ref):
    o = (jnp.dot(a_ref[...], wa_ref[...], preferred_element_type=F32)
         + jnp.dot(y_ref[...], wy_ref[...], preferred_element_type=F32))
    o_ref[...] = x_ref[...] + mod_ref[5:6, :] * o


def _outproj(xs, mods, att, hy, w_out, layer, e, nblk):
    return pl.pallas_call(
        _outproj_kernel,
        grid=(nblk,),
        in_specs=[
            pl.BlockSpec((TM, D), lambda i: (i, 0)),
            pl.BlockSpec((None, None, N_MOD, D), lambda i: (layer, _mod_index(i), 0, 0)),
            pl.BlockSpec((TM, WA), lambda i: (i, 0)),
            pl.BlockSpec((TM, WB), lambda i: (i, 0)),
            pl.BlockSpec((None, WA, D), lambda i: (e, 0, 0)),
            pl.BlockSpec((None, WB, D), lambda i: (e, 1, 0)),
        ],
        out_specs=pl.BlockSpec((TM, D), lambda i: (i, 0)),
        out_shape=jax.ShapeDtypeStruct(xs.shape, F32),
        input_output_aliases={0: 0},
        compiler_params=_cparams(("parallel",)),
        name="mix_outproj",
    )(xs, mods, att, hy, w_out, w_out)


CONV_HALO = 16
CONV_RC = 64


def _conf_out_kernel(x_ref, mod_ref, up_ref, uc_ref, un_ref, wdw_ref, bdw_ref, lg_ref, lb_ref,
                     w2_ref, b2_ref, o_ref, ext_scr, cv_scr, *, tm, bps):
    i = pl.program_id(0)
    first = (i % bps) == 0
    last = (i % bps) == bps - 1
    halo = CONV_HALO
    zero = jnp.zeros((halo, D), F32)
    ext_scr[0:halo, :] = jnp.where(first, zero, up_ref[...])
    ext_scr[halo:halo + tm, :] = uc_ref[...]
    ext_scr[halo + tm:2 * halo + tm, :] = jnp.where(last, zero, un_ref[...])
    off = halo - CONV_W // 2

    def col_body(cc, carry):
        c0 = pl.multiple_of(cc * LANE, LANE)
        wv = wdw_ref[:, pl.ds(c0, LANE)]
        bv = bdw_ref[:, pl.ds(c0, LANE)]

        for r0 in range(0, tm, CONV_RC):
            acc = jnp.zeros((CONV_RC, LANE), F32)
            for j in range(CONV_W):
                acc = acc + ext_scr[pl.ds(r0 + off + j, CONV_RC), pl.ds(c0, LANE)] * wv[j:j + 1, :]
            cv_scr[pl.ds(r0, CONV_RC), pl.ds(c0, LANE)] = acc + bv
        return carry

    lax.fori_loop(0, D // LANE, col_body, 0)

    v = cv_scr[...]
    mu = jnp.mean(v, axis=-1, keepdims=True)
    vc = v - mu
    var = jnp.mean(vc * vc, axis=-1, keepdims=True)
    t = vc * lax.rsqrt(var + LN_EPS) * lg_ref[...] + lb_ref[...]
    t = (t * jax.nn.sigmoid(t)).astype(BF16)
    o = jnp.dot(t, w2_ref[...], preferred_element_type=F32) + b2_ref[...]
    o_ref[...] = x_ref[...] + mod_ref[5:6, :] * o


def _conf_out(xs, mods, u, w_dw, b_dw, ln_g, ln_b, w_pw2, b_pw2, layer, o, tm, row_off, seq, nblk):
    bps = seq // tm
    rb = row_off // tm
    tm_ratio = TM // tm
    vec = lambda a: a.reshape(-1, 1, D)
    blk = lambda i: (rb + i, 0)
    hpb = tm // CONV_HALO
    prev = lambda i: (jnp.maximum((rb + i) * hpb - 1, 0), 0)
    nxt = lambda i: (jnp.minimum((rb + i + 1) * hpb, T_ALL // CONV_HALO - 1), 0)
    modmap = lambda i: (layer, _mod_index((rb + i) // tm_ratio), 0, 0)
    return pl.pallas_call(
        functools.partial(_conf_out_kernel, tm=tm, bps=bps),
        grid=(nblk,),
        in_specs=[
            pl.BlockSpec((tm, D), blk),
            pl.BlockSpec((None, None, N_MOD, D), modmap),
            pl.BlockSpec((CONV_HALO, D), prev),
            pl.BlockSpec((tm, D), blk),
            pl.BlockSpec((CONV_HALO, D), nxt),
            pl.BlockSpec((None, CONV_W, D), lambda i: (o, 0, 0)),
            pl.BlockSpec((None, 1, D), lambda i: (o, 0, 0)),
            pl.BlockSpec((None, 1, D), lambda i: (o, 0, 0)),
            pl.BlockSpec((None, 1, D), lambda i: (o, 0, 0)),
            pl.BlockSpec((None, D, D), lambda i: (o, 0, 0)),
            pl.BlockSpec((None, 1, D), lambda i: (o, 0, 0)),
        ],
        out_specs=pl.BlockSpec((tm, D), blk),
        out_shape=jax.ShapeDtypeStruct(xs.shape, F32),
        scratch_shapes=[pltpu.VMEM((tm + 2 * CONV_HALO, D), F32), pltpu.VMEM((tm, D), F32)],
        input_output_aliases={0: 0},
        compiler_params=_cparams(("parallel",)),
        name="conf_out",
    )(xs, mods, u, u, u, w_dw, vec(b_dw), vec(ln_g), vec(ln_b), w_pw2, vec(b_pw2))


def _ffn_weights(w1, w3, w2):
    pad = DFF_PAD - DFF
    return (jnp.pad(w1.astype(BF16), ((0, 0), (0, 0), (0, pad))),
            jnp.pad(w3.astype(BF16), ((0, 0), (0, 0), (0, pad))),
            jnp.pad(w2.astype(BF16), ((0, 0), (0, pad), (0, 0))))


def kernel(x, c, ctx, c_ctx, ada_w, ada_b, norm_g, ff1_w1, ff1_w3, ff1_w2, ff2_w1, ff2_w3, ff2_w2,
           mix_w_in, mix_w_out, na_rpb, hy_short_w, hy_short_b, hy_w0, hy_b0, hy_w1, hy_b1,
           hy_w2, hy_b2, hy_w3, hy_freq, hy_bias, cv_w_pw1, cv_b_pw1, cv_w_dw, cv_b_dw,
           cv_ln_g, cv_ln_b, cv_w_pw2, cv_b_pw2, final_g):
    xs = jnp.concatenate([x.reshape(T_LAT, D), ctx.reshape(T_CTX, D)], axis=0)
    cond = jnp.concatenate([c, c_ctx[None, :], jnp.zeros((8 - NB - 1, D), F32)], axis=0)
    mods = _ada(cond, ada_w, ada_b)[:, :NB + 1].reshape(DEPTH, NB + 1, N_MOD, D)

    f1 = _ffn_weights(ff1_w1, ff1_w3, ff1_w2)
    f2 = _ffn_weights(ff2_w1, ff2_w3, ff2_w2)
    w_in = mix_w_in.astype(BF16)
    w_out = mix_w_out.astype(BF16)
    w_pw1 = cv_w_pw1.astype(BF16)
    w_pw2 = cv_w_pw2.astype(BF16)

    mats_lat = _dft_mats(S)
    mats_ctx = _dft_mats(LC)

    out = None
    for i in range(DEPTH):
        ctx_full = i < LAST_EVEN
        n1 = NBLK_ALL if i <= LAST_EVEN else NBLK_LAT
        n2 = NBLK_ALL if ctx_full else NBLK_LAT
        xs = _ffn(xs, mods, norm_g[i, 0], *f1, i, 0, n1)
        if i % 2 == 0:
            e = i // 2
            hyp = (hy_short_w[e], hy_short_b[e], hy_w0[e], hy_b0[e], hy_w1[e], hy_b1[e],
                   hy_w2[e], hy_b2[e], hy_w3[e], hy_freq[e], hy_bias[e])
            qkv, z = _inproj(xs, mods, norm_g[i, 1], w_in, i, e, n1)
            att = _natten(qkv, na_rpb[e])
            hy = _hyena(z, hyp, mats_lat, S, 0)
            if ctx_full:
                att = _ctx_attn(qkv, att)
                hy = _hyena(z, hyp, mats_ctx, LC, T_LAT, hy)
            xs = _outproj(xs, mods, att, hy, w_out, i, e, n2)
        else:
            o = i // 2
            u = _pw1(xs, mods, norm_g[i, 1], w_pw1, cv_b_pw1, i, o, n2)
            cv = (cv_w_dw, cv_b_dw, cv_ln_g, cv_ln_b, w_pw2, cv_b_pw2)
            xs = _conf_out(xs, mods, u, *cv, i, o, TM, 0, S, NBLK_LAT)
            if ctx_full:
                xs = _conf_out(xs, mods, u, *cv, i, o, LC, T_LAT, LC, T_CTX // LC)
        if i == DEPTH - 1:
            out = _ffn(xs, mods, norm_g[i, 2], *f2, i, 6, n2, final_g=final_g)
        else:
            xs = _ffn(xs, mods, norm_g[i, 2], *f2, i, 6, n2)
    return out.reshape(NB, S, D)
```

```python
import functools
import math

import numpy as np
import jax
import jax.numpy as jnp
from jax import lax
from jax.experimental import pallas as pl
from jax.experimental.pallas import tpu as pltpu

F32 = jnp.float32
BF16 = jnp.bfloat16

D = 2048
NB = 2
S = 4096
DEPTH = 4
GRID_W = 64
GRID_H = S // GRID_W
LC = 256
HD = 128
NH = 8
WA = NH * HD
WB = D - WA
WIN_R = 8
WIN_C = 16
DFF = 5504
HY_EMB = 33
HY_ORDER = 64
CONV_W = 31
N_MOD = 9
RMS_EPS = 1e-6
LN_EPS = 1e-5
LAST_EVEN = (DEPTH - 1) - ((DEPTH - 1) % 2)

T_LAT = NB * S
T_CTX = NB * LC
T_ALL = T_LAT + T_CTX
T_PAD = 9216

LANE = 128
SUBLANE = 8
TM = 512
TM_FFN = 1024
TM_PROJ = 1024
FFN_ROWS = 512
TF = 512
DFF_PAD = TF * (-(-DFF // TF))
NBLK_LAT = T_LAT // TM
NBLK_ALL = T_ALL // TM
NEG = -1e30
VMEM_LIMIT = 60 * 1024 * 1024


def _cparams(sem, vmem=VMEM_LIMIT):
    return pltpu.CompilerParams(dimension_semantics=sem, vmem_limit_bytes=vmem)


def _mod_index(i, tm):
    return jnp.minimum(i // (S // tm), NB)


def _stream_blocks(with_ctx, tm):
    return pl.cdiv(T_ALL if with_ctx else T_LAT, tm)


def _norm_mod(x, g, shift, scale):
    ms = jnp.mean(x * x, axis=-1, keepdims=True)
    return (x * lax.rsqrt(ms + RMS_EPS)) * g * (1.0 + scale) + shift


def _ada_kernel(s_ref, w_ref, b_ref, o_ref):
    s = s_ref[...]
    s = (s * jax.nn.sigmoid(s)).astype(BF16)
    o_ref[...] = jnp.dot(s, w_ref[...].astype(BF16), preferred_element_type=F32) + b_ref[...]


def _ada(cond, ada_w, ada_b):
    tn = 1024
    n = N_MOD * D
    return pl.pallas_call(
        _ada_kernel,
        grid=(DEPTH, n // tn),
        in_specs=[
            pl.BlockSpec((8, D), lambda l, j: (0, 0)),
            pl.BlockSpec((None, D, tn), lambda l, j: (l, 0, j)),
            pl.BlockSpec((None, 1, tn), lambda l, j: (l, 0, j)),
        ],
        out_specs=pl.BlockSpec((None, 8, tn), lambda l, j: (l, 0, j)),
        out_shape=jax.ShapeDtypeStruct((DEPTH, 8, n), F32),
        compiler_params=_cparams(("parallel", "parallel")),
        name="ada_mod",
    )(cond, ada_w, ada_b.reshape(DEPTH, 1, n))


def _ffn_kernel(x_ref, mod_ref, g_ref, w1_ref, w3_ref, w2_ref, *rest, jbase, nf, final):
    if final:
        fg_ref, o_ref, h_scr = rest
    else:
        o_ref, h_scr = rest
    f = pl.program_id(1)

    @pl.when(f == 0)
    def _():
        h = _norm_mod(x_ref[...], g_ref[...], mod_ref[jbase:jbase + 1, :],
                      mod_ref[jbase + 1:jbase + 2, :])
        h_scr[...] = h.astype(BF16)
        o_ref[...] = jnp.zeros(o_ref.shape, F32)

    for r in range(0, h_scr.shape[0], FFN_ROWS):
        h = h_scr[r:r + FFN_ROWS, :]
        a = jnp.dot(h, w1_ref[...], preferred_element_type=F32)
        b = jnp.dot(h, w3_ref[...], preferred_element_type=F32)
        gate = (a * jax.nn.sigmoid(a) * b).astype(BF16)
        o_ref[r:r + FFN_ROWS, :] += jnp.dot(gate, w2_ref[...], preferred_element_type=F32)

    @pl.when(f == nf - 1)
    def _():
        y = x_ref[...] + (0.5 * mod_ref[jbase + 2:jbase + 3, :]) * o_ref[...]
        if final:
            ms = jnp.mean(y * y, axis=-1, keepdims=True)
            y = (y * lax.rsqrt(ms + RMS_EPS)) * fg_ref[...]
        o_ref[...] = y


def _ffn(xs, mods, g, w1, w3, w2, layer, jbase, with_ctx, final_g=None):
    nf = DFF_PAD // TF
    final = final_g is not None
    tm = TM_FFN
    nblk = _stream_blocks(with_ctx, tm)
    in_specs = [
        pl.BlockSpec((tm, D), lambda i, f: (i, 0), pipeline_mode=pl.Buffered(1)),
        pl.BlockSpec((None, None, N_MOD, D),
                     lambda i, f: (layer, _mod_index(i, tm), 0, 0)),
        pl.BlockSpec((1, D), lambda i, f: (0, 0)),
        pl.BlockSpec((None, D, TF), lambda i, f: (layer, 0, f)),
        pl.BlockSpec((None, D, TF), lambda i, f: (layer, 0, f)),
        pl.BlockSpec((None, TF, D), lambda i, f: (layer, f, 0)),
    ]
    args = [xs, mods, g.reshape(1, D), w1, w3, w2]
    if final:
        in_specs.append(pl.BlockSpec((1, D), lambda i, f: (0, 0)))
        args.append(final_g.reshape(1, D))
        out_shape = jax.ShapeDtypeStruct((nblk * tm, D), F32)
        aliases = {}
    else:
        out_shape = jax.ShapeDtypeStruct(xs.shape, F32)
        aliases = {0: 0}
    return pl.pallas_call(
        functools.partial(_ffn_kernel, jbase=jbase, nf=nf, final=final),
        grid=(nblk, nf),
        in_specs=in_specs,
        out_specs=pl.BlockSpec((tm, D), lambda i, f: (i, 0)),
        out_shape=out_shape,
        scratch_shapes=[pltpu.VMEM((tm, D), BF16)],
        input_output_aliases=aliases,
        compiler_params=_cparams(("parallel", "arbitrary")),
        name="ffn",
    )(*args)


def _inproj_kernel(x_ref, mod_ref, g_ref, w_ref, qkv_ref, z_ref, h_scr, *, nq):
    j = pl.program_id(1)

    @pl.when(j == 0)
    def _():
        h = _norm_mod(x_ref[...], g_ref[...], mod_ref[3:4, :], mod_ref[4:5, :])
        h_scr[...] = h.astype(BF16)

    acc = jnp.dot(h_scr[...], w_ref[...], preferred_element_type=F32)

    @pl.when(j == 0)
    def _():
        qkv_ref[...] = (acc * (HD ** -0.5)).astype(BF16)

    @pl.when((j > 0) & (j < nq))
    def _():
        qkv_ref[...] = acc.astype(BF16)

    @pl.when(j >= nq)
    def _():
        z_ref[...] = acc


def _inproj(xs, mods, g, w_in, layer, e, with_ctx):
    tn = WA
    nq = 3 * WA // tn
    nz = 3 * WB // tn
    tm = TM_PROJ
    return pl.pallas_call(
        functools.partial(_inproj_kernel, nq=nq),
        grid=(_stream_blocks(with_ctx, tm), nq + nz),
        in_specs=[
            pl.BlockSpec((tm, D), lambda i, j: (i, 0), pipeline_mode=pl.Buffered(1)),
            pl.BlockSpec((None, None, N_MOD, D), lambda i, j: (layer, _mod_index(i, tm), 0, 0)),
            pl.BlockSpec((1, D), lambda i, j: (0, 0)),
            pl.BlockSpec((None, D, tn), lambda i, j: (e, 0, j)),
        ],
        out_specs=[
            pl.BlockSpec((tm, tn), lambda i, j: (i, jnp.minimum(j, nq - 1))),
            pl.BlockSpec((tm, tn), lambda i, j: (i, jnp.maximum(j - nq, 0))),
        ],
        out_shape=[
            jax.ShapeDtypeStruct((T_PAD, 3 * WA), BF16),
            jax.ShapeDtypeStruct((T_PAD, 3 * WB), F32),
        ],
        scratch_shapes=[pltpu.VMEM((tm, D), BF16)],
        compiler_params=_cparams(("parallel", "arbitrary")),
        name="mix_inproj",
    )(xs, mods, g.reshape(1, D), w_in)


def _pw1_kernel(x_ref, mod_ref, g_ref, wa_ref, wg_ref, ba_ref, bg_ref, u_ref, h_scr):
    j = pl.program_id(1)

    @pl.when(j == 0)
    def _():
        h = _norm_mod(x_ref[...], g_ref[...], mod_ref[3:4, :], mod_ref[4:5, :])
        h_scr[...] = h.astype(BF16)

    h = h_scr[...]
    a = jnp.dot(h, wa_ref[...], preferred_element_type=F32) + ba_ref[...]
    gt = jnp.dot(h, wg_ref[...], preferred_element_type=F32) + bg_ref[...]
    u_ref[...] = a * jax.nn.sigmoid(gt)


def _pw1(xs, mods, g, w_pw1, b_pw1, layer, o, with_ctx):
    tn = 512
    nj = D // tn
    tm = TM_PROJ
    b3 = b_pw1.reshape(-1, 1, 2 * D)
    return pl.pallas_call(
        _pw1_kernel,
        grid=(_stream_blocks(with_ctx, tm), nj),
        in_specs=[
            pl.BlockSpec((tm, D), lambda i, j: (i, 0), pipeline_mode=pl.Buffered(1)),
            pl.BlockSpec((None, None, N_MOD, D), lambda i, j: (layer, _mod_index(i, tm), 0, 0)),
            pl.BlockSpec((1, D), lambda i, j: (0, 0)),
            pl.BlockSpec((None, D, tn), lambda i, j: (o, 0, j)),
            pl.BlockSpec((None, D, tn), lambda i, j: (o, 0, j + nj)),
            pl.BlockSpec((None, 1, tn), lambda i, j: (o, 0, j)),
            pl.BlockSpec((None, 1, tn), lambda i, j: (o, 0, j + nj)),
        ],
        out_specs=pl.BlockSpec((tm, tn), lambda i, j: (i, j)),
        out_shape=jax.ShapeDtypeStruct((T_PAD, D), F32),
        scratch_shapes=[pltpu.VMEM((tm, D), BF16)],
        compiler_params=_cparams(("parallel", "arbitrary")),
        name="conf_pw1",
    )(xs, mods, g.reshape(1, D), w_pw1, w_pw1, b3, b3)


NAT_G = 8
NAT_KR = {1: 8, 2: 10, 4: 12, 8: 16}[NAT_G]


def _natten_geometry():
    kstart, types, type_id = [], [], []
    gi = np.arange(NAT_G)[:, None]
    kj = np.arange(NAT_KR)[None, :]
    for g in range(GRID_H // NAT_G):
        r0 = g * NAT_G
        ks = min(max(r0 - WIN_R // 2, 0), GRID_H - NAT_KR)
        qrow, krow = r0 + gi, ks + kj
        rs = np.clip(qrow - WIN_R // 2, 0, GRID_H - WIN_R)
        valid = (krow >= rs) & (krow < rs + WIN_R)
        assert (valid.sum(axis=1) == WIN_R).all()
        dr = np.where(valid, krow - qrow + WIN_R - 1, 2 * WIN_R - 1).astype(np.int32)
        for t, d0 in enumerate(types):
            if (d0 == dr).all():
                type_id.append(t)
                break
        else:
            type_id.append(len(types))
            types.append(dr)
        kstart.append(ks)
    return np.asarray(kstart, np.int32), np.asarray(type_id, np.int32), np.stack(types)


def _natten_bias(rpb):
    _, _, dr = _natten_geometry()
    ntypes = dr.shape[0]
    c = np.arange(GRID_W)[:, None]
    kc = np.arange(GRID_W)[None, :]
    cs = np.clip(c - WIN_C // 2, 0, GRID_W - WIN_C)
    cvalid = (kc >= cs) & (kc < cs + WIN_C)
    dc = kc - c + WIN_C - 1
    onehot = ((np.arange(2 * WIN_C - 1)[:, None, None] == dc[None]) & cvalid[None]).astype(np.float32)
    colmask = np.where(cvalid, 0.0, NEG).astype(np.float32)
    bcols = jnp.einsum("hrd,dck->hrck", rpb, jnp.asarray(onehot), precision=lax.Precision.HIGHEST)
    bcols = bcols + jnp.asarray(colmask)
    bext = jnp.concatenate([bcols, jnp.full((NH, 1, GRID_W, GRID_W), NEG, F32)], axis=1)
    tab = jnp.take(bext, jnp.asarray(dr.reshape(-1)), axis=1)
    tab = tab.reshape(NH, ntypes, NAT_G, NAT_KR, GRID_W, GRID_W).transpose(0, 1, 2, 4, 3, 5)
    return tab.reshape(NH, ntypes, NAT_G * GRID_W, NAT_KR * GRID_W)


def _natten_kernel(ks_ref, ty_ref, q_ref, k_ref, v_ref, kc_ref, vc_ref, bias_ref, o_ref):
    gq, kk = NAT_G * GRID_W, NAT_KR * GRID_W
    kc = kc_ref[...]
    vc = vc_ref[...]
    nt = (((1,), (1,)), ((), ()))

    def body(g, carry):
        q0 = pl.multiple_of(g * gq, gq)
        k0 = pl.multiple_of(ks_ref[g] * GRID_W, GRID_W)
        q = q_ref[pl.ds(q0, gq), :]
        k = k_ref[pl.ds(k0, kk), :]
        v = v_ref[pl.ds(k0, kk), :]
        s_loc = lax.dot_general(q, k, nt, preferred_element_type=F32) + bias_ref[ty_ref[g]]
        s_ctx = lax.dot_general(q, kc, nt, preferred_element_type=F32)
        m = jnp.maximum(jnp.max(s_loc, axis=-1, keepdims=True),
                        jnp.max(s_ctx, axis=-1, keepdims=True))
        p_loc = jnp.exp(s_loc - m)
        p_ctx = jnp.exp(s_ctx - m)
        den = jnp.sum(p_loc, axis=-1, keepdims=True) + jnp.sum(p_ctx, axis=-1, keepdims=True)
        o = (jnp.dot(p_loc.astype(BF16), v, preferred_element_type=F32)
             + jnp.dot(p_ctx.astype(BF16), vc, preferred_element_type=F32))
        o_ref[pl.ds(q0, gq), :] = (o / den).astype(BF16)
        return carry

    lax.fori_loop(0, GRID_H // NAT_G, body, 0)


def _natten(qkv, rpb):
    kstart, type_id, _ = _natten_geometry()
    bias = _natten_bias(rpb)
    _, ntypes, gq, kk = bias.shape
    cb = T_LAT // LC
    grid_spec = pltpu.PrefetchScalarGridSpec(
        num_scalar_prefetch=2,
        grid=(NB, NH),
        in_specs=[
            pl.BlockSpec((S, HD), lambda b, h, *_: (b, h)),
            pl.BlockSpec((S, HD), lambda b, h, *_: (b, NH + h)),
            pl.BlockSpec((S, HD), lambda b, h, *_: (b, 2 * NH + h)),
            pl.BlockSpec((LC, HD), lambda b, h, *_: (cb + b, NH + h)),
            pl.BlockSpec((LC, HD), lambda b, h, *_: (cb + b, 2 * NH + h)),
            pl.BlockSpec((None, ntypes, gq, kk), lambda b, h, *_: (h, 0, 0, 0)),
        ],
        out_specs=pl.BlockSpec((S, HD), lambda b, h, *_: (b, h)),
    )
    return pl.pallas_call(
        _natten_kernel,
        grid_spec=grid_spec,
        out_shape=jax.ShapeDtypeStruct((T_ALL, WA), BF16),
        compiler_params=_cparams(("parallel", "parallel")),
        name="natten",
    )(jnp.asarray(kstart), jnp.asarray(type_id), qkv, qkv, qkv, qkv, qkv, bias)


def _ctx_attn_kernel(q_ref, k_ref, v_ref, att_hbm, o_ref):
    del att_hbm
    s = lax.dot_general(q_ref[...], k_ref[...], (((1,), (1,)), ((), ())),
                        preferred_element_type=F32)
    m = jnp.max(s, axis=-1, keepdims=True)
    p = jnp.exp(s - m)
    den = jnp.sum(p, axis=-1, keepdims=True)
    o = jnp.dot(p.astype(BF16), v_ref[...], preferred_element_type=F32)
    o_ref[...] = (o / den).astype(BF16)


def _ctx_attn(qkv, att):
    cb = T_LAT // LC
    return pl.pallas_call(
        _ctx_attn_kernel,
        grid=(NB, NH),
        in_specs=[
            pl.BlockSpec((LC, HD), lambda b, h: (cb + b, h)),
            pl.BlockSpec((LC, HD), lambda b, h: (cb + b, NH + h)),
            pl.BlockSpec((LC, HD), lambda b, h: (cb + b, 2 * NH + h)),
            pl.BlockSpec(memory_space=pl.ANY),
        ],
        out_specs=pl.BlockSpec((LC, HD), lambda b, h: (cb + b, h)),
        out_shape=jax.ShapeDtypeStruct(att.shape, att.dtype),
        input_output_aliases={3: 0},
        compiler_params=_cparams(("parallel", "parallel")),
        name="ctx_attn",
    )(qkv, qkv, qkv, att)


def _dft_tables(length):
    n2 = 4 * length
    r = jnp.arange(length, dtype=jnp.int32)[:, None]
    hi = jnp.arange(length // LANE, dtype=jnp.int32)[None, :]
    lo = jnp.arange(LANE, dtype=jnp.int32)[None, :]

    def cs(phase):
        ang = (phase % n2).astype(F32) * (2.0 * math.pi / n2)
        return jnp.cos(ang), jnp.sin(ang)

    kt = cs((2 * r + 1) * (LANE * hi)) + cs((2 * r + 1) * lo)
    tk = cs(r * (2 * LANE * hi)) + cs(r * (2 * lo + 1))
    return kt, tk


def _dftgen_kernel(ca_ref, sa_ref, cb_ref, sb_ref, c_ref, s_ref, *, nt, sgn):
    cb = cb_ref[...]
    sb = sb_ref[...]
    for t1 in range(nt):
        ca = ca_ref[:, t1:t1 + 1]
        sa = sa_ref[:, t1:t1 + 1]
        sl = slice(t1 * LANE, (t1 + 1) * LANE)
        c_ref[:, sl] = (ca * cb - sa * sb).astype(BF16)
        s_ref[:, sl] = (sgn * (sa * cb + ca * sb)).astype(BF16)


def _dftgen(tables, length, sgn):
    tr = min(256, length)
    nt = length // LANE
    row = lambda i: (i, 0)
    return pl.pallas_call(
        functools.partial(_dftgen_kernel, nt=nt, sgn=sgn),
        grid=(length // tr,),
        in_specs=[pl.BlockSpec((tr, nt), row), pl.BlockSpec((tr, nt), row),
                  pl.BlockSpec((tr, LANE), row), pl.BlockSpec((tr, LANE), row)],
        out_specs=[pl.BlockSpec((tr, length), row), pl.BlockSpec((tr, length), row)],
        out_shape=[jax.ShapeDtypeStruct((length, length), BF16)] * 2,
        compiler_params=_cparams(("parallel",)),
        name="dft_gen",
    )(*tables)


def _hy_pre_kernel(z0_ref, z1_ref, z2_ref, w_ref, b_ref, u16_ref, u32_ref, x0_ref):
    def conv(z_ref, part):
        z = z_ref[...]
        n = z.shape[0]
        row = lax.broadcasted_iota(jnp.int32, z.shape, 0)
        zm = jnp.where(row == 0, 0.0, pltpu.roll(z, 1, 0))
        zp = jnp.where(row == n - 1, 0.0, pltpu.roll(z, n - 1, 0))
        w = w_ref[part]
        return zm * w[0:1, :] + z * w[1:2, :] + zp * w[2:3, :] + b_ref[part]

    x0_ref[...] = conv(z0_ref, 0)
    u = conv(z2_ref, 2) * conv(z1_ref, 1)
    u32_ref[...] = u
    u16_ref[...] = u.astype(BF16)


def _hy_pre(z, short_w, short_b, length, row_off):
    tc = LANE
    nc = WB // tc
    rb = row_off // length
    w = short_w.reshape(3, 3, WB).transpose(1, 0, 2)
    bb = short_b.reshape(3, 1, WB)
    out_spec = pl.BlockSpec((length, tc), lambda b, c: (0, b * nc + c))
    return pl.pallas_call(
        _hy_pre_kernel,
        grid=(NB, nc),
        in_specs=[
            pl.BlockSpec((length, tc), lambda b, c: (rb + b, c)),
            pl.BlockSpec((length, tc), lambda b, c: (rb + b, nc + c)),
            pl.BlockSpec((length, tc), lambda b, c: (rb + b, 2 * nc + c)),
            pl.BlockSpec((3, 3, tc), lambda b, c: (0, 0, c)),
            pl.BlockSpec((3, 1, tc), lambda b, c: (0, 0, c)),
        ],
        out_specs=[out_spec, out_spec, out_spec],
        out_shape=[jax.ShapeDtypeStruct((length, NB * WB), BF16),
                   jax.ShapeDtypeStruct((length, NB * WB), F32),
                   jax.ShapeDtypeStruct((length, NB * WB), F32)],
        compiler_params=_cparams(("parallel", "parallel")),
        name="hyena_pre",
    )(z, z, z, w, bb)


def _hy_filter_kernel(z_ref, w0_ref, b0_ref, w1_ref, b1_ref, w2_ref, b2_ref, w3_ref, fr_ref,
                      dl_ref, fs_ref, fd_ref):
    hp = lax.Precision.HIGHEST
    z = z_ref[...]
    fr = fr_ref[...]
    h = jnp.sin(fr * (jnp.dot(z, w0_ref[...], precision=hp, preferred_element_type=F32) + b0_ref[...]))
    h = jnp.sin(fr * (jnp.dot(h, w1_ref[...], precision=hp, preferred_element_type=F32) + b1_ref[...]))
    h = jnp.sin(fr * (jnp.dot(h, w2_ref[...], precision=hp, preferred_element_type=F32) + b2_ref[...]))
    hh = jnp.dot(h, w3_ref[...], precision=hp, preferred_element_type=F32)
    win = jnp.exp(-z[:, 0:1] * dl_ref[...])
    fwd = hh[:, :WB] * win
    bwd = hh[:, WB:] * win
    row = lax.broadcasted_iota(jnp.int32, bwd.shape, 0) + pl.program_id(0) * z.shape[0]
    bwd = jnp.where(row == 0, 0.0, bwd)
    fs_ref[...] = (fwd + bwd).astype(BF16)
    fd_ref[...] = (bwd - fwd).astype(BF16)


def _hy_filter(length, w0, b0, w1, b1, w2, b2, w3, freq):
    t = jnp.linspace(0.0, 1.0, length, dtype=F32)[:, None]
    bands = (HY_EMB - 1) // 2
    f = jnp.linspace(1e-4, bands - 1, bands, dtype=F32)
    w = 2 * math.pi * jnp.arange(length, dtype=F32)[:, None] / length
    z = jnp.concatenate([t, jnp.cos(f * w), -jnp.sin(f * w)], axis=-1)
    emb = HY_ORDER
    z = jnp.pad(z, ((0, 0), (0, emb - HY_EMB)))
    w0p = jnp.pad(w0, ((0, emb - HY_EMB), (0, 0)))
    max_decay = math.log(1e-2) / 0.3
    min_decay = math.log(1e-2) / 1.5
    deltas = jnp.abs(jnp.linspace(min_decay, max_decay, WB, dtype=F32))[None, :]
    tt = min(256, length)
    full = lambda shape: pl.BlockSpec(shape, lambda i: (0,) * len(shape))
    row = lambda i: (i, 0)
    vec = lambda a: a.reshape(1, -1)
    return pl.pallas_call(
        _hy_filter_kernel,
        grid=(length // tt,),
        in_specs=[pl.BlockSpec((tt, emb), row),
                  full((emb, HY_ORDER)), full((1, HY_ORDER)),
                  full((HY_ORDER, HY_ORDER)), full((1, HY_ORDER)),
                  full((HY_ORDER, HY_ORDER)), full((1, HY_ORDER)),
                  full((HY_ORDER, 2 * WB)), full((1, HY_ORDER)), full((1, WB))],
        out_specs=[pl.BlockSpec((tt, WB), row), pl.BlockSpec((tt, WB), row)],
        out_shape=[jax.ShapeDtypeStruct((length, WB), BF16)] * 2,
        compiler_params=_cparams(("parallel",)),
        name="hyena_filter",
    )(z, w0p, vec(b0), w1, vec(b1), w2, vec(b2), w3, vec(freq), deltas)


def _dft_tiles(length):
    return min(512, length), min(1024, length), 1024


def _dft_spec_kernel(c_ref, s_ref, fs_ref, fd_ref, a_ref, b_ref, *, nk, scale):
    kk = pl.program_id(2)
    pa = jnp.dot(c_ref[...], fs_ref[...], preferred_element_type=F32)
    pb = jnp.dot(s_ref[...], fd_ref[...], preferred_element_type=F32)

    @pl.when(kk == 0)
    def _():
        a_ref[...] = pa
        b_ref[...] = pb

    @pl.when(kk > 0)
    def _():
        a_ref[...] += pa
        b_ref[...] += pb

    @pl.when(kk == nk - 1)
    def _():
        a_ref[...] *= scale
        b_ref[...] *= scale


def _dft_spec(cm, sm, fs, fd, length):
    tr, tk, tc = _dft_tiles(length)
    nk = length // tk
    return pl.pallas_call(
        functools.partial(_dft_spec_kernel, nk=nk, scale=1.0 / length),
        grid=(WB // tc, length // tr, nk),
        in_specs=[pl.BlockSpec((tr, tk), lambda j, i, k: (i, k)),
                  pl.BlockSpec((tr, tk), lambda j, i, k: (i, k)),
                  pl.BlockSpec((tk, tc), lambda j, i, k: (k, j)),
                  pl.BlockSpec((tk, tc), lambda j, i, k: (k, j))],
        out_specs=[pl.BlockSpec((tr, tc), lambda j, i, k: (i, j))] * 2,
        out_shape=[jax.ShapeDtypeStruct((length, WB), F32)] * 2,
        compiler_params=_cparams(("parallel", "parallel", "arbitrary")),
        name="hyena_filter_dft",
    )(cm, sm, fs, fd)


def _dft_fwd_kernel(c_ref, s_ref, u_ref, a_ref, b_ref, re_ref, im_ref, p_scr, q_scr, *, nk):
    kk = pl.program_id(2)
    u = u_ref[...]
    pp = jnp.dot(c_ref[...], u, preferred_element_type=F32)
    qq = jnp.dot(s_ref[...], u, preferred_element_type=F32)

    @pl.when(kk == 0)
    def _():
        p_scr[...] = pp
        q_scr[...] = qq

    @pl.when(kk > 0)
    def _():
        p_scr[...] += pp
        q_scr[...] += qq

    @pl.when(kk == nk - 1)
    def _():
        p = p_scr[...]
        q = q_scr[...]
        a = a_ref[...]
        b = b_ref[...]
        re_ref[...] = (p * a + q * b).astype(BF16)
        im_ref[...] = (p * b - q * a).astype(BF16)


def _dft_fwd(cm, sm, u16, a, bq, length):
    tr, tk, tc = _dft_tiles(length)
    nk = length // tk
    ncj = WB // tc
    return pl.pallas_call(
        functools.partial(_dft_fwd_kernel, nk=nk),
        grid=(NB * ncj, length // tr, nk),
        in_specs=[pl.BlockSpec((tr, tk), lambda j, i, k: (i, k)),
                  pl.BlockSpec((tr, tk), lambda j, i, k: (i, k)),
                  pl.BlockSpec((tk, tc), lambda j, i, k: (k, j)),
                  pl.BlockSpec((tr, tc), lambda j, i, k: (i, j % ncj)),
                  pl.BlockSpec((tr, tc), lambda j, i, k: (i, j % ncj))],
        out_specs=[pl.BlockSpec((tr, tc), lambda j, i, k: (i, j))] * 2,
        out_shape=[jax.ShapeDtypeStruct((length, NB * WB), BF16)] * 2,
        scratch_shapes=[pltpu.VMEM((tr, tc), F32), pltpu.VMEM((tr, tc), F32)],
        compiler_params=_cparams(("parallel", "parallel", "arbitrary")),
        name="hyena_fwd_dft",
    )(cm, sm, u16, a, bq)


def _dft_inv_kernel(c_ref, s_ref, re_ref, im_ref, u_ref, x0_ref, bias_ref, *rest, nk, aliased):
    o_ref, y_scr = rest[-2:]
    kk = pl.program_id(2)
    part = (jnp.dot(c_ref[...], re_ref[...], preferred_element_type=F32)
            + jnp.dot(s_ref[...], im_ref[...], preferred_element_type=F32))

    @pl.when(kk == 0)
    def _():
        y_scr[...] = part

    @pl.when(kk > 0)
    def _():
        y_scr[...] += part

    @pl.when(kk == nk - 1)
    def _():
        o_ref[...] = ((y_scr[...] + u_ref[...] * bias_ref[...]) * x0_ref[...]).astype(BF16)


def _dft_inv(ct, snt, re, im, u32, x0, bias, length, row_off, hy=None):
    tr, tk, tc = _dft_tiles(length)
    nk = length // tk
    ncj = WB // tc
    nri = length // tr
    rb = row_off // tr
    aliased = hy is not None
    in_specs = [pl.BlockSpec((tr, tk), lambda j, i, k: (i, k)),
                pl.BlockSpec((tr, tk), lambda j, i, k: (i, k)),
                pl.BlockSpec((tk, tc), lambda j, i, k: (k, j)),
                pl.BlockSpec((tk, tc), lambda j, i, k: (k, j)),
                pl.BlockSpec((tr, tc), lambda j, i, k: (i, j)),
                pl.BlockSpec((tr, tc), lambda j, i, k: (i, j)),
                pl.BlockSpec((1, tc), lambda j, i, k: (0, j % ncj))]
    args = [ct, snt, re, im, u32, x0, bias.reshape(1, WB)]
    if aliased:
        in_specs.append(pl.BlockSpec(memory_space=pl.ANY))
        args.append(hy)
    return pl.pallas_call(
        functools.partial(_dft_inv_kernel, nk=nk, aliased=aliased),
        grid=(NB * ncj, nri, nk),
        in_specs=in_specs,
        out_specs=pl.BlockSpec((tr, tc), lambda j, i, k: (rb + (j // ncj) * nri + i, j % ncj)),
        out_shape=jax.ShapeDtypeStruct((T_ALL, WB), BF16),
        scratch_shapes=[pltpu.VMEM((tr, tc), F32)],
        input_output_aliases={7: 0} if aliased else {},
        compiler_params=_cparams(("parallel", "parallel", "arbitrary")),
        name="hyena_inv_dft",
    )(*args)


def _hyena(z, hy_params, mats, length, row_off, hy=None):
    short_w, short_b, w0, b0, w1, b1, w2, b2, w3, freq, bias = hy_params
    ck, sk, ct, snt = mats
    u16, u32, x0 = _hy_pre(z, short_w, short_b, length, row_off)
    fs, fd = _hy_filter(length, w0, b0, w1, b1, w2, b2, w3, freq)
    a, bq = _dft_spec(ck, sk, fs, fd, length)
    re, im = _dft_fwd(ck, sk, u16, a, bq, length)
    return _dft_inv(ct, snt, re, im, u32, x0, bias, length, row_off, hy)


def _dft_mats(length):
    kt, tk = _dft_tables(length)
    ck, sk = _dftgen(kt, length, 1.0)
    ct, snt = _dftgen(tk, length, -1.0)
    return ck, sk, ct, snt


def _outproj_kernel(x_ref, mod_ref, a_ref, y_ref, wa_ref, wy_ref, o_ref):
    o = (jnp.dot(a_ref[...], wa_ref[...], preferred_element_type=F32)
         + jnp.dot(y_ref[...], wy_ref[...], preferred_element_type=F32))
    o_ref[...] = x_ref[...] + mod_ref[5:6, :] * o


def _outproj(xs, mods, att, hy, w_out, layer, e, nblk):
    return pl.pallas_call(
        _outproj_kernel,
        grid=(nblk,),
        in_specs=[
            pl.BlockSpec((TM, D), lambda i: (i, 0)),
            pl.BlockSpec((None, None, N_MOD, D), lambda i: (layer, _mod_index(i, TM), 0, 0)),
            pl.BlockSpec((TM, WA), lambda i: (i, 0)),
            pl.BlockSpec((TM, WB), lambda i: (i, 0)),
            pl.BlockSpec((None, WA, D), lambda i: (e, 0, 0)),
            pl.BlockSpec((None, WB, D), lambda i: (e, 1, 0)),
        ],
        out_specs=pl.BlockSpec((TM, D), lambda i: (i, 0)),
        out_shape=jax.ShapeDtypeStruct(xs.shape, F32),
        input_output_aliases={0: 0},
        compiler_params=_cparams(("parallel",)),
        name="mix_outproj",
    )(xs, mods, att, hy, w_out, w_out)


CONV_HALO = 16
CONV_RC = 64


def _conf_out_kernel(x_ref, mod_ref, up_ref, uc_ref, un_ref, wdw_ref, bdw_ref, lg_ref, lb_ref,
                     w2_ref, b2_ref, o_ref, ext_scr, cv_scr, ph_scr, *, tm, bps):
    i = pl.program_id(0)
    first = (i % bps) == 0
    last = (i % bps) == bps - 1
    halo = CONV_HALO
    zero = jnp.zeros((halo, D), F32)
    ext_scr[0:halo, :] = jnp.where(first, zero, up_ref[...])
    ext_scr[halo:halo + tm, :] = uc_ref[...]
    ext_scr[halo + tm:2 * halo + tm, :] = jnp.where(last, zero, un_ref[...])
    off = halo - CONV_W // 2

    def col_body(cc, carry):
        c0 = pl.multiple_of(cc * LANE, LANE)
        wv = wdw_ref[:, pl.ds(c0, LANE)]
        bv = bdw_ref[:, pl.ds(c0, LANE)]

        span = ph_scr.shape[1]
        for p in range(1, SUBLANE):
            ph_scr[p] = ext_scr[pl.ds(p, span), pl.ds(c0, LANE)]
        for r0 in range(0, tm, CONV_RC):
            acc = jnp.zeros((CONV_RC, LANE), F32)
            for j in range(CONV_W):
                p = (off + j) % SUBLANE
                base = off + j - p + r0
                if p == 0:
                    rows = ext_scr[pl.ds(base, CONV_RC), pl.ds(c0, LANE)]
                else:
                    rows = ph_scr[p, pl.ds(base, CONV_RC), :]
                acc = acc + rows * wv[j:j + 1, :]
            cv_scr[pl.ds(r0, CONV_RC), pl.ds(c0, LANE)] = acc + bv
        return carry

    lax.fori_loop(0, D // LANE, col_body, 0)

    v = cv_scr[...]
    mu = jnp.mean(v, axis=-1, keepdims=True)
    vc = v - mu
    var = jnp.mean(vc * vc, axis=-1, keepdims=True)
    t = vc * lax.rsqrt(var + LN_EPS) * lg_ref[...] + lb_ref[...]
    t = (t * jax.nn.sigmoid(t)).astype(BF16)
    o = jnp.dot(t, w2_ref[...], preferred_element_type=F32) + b2_ref[...]
    o_ref[...] = x_ref[...] + mod_ref[5:6, :] * o


def _conf_out(xs, mods, u, w_dw, b_dw, ln_g, ln_b, w_pw2, b_pw2, layer, o, tm, row_off, seq, nblk):
    bps = seq // tm
    rb = row_off // tm
    vec = lambda a: a.reshape(-1, 1, D)
    blk = lambda i: (rb + i, 0)
    hpb = tm // CONV_HALO
    prev = lambda i: (jnp.maximum((rb + i) * hpb - 1, 0), 0)
    nxt = lambda i: (jnp.minimum((rb + i + 1) * hpb, T_ALL // CONV_HALO - 1), 0)
    modmap = lambda i: (layer, _mod_index(rb + i, tm), 0, 0)
    return pl.pallas_call(
        functools.partial(_conf_out_kernel, tm=tm, bps=bps),
        grid=(nblk,),
        in_specs=[
            pl.BlockSpec((tm, D), blk),
            pl.BlockSpec((None, None, N_MOD, D), modmap),
            pl.BlockSpec((CONV_HALO, D), prev),
            pl.BlockSpec((tm, D), blk),
            pl.BlockSpec((CONV_HALO, D), nxt),
            pl.BlockSpec((None, CONV_W, D), lambda i: (o, 0, 0)),
            pl.BlockSpec((None, 1, D), lambda i: (o, 0, 0)),
            pl.BlockSpec((None, 1, D), lambda i: (o, 0, 0)),
            pl.BlockSpec((None, 1, D), lambda i: (o, 0, 0)),
            pl.BlockSpec((None, D, D), lambda i: (o, 0, 0)),
            pl.BlockSpec((None, 1, D), lambda i: (o, 0, 0)),
        ],
        out_specs=pl.BlockSpec((tm, D), blk),
        out_shape=jax.ShapeDtypeStruct(xs.shape, F32),
        scratch_shapes=[pltpu.VMEM((tm + 2 * CONV_HALO, D), F32), pltpu.VMEM((tm, D), F32),
                        pltpu.VMEM((SUBLANE, tm + 2 * CONV_HALO - SUBLANE, LANE), F32)],
        input_output_aliases={0: 0},
        compiler_params=_cparams(("parallel",)),
        name="conf_out",
    )(xs, mods, u, u, u, w_dw, vec(b_dw), vec(ln_g), vec(ln_b), w_pw2, vec(b_pw2))


def _ffn_weights(w1, w3, w2):
    pad = DFF_PAD - DFF
    return (jnp.pad(w1, ((0, 0), (0, 0), (0, pad))).astype(BF16),
            jnp.pad(w3, ((0, 0), (0, 0), (0, pad))).astype(BF16),
            jnp.pad(w2, ((0, 0), (0, pad), (0, 0))).astype(BF16))


def kernel(x, c, ctx, c_ctx, ada_w, ada_b, norm_g, ff1_w1, ff1_w3, ff1_w2, ff2_w1, ff2_w3, ff2_w2,
           mix_w_in, mix_w_out, na_rpb, hy_short_w, hy_short_b, hy_w0, hy_b0, hy_w1, hy_b1,
           hy_w2, hy_b2, hy_w3, hy_freq, hy_bias, cv_w_pw1, cv_b_pw1, cv_w_dw, cv_b_dw,
           cv_ln_g, cv_ln_b, cv_w_pw2, cv_b_pw2, final_g):
    xs = jnp.concatenate([x.reshape(T_LAT, D), ctx.reshape(T_CTX, D),
                          jnp.zeros((T_PAD - T_ALL, D), F32)], axis=0)
    cond = jnp.concatenate([c, c_ctx[None, :], jnp.zeros((8 - NB - 1, D), F32)], axis=0)
    mods = _ada(cond, ada_w, ada_b)[:, :NB + 1].reshape(DEPTH, NB + 1, N_MOD, D)

    f1 = _ffn_weights(ff1_w1, ff1_w3, ff1_w2)
    f2 = _ffn_weights(ff2_w1, ff2_w3, ff2_w2)
    w_in = mix_w_in.astype(BF16)
    w_out = mix_w_out.astype(BF16)
    w_pw1 = cv_w_pw1.astype(BF16)
    w_pw2 = cv_w_pw2.astype(BF16)

    mats_lat = _dft_mats(S)
    mats_ctx = _dft_mats(LC)

    out = None
    for i in range(DEPTH):
        ctx_full = i < LAST_EVEN
        xs = _ffn(xs, mods, norm_g[i, 0], *f1, i, 0, i <= LAST_EVEN)
        if i % 2 == 0:
            e = i // 2
            hyp = (hy_short_w[e], hy_short_b[e], hy_w0[e], hy_b0[e], hy_w1[e], hy_b1[e],
                   hy_w2[e], hy_b2[e], hy_w3[e], hy_freq[e], hy_bias[e])
            qkv, z = _inproj(xs, mods, norm_g[i, 1], w_in, i, e, i <= LAST_EVEN)
            att = _natten(qkv, na_rpb[e])
            hy = _hyena(z, hyp, mats_lat, S, 0)
            if ctx_full:
                att = _ctx_attn(qkv, att)
                hy = _hyena(z, hyp, mats_ctx, LC, T_LAT, hy)
            xs = _outproj(xs, mods, att, hy, w_out, i, e, NBLK_ALL if ctx_full else NBLK_LAT)
        else:
            o = i // 2
            u = _pw1(xs, mods, norm_g[i, 1], w_pw1, cv_b_pw1, i, o, ctx_full)
            cv = (cv_w_dw, cv_b_dw, cv_ln_g, cv_ln_b, w_pw2, cv_b_pw2)
            xs = _conf_out(xs, mods, u, *cv, i, o, TM, 0, S, NBLK_LAT)
            if ctx_full:
                xs = _conf_out(xs, mods, u, *cv, i, o, LC, T_LAT, LC, T_CTX // LC)
        if i == DEPTH - 1:
            out = _ffn(xs, mods, norm_g[i, 2], *f2, i, 6, ctx_full, final_g=final_g)
        else:
            xs = _ffn(xs, mods, norm_g[i, 2], *f2, i, 6, ctx_full)
    return out.reshape(NB, S, D)
```

```python
import functools
import math

import numpy as np
import jax
import jax.numpy as jnp
from jax import lax
from jax.experimental import pallas as pl
from jax.experimental.pallas import tpu as pltpu

F32 = jnp.float32
BF16 = jnp.bfloat16

D = 2048
NB = 2
S = 4096
DEPTH = 4
GRID_W = 64
GRID_H = S // GRID_W
LC = 256
HD = 128
NH = 8
WA = NH * HD
WB = D - WA
WIN_R = 8
WIN_C = 16
DFF = 5504
HY_EMB = 33
HY_ORDER = 64
CONV_W = 31
N_MOD = 9
RMS_EPS = 1e-6
LN_EPS = 1e-5
LAST_EVEN = (DEPTH - 1) - ((DEPTH - 1) % 2)

T_LAT = NB * S
T_CTX = NB * LC
T_ALL = T_LAT + T_CTX
T_PAD = 9216

LANE = 128
SUBLANE = 8
TM = 512
TM_FFN = 1024
TM_PROJ = 1024
FFN_ROWS = 512
TF = 512
DFF_PAD = TF * (-(-DFF // TF))
NBLK_LAT = T_LAT // TM
NBLK_ALL = T_ALL // TM
NEG = -1e30
VMEM_LIMIT = 60 * 1024 * 1024


def _cparams(sem, vmem=VMEM_LIMIT):
    return pltpu.CompilerParams(dimension_semantics=sem, vmem_limit_bytes=vmem)


def _mod_index(i, tm):
    return jnp.minimum(i // (S // tm), NB)


def _stream_blocks(with_ctx, tm):
    return pl.cdiv(T_ALL if with_ctx else T_LAT, tm)


def _norm_mod(x, g, shift, scale):
    ms = jnp.mean(x * x, axis=-1, keepdims=True)
    return (x * lax.rsqrt(ms + RMS_EPS)) * g * (1.0 + scale) + shift


def _ada_kernel(s_ref, w_ref, b_ref, o_ref):
    s = s_ref[...]
    s = (s * jax.nn.sigmoid(s)).astype(BF16)
    o_ref[...] = jnp.dot(s, w_ref[...].astype(BF16), preferred_element_type=F32) + b_ref[...]


def _ada(cond, ada_w, ada_b):
    tn = 1024
    n = N_MOD * D
    return pl.pallas_call(
        _ada_kernel,
        grid=(DEPTH, n // tn),
        in_specs=[
            pl.BlockSpec((8, D), lambda l, j: (0, 0)),
            pl.BlockSpec((None, D, tn), lambda l, j: (l, 0, j)),
            pl.BlockSpec((None, 1, tn), lambda l, j: (l, 0, j)),
        ],
        out_specs=pl.BlockSpec((None, 8, tn), lambda l, j: (l, 0, j)),
        out_shape=jax.ShapeDtypeStruct((DEPTH, 8, n), F32),
        compiler_params=_cparams(("parallel", "parallel")),
        name="ada_mod",
    )(cond, ada_w, ada_b.reshape(DEPTH, 1, n))


def _ffn_kernel(x_ref, mod_ref, g_ref, w1_ref, w3_ref, w2_ref, *rest, jbase, nf, final):
    if final:
        fg_ref, o_ref, h_scr = rest
    else:
        o_ref, h_scr = rest
    f = pl.program_id(1)

    @pl.when(f == 0)
    def _():
        h = _norm_mod(x_ref[...], g_ref[...], mod_ref[jbase:jbase + 1, :],
                      mod_ref[jbase + 1:jbase + 2, :])
        h_scr[...] = h.astype(BF16)
        o_ref[...] = jnp.zeros(o_ref.shape, F32)

    for r in range(0, h_scr.shape[0], FFN_ROWS):
        h = h_scr[r:r + FFN_ROWS, :]
        a = jnp.dot(h, w1_ref[...], preferred_element_type=F32)
        b = jnp.dot(h, w3_ref[...], preferred_element_type=F32)
        gate = (a * jax.nn.sigmoid(a) * b).astype(BF16)
        o_ref[r:r + FFN_ROWS, :] += jnp.dot(gate, w2_ref[...], preferred_element_type=F32)

    @pl.when(f == nf - 1)
    def _():
        y = x_ref[...] + (0.5 * mod_ref[jbase + 2:jbase + 3, :]) * o_ref[...]
        if final:
            ms = jnp.mean(y * y, axis=-1, keepdims=True)
            y = (y * lax.rsqrt(ms + RMS_EPS)) * fg_ref[...]
        o_ref[...] = y


def _ffn(xs, mods, g, w1, w3, w2, layer, jbase, with_ctx, final_g=None):
    nf = DFF_PAD // TF
    final = final_g is not None
    tm = TM_FFN
    nblk = _stream_blocks(with_ctx, tm)
    in_specs = [
        pl.BlockSpec((tm, D), lambda i, f: (i, 0)),
        pl.BlockSpec((None, None, N_MOD, D),
                     lambda i, f: (layer, _mod_index(i, tm), 0, 0)),
        pl.BlockSpec((1, D), lambda i, f: (0, 0)),
        pl.BlockSpec((None, D, TF), lambda i, f: (layer, 0, f)),
        pl.BlockSpec((None, D, TF), lambda i, f: (layer, 0, f)),
        pl.BlockSpec((None, TF, D), lambda i, f: (layer, f, 0)),
    ]
    args = [xs, mods, g.reshape(1, D), w1, w3, w2]
    if final:
        in_specs.append(pl.BlockSpec((1, D), lambda i, f: (0, 0)))
        args.append(final_g.reshape(1, D))
        out_shape = jax.ShapeDtypeStruct((nblk * tm, D), F32)
        aliases = {}
    else:
        out_shape = jax.ShapeDtypeStruct(xs.shape, F32)
        aliases = {0: 0}
    return pl.pallas_call(
        functools.partial(_ffn_kernel, jbase=jbase, nf=nf, final=final),
        grid=(nblk, nf),
        in_specs=in_specs,
        out_specs=pl.BlockSpec((tm, D), lambda i, f: (i, 0)),
        out_shape=out_shape,
        scratch_shapes=[pltpu.VMEM((tm, D), BF16)],
        input_output_aliases=aliases,
        compiler_params=_cparams(("parallel", "arbitrary")),
        name="ffn",
    )(*args)


def _inproj_kernel(x_ref, mod_ref, g_ref, w_ref, qkv_ref, z_ref, h_scr, *, nq):
    j = pl.program_id(1)

    @pl.when(j == 0)
    def _():
        h = _norm_mod(x_ref[...], g_ref[...], mod_ref[3:4, :], mod_ref[4:5, :])
        h_scr[...] = h.astype(BF16)

    acc = jnp.dot(h_scr[...], w_ref[...], preferred_element_type=F32)

    @pl.when(j == 0)
    def _():
        qkv_ref[...] = (acc * (HD ** -0.5)).astype(BF16)

    @pl.when((j > 0) & (j < nq))
    def _():
        qkv_ref[...] = acc.astype(BF16)

    @pl.when(j >= nq)
    def _():
        z_ref[...] = acc


def _inproj(xs, mods, g, w_in, layer, e, with_ctx):
    tn = WA
    nq = 3 * WA // tn
    nz = 3 * WB // tn
    tm = TM_PROJ
    return pl.pallas_call(
        functools.partial(_inproj_kernel, nq=nq),
        grid=(_stream_blocks(with_ctx, tm), nq + nz),
        in_specs=[
            pl.BlockSpec((tm, D), lambda i, j: (i, 0)),
            pl.BlockSpec((None, None, N_MOD, D), lambda i, j: (layer, _mod_index(i, tm), 0, 0)),
            pl.BlockSpec((1, D), lambda i, j: (0, 0)),
            pl.BlockSpec((None, D, tn), lambda i, j: (e, 0, j)),
        ],
        out_specs=[
            pl.BlockSpec((tm, tn), lambda i, j: (i, jnp.minimum(j, nq - 1))),
            pl.BlockSpec((tm, tn), lambda i, j: (i, jnp.maximum(j - nq, 0))),
        ],
        out_shape=[
            jax.ShapeDtypeStruct((T_PAD, 3 * WA), BF16),
            jax.ShapeDtypeStruct((T_PAD, 3 * WB), F32),
        ],
        scratch_shapes=[pltpu.VMEM((tm, D), BF16)],
        compiler_params=_cparams(("parallel", "arbitrary")),
        name="mix_inproj",
    )(xs, mods, g.reshape(1, D), w_in)


def _pw1_kernel(x_ref, mod_ref, g_ref, wa_ref, wg_ref, ba_ref, bg_ref, u_ref, h_scr):
    j = pl.program_id(1)

    @pl.when(j == 0)
    def _():
        h = _norm_mod(x_ref[...], g_ref[...], mod_ref[3:4, :], mod_ref[4:5, :])
        h_scr[...] = h.astype(BF16)

    h = h_scr[...]
    a = jnp.dot(h, wa_ref[...], preferred_element_type=F32) + ba_ref[...]
    gt = jnp.dot(h, wg_ref[...], preferred_element_type=F32) + bg_ref[...]
    u_ref[...] = a * jax.nn.sigmoid(gt)


def _pw1(xs, mods, g, w_pw1, b_pw1, layer, o, with_ctx):
    tn = 512
    nj = D // tn
    tm = TM_PROJ
    b3 = b_pw1.reshape(-1, 1, 2 * D)
    return pl.pallas_call(
        _pw1_kernel,
        grid=(_stream_blocks(with_ctx, tm), nj),
        in_specs=[
            pl.BlockSpec((tm, D), lambda i, j: (i, 0)),
            pl.BlockSpec((None, None, N_MOD, D), lambda i, j: (layer, _mod_index(i, tm), 0, 0)),
            pl.BlockSpec((1, D), lambda i, j: (0, 0)),
            pl.BlockSpec((None, D, tn), lambda i, j: (o, 0, j)),
            pl.BlockSpec((None, D, tn), lambda i, j: (o, 0, j + nj)),
            pl.BlockSpec((None, 1, tn), lambda i, j: (o, 0, j)),
            pl.BlockSpec((None, 1, tn), lambda i, j: (o, 0, j + nj)),
        ],
        out_specs=pl.BlockSpec((tm, tn), lambda i, j: (i, j)),
        out_shape=jax.ShapeDtypeStruct((T_PAD, D), F32),
        scratch_shapes=[pltpu.VMEM((tm, D), BF16)],
        compiler_params=_cparams(("parallel", "arbitrary")),
        name="conf_pw1",
    )(xs, mods, g.reshape(1, D), w_pw1, w_pw1, b3, b3)


NAT_G = 8
NAT_KR = {1: 8, 2: 10, 4: 12, 8: 16}[NAT_G]


def _natten_geometry():
    kstart, types, type_id = [], [], []
    gi = np.arange(NAT_G)[:, None]
    kj = np.arange(NAT_KR)[None, :]
    for g in range(GRID_H // NAT_G):
        r0 = g * NAT_G
        ks = min(max(r0 - WIN_R // 2, 0), GRID_H - NAT_KR)
        qrow, krow = r0 + gi, ks + kj
        rs = np.clip(qrow - WIN_R // 2, 0, GRID_H - WIN_R)
        valid = (krow >= rs) & (krow < rs + WIN_R)
        assert (valid.sum(axis=1) == WIN_R).all()
        dr = np.where(valid, krow - qrow + WIN_R - 1, 2 * WIN_R - 1).astype(np.int32)
        for t, d0 in enumerate(types):
            if (d0 == dr).all():
                type_id.append(t)
                break
        else:
            type_id.append(len(types))
            types.append(dr)
        kstart.append(ks)
    return np.asarray(kstart, np.int32), np.asarray(type_id, np.int32), np.stack(types)


def _natten_bias(rpb):
    _, _, dr = _natten_geometry()
    ntypes = dr.shape[0]
    c = np.arange(GRID_W)[:, None]
    kc = np.arange(GRID_W)[None, :]
    cs = np.clip(c - WIN_C // 2, 0, GRID_W - WIN_C)
    cvalid = (kc >= cs) & (kc < cs + WIN_C)
    dc = kc - c + WIN_C - 1
    onehot = ((np.arange(2 * WIN_C - 1)[:, None, None] == dc[None]) & cvalid[None]).astype(np.float32)
    colmask = np.where(cvalid, 0.0, NEG).astype(np.float32)
    bcols = jnp.einsum("hrd,dck->hrck", rpb, jnp.asarray(onehot), precision=lax.Precision.HIGHEST)
    bcols = (bcols + jnp.asarray(colmask)).transpose(0, 2, 1, 3)
    slabs = []
    for ty in range(ntypes):
        for gi in range(NAT_G):
            kj = np.nonzero(dr[ty, gi] != 2 * WIN_R - 1)[0]
            a, r0 = int(kj[0]), int(dr[ty, gi, kj[0]])
            assert (kj == a + np.arange(WIN_R)).all() and (dr[ty, gi, kj] == r0 + np.arange(WIN_R)).all()
            s = bcols[:, :, r0:r0 + WIN_R, :].reshape(NH, GRID_W, WIN_R * GRID_W)
            slabs.append(jnp.pad(s, ((0, 0), (0, 0), (a * GRID_W, (NAT_KR - a - WIN_R) * GRID_W)),
                                 constant_values=NEG))
    return jnp.stack(slabs, axis=1).reshape(NH, ntypes, NAT_G * GRID_W, NAT_KR * GRID_W)


def _natten_kernel(ks_ref, ty_ref, q_ref, k_ref, v_ref, kc_ref, vc_ref, bias_ref, o_ref):
    gq, kk = NAT_G * GRID_W, NAT_KR * GRID_W
    kc = kc_ref[...]
    vc = vc_ref[...]
    nt = (((1,), (1,)), ((), ()))

    def body(g, carry):
        q0 = pl.multiple_of(g * gq, gq)
        k0 = pl.multiple_of(ks_ref[g] * GRID_W, GRID_W)
        q = q_ref[pl.ds(q0, gq), :]
        k = k_ref[pl.ds(k0, kk), :]
        v = v_ref[pl.ds(k0, kk), :]
        s_loc = lax.dot_general(q, k, nt, preferred_element_type=F32) + bias_ref[ty_ref[g]]
        s_ctx = lax.dot_general(q, kc, nt, preferred_element_type=F32)
        m = jnp.maximum(jnp.max(s_loc, axis=-1, keepdims=True),
                        jnp.max(s_ctx, axis=-1, keepdims=True))
        p_loc = jnp.exp(s_loc - m)
        p_ctx = jnp.exp(s_ctx - m)
        den = jnp.sum(p_loc, axis=-1, keepdims=True) + jnp.sum(p_ctx, axis=-1, keepdims=True)
        o = (jnp.dot(p_loc.astype(BF16), v, preferred_element_type=F32)
             + jnp.dot(p_ctx.astype(BF16), vc, preferred_element_type=F32))
        o_ref[pl.ds(q0, gq), :] = (o / den).astype(BF16)
        return carry

    lax.fori_loop(0, GRID_H // NAT_G, body, 0, unroll=4)


def _natten(qkv, rpb):
    kstart, type_id, _ = _natten_geometry()
    bias = _natten_bias(rpb)
    _, ntypes, gq, kk = bias.shape
    cb = T_LAT // LC
    grid_spec = pltpu.PrefetchScalarGridSpec(
        num_scalar_prefetch=2,
        grid=(NB, NH),
        in_specs=[
            pl.BlockSpec((S, HD), lambda b, h, *_: (b, h)),
            pl.BlockSpec((S, HD), lambda b, h, *_: (b, NH + h)),
            pl.BlockSpec((S, HD), lambda b, h, *_: (b, 2 * NH + h)),
            pl.BlockSpec((LC, HD), lambda b, h, *_: (cb + b, NH + h)),
            pl.BlockSpec((LC, HD), lambda b, h, *_: (cb + b, 2 * NH + h)),
            pl.BlockSpec((None, ntypes, gq, kk), lambda b, h, *_: (h, 0, 0, 0)),
        ],
        out_specs=pl.BlockSpec((S, HD), lambda b, h, *_: (b, h)),
    )
    return pl.pallas_call(
        _natten_kernel,
        grid_spec=grid_spec,
        out_shape=jax.ShapeDtypeStruct((T_ALL, WA), BF16),
        compiler_params=_cparams(("parallel", "parallel")),
        name="natten",
    )(jnp.asarray(kstart), jnp.asarray(type_id), qkv, qkv, qkv, qkv, qkv, bias)


def _ctx_attn_kernel(q_ref, k_ref, v_ref, att_hbm, o_ref):
    del att_hbm
    s = lax.dot_general(q_ref[...], k_ref[...], (((1,), (1,)), ((), ())),
                        preferred_element_type=F32)
    m = jnp.max(s, axis=-1, keepdims=True)
    p = jnp.exp(s - m)
    den = jnp.sum(p, axis=-1, keepdims=True)
    o = jnp.dot(p.astype(BF16), v_ref[...], preferred_element_type=F32)
    o_ref[...] = (o / den).astype(BF16)


def _ctx_attn(qkv, att):
    cb = T_LAT // LC
    return pl.pallas_call(
        _ctx_attn_kernel,
        grid=(NB, NH),
        in_specs=[
            pl.BlockSpec((LC, HD), lambda b, h: (cb + b, h)),
            pl.BlockSpec((LC, HD), lambda b, h: (cb + b, NH + h)),
            pl.BlockSpec((LC, HD), lambda b, h: (cb + b, 2 * NH + h)),
            pl.BlockSpec(memory_space=pl.ANY),
        ],
        out_specs=pl.BlockSpec((LC, HD), lambda b, h: (cb + b, h)),
        out_shape=jax.ShapeDtypeStruct(att.shape, att.dtype),
        input_output_aliases={3: 0},
        compiler_params=_cparams(("parallel", "parallel")),
        name="ctx_attn",
    )(qkv, qkv, qkv, att)


def _dft_tables(length):
    n2 = 4 * length
    r = jnp.arange(length, dtype=jnp.int32)[:, None]
    hi = jnp.arange(length // LANE, dtype=jnp.int32)[None, :]
    lo = jnp.arange(LANE, dtype=jnp.int32)[None, :]

    def cs(phase):
        ang = (phase % n2).astype(F32) * (2.0 * math.pi / n2)
        return jnp.cos(ang), jnp.sin(ang)

    kt = cs((2 * r + 1) * (LANE * hi)) + cs((2 * r + 1) * lo)
    tk = cs(r * (2 * LANE * hi)) + cs(r * (2 * lo + 1))
    return kt, tk


def _dftgen_kernel(ca_ref, sa_ref, cb_ref, sb_ref, c_ref, s_ref, *, nt, sgn):
    cb = cb_ref[...]
    sb = sb_ref[...]
    for t1 in range(nt):
        ca = ca_ref[:, t1:t1 + 1]
        sa = sa_ref[:, t1:t1 + 1]
        sl = slice(t1 * LANE, (t1 + 1) * LANE)
        c_ref[:, sl] = (ca * cb - sa * sb).astype(BF16)
        s_ref[:, sl] = (sgn * (sa * cb + ca * sb)).astype(BF16)


def _dftgen(tables, length, sgn):
    tr = min(256, length)
    nt = length // LANE
    row = lambda i: (i, 0)
    return pl.pallas_call(
        functools.partial(_dftgen_kernel, nt=nt, sgn=sgn),
        grid=(length // tr,),
        in_specs=[pl.BlockSpec((tr, nt), row), pl.BlockSpec((tr, nt), row),
                  pl.BlockSpec((tr, LANE), row), pl.BlockSpec((tr, LANE), row)],
        out_specs=[pl.BlockSpec((tr, length), row), pl.BlockSpec((tr, length), row)],
        out_shape=[jax.ShapeDtypeStruct((length, length), BF16)] * 2,
        compiler_params=_cparams(("parallel",)),
        name="dft_gen",
    )(*tables)


def _hy_pre_kernel(z0_ref, z1_ref, z2_ref, w_ref, b_ref, u16_ref, u32_ref, x0_ref):
    def conv(z_ref, part):
        z = z_ref[...]
        n = z.shape[0]
        row = lax.broadcasted_iota(jnp.int32, z.shape, 0)
        zm = jnp.where(row == 0, 0.0, pltpu.roll(z, 1, 0))
        zp = jnp.where(row == n - 1, 0.0, pltpu.roll(z, n - 1, 0))
        w = w_ref[part]
        return zm * w[0:1, :] + z * w[1:2, :] + zp * w[2:3, :] + b_ref[part]

    x0_ref[...] = conv(z0_ref, 0)
    u = conv(z2_ref, 2) * conv(z1_ref, 1)
    u32_ref[...] = u
    u16_ref[...] = u.astype(BF16)


def _hy_pre(z, short_w, short_b, length, row_off):
    tc = LANE
    nc = WB // tc
    rb = row_off // length
    w = short_w.reshape(3, 3, WB).transpose(1, 0, 2)
    bb = short_b.reshape(3, 1, WB)
    out_spec = pl.BlockSpec((length, tc), lambda b, c: (0, b * nc + c))
    return pl.pallas_call(
        _hy_pre_kernel,
        grid=(NB, nc),
        in_specs=[
            pl.BlockSpec((length, tc), lambda b, c: (rb + b, c)),
            pl.BlockSpec((length, tc), lambda b, c: (rb + b, nc + c)),
            pl.BlockSpec((length, tc), lambda b, c: (rb + b, 2 * nc + c)),
            pl.BlockSpec((3, 3, tc), lambda b, c: (0, 0, c)),
            pl.BlockSpec((3, 1, tc), lambda b, c: (0, 0, c)),
        ],
        out_specs=[out_spec, out_spec, out_spec],
        out_shape=[jax.ShapeDtypeStruct((length, NB * WB), BF16),
                   jax.ShapeDtypeStruct((length, NB * WB), F32),
                   jax.ShapeDtypeStruct((length, NB * WB), F32)],
        compiler_params=_cparams(("parallel", "parallel")),
        name="hyena_pre",
    )(z, z, z, w, bb)


def _hy_filter_kernel(z_ref, w0_ref, b0_ref, w1_ref, b1_ref, w2_ref, b2_ref, w3_ref, fr_ref,
                      dl_ref, fs_ref, fd_ref):
    hp = lax.Precision.HIGHEST
    z = z_ref[...]
    fr = fr_ref[...]
    h = jnp.sin(fr * (jnp.dot(z, w0_ref[...], precision=hp, preferred_element_type=F32) + b0_ref[...]))
    h = jnp.sin(fr * (jnp.dot(h, w1_ref[...], precision=hp, preferred_element_type=F32) + b1_ref[...]))
    h = jnp.sin(fr * (jnp.dot(h, w2_ref[...], precision=hp, preferred_element_type=F32) + b2_ref[...]))
    hh = jnp.dot(h, w3_ref[...], precision=hp, preferred_element_type=F32)
    win = jnp.exp(-z[:, 0:1] * dl_ref[...])
    fwd = hh[:, :WB] * win
    bwd = hh[:, WB:] * win
    row = lax.broadcasted_iota(jnp.int32, bwd.shape, 0) + pl.program_id(0) * z.shape[0]
    bwd = jnp.where(row == 0, 0.0, bwd)
    fs_ref[...] = (fwd + bwd).astype(BF16)
    fd_ref[...] = (bwd - fwd).astype(BF16)


def _hy_filter(length, w0, b0, w1, b1, w2, b2, w3, freq):
    t = jnp.linspace(0.0, 1.0, length, dtype=F32)[:, None]
    bands = (HY_EMB - 1) // 2
    f = jnp.linspace(1e-4, bands - 1, bands, dtype=F32)
    w = 2 * math.pi * jnp.arange(length, dtype=F32)[:, None] / length
    z = jnp.concatenate([t, jnp.cos(f * w), -jnp.sin(f * w)], axis=-1)
    emb = HY_ORDER
    z = jnp.pad(z, ((0, 0), (0, emb - HY_EMB)))
    w0p = jnp.pad(w0, ((0, emb - HY_EMB), (0, 0)))
    max_decay = math.log(1e-2) / 0.3
    min_decay = math.log(1e-2) / 1.5
    deltas = jnp.abs(jnp.linspace(min_decay, max_decay, WB, dtype=F32))[None, :]
    tt = min(256, length)
    full = lambda shape: pl.BlockSpec(shape, lambda i: (0,) * len(shape))
    row = lambda i: (i, 0)
    vec = lambda a: a.reshape(1, -1)
    return pl.pallas_call(
        _hy_filter_kernel,
        grid=(length // tt,),
        in_specs=[pl.BlockSpec((tt, emb), row),
                  full((emb, HY_ORDER)), full((1, HY_ORDER)),
                  full((HY_ORDER, HY_ORDER)), full((1, HY_ORDER)),
                  full((HY_ORDER, HY_ORDER)), full((1, HY_ORDER)),
                  full((HY_ORDER, 2 * WB)), full((1, HY_ORDER)), full((1, WB))],
        out_specs=[pl.BlockSpec((tt, WB), row), pl.BlockSpec((tt, WB), row)],
        out_shape=[jax.ShapeDtypeStruct((length, WB), BF16)] * 2,
        compiler_params=_cparams(("parallel",)),
        name="hyena_filter",
    )(z, w0p, vec(b0), w1, vec(b1), w2, vec(b2), w3, vec(freq), deltas)


def _dft_tiles(length):
    return min(512, length), 512


def _dft_spec_kernel(c_ref, s_ref, fs_ref, fd_ref, a_ref, b_ref, *, scale):
    a_ref[...] = jnp.dot(c_ref[...], fs_ref[...], preferred_element_type=F32) * scale
    b_ref[...] = jnp.dot(s_ref[...], fd_ref[...], preferred_element_type=F32) * scale


def _dft_spec(cm, sm, fs, fd, length):
    tr, tc = _dft_tiles(length)
    return pl.pallas_call(
        functools.partial(_dft_spec_kernel, scale=1.0 / length),
        grid=(WB // tc, length // tr),
        in_specs=[pl.BlockSpec((tr, length), lambda j, i: (i, 0)),
                  pl.BlockSpec((tr, length), lambda j, i: (i, 0)),
                  pl.BlockSpec((length, tc), lambda j, i: (0, j)),
                  pl.BlockSpec((length, tc), lambda j, i: (0, j))],
        out_specs=[pl.BlockSpec((tr, tc), lambda j, i: (i, j))] * 2,
        out_shape=[jax.ShapeDtypeStruct((length, WB), F32)] * 2,
        compiler_params=_cparams(("parallel", "parallel")),
        name="hyena_filter_dft",
    )(cm, sm, fs, fd)


def _dft_fwd_kernel(c_ref, s_ref, u_ref, a_ref, b_ref, re_ref, im_ref):
    u = u_ref[...]
    p = jnp.dot(c_ref[...], u, preferred_element_type=F32)
    q = jnp.dot(s_ref[...], u, preferred_element_type=F32)
    a = a_ref[...]
    b = b_ref[...]
    re_ref[...] = (p * a + q * b).astype(BF16)
    im_ref[...] = (p * b - q * a).astype(BF16)


def _dft_fwd(cm, sm, u16, a, bq, length):
    tr, tc = _dft_tiles(length)
    ncj = WB // tc
    return pl.pallas_call(
        _dft_fwd_kernel,
        grid=(NB * ncj, length // tr),
        in_specs=[pl.BlockSpec((tr, length), lambda j, i: (i, 0)),
                  pl.BlockSpec((tr, length), lambda j, i: (i, 0)),
                  pl.BlockSpec((length, tc), lambda j, i: (0, j)),
                  pl.BlockSpec((tr, tc), lambda j, i: (i, j % ncj)),
                  pl.BlockSpec((tr, tc), lambda j, i: (i, j % ncj))],
        out_specs=[pl.BlockSpec((tr, tc), lambda j, i: (i, j))] * 2,
        out_shape=[jax.ShapeDtypeStruct((length, NB * WB), BF16)] * 2,
        compiler_params=_cparams(("parallel", "parallel")),
        name="hyena_fwd_dft",
    )(cm, sm, u16, a, bq)


def _dft_inv_kernel(c_ref, s_ref, re_ref, im_ref, u_ref, x0_ref, bias_ref, *rest):
    o_ref = rest[-1]
    y = (jnp.dot(c_ref[...], re_ref[...], preferred_element_type=F32)
         + jnp.dot(s_ref[...], im_ref[...], preferred_element_type=F32))
    o_ref[...] = ((y + u_ref[...] * bias_ref[...]) * x0_ref[...]).astype(BF16)


def _dft_inv(ct, snt, re, im, u32, x0, bias, length, row_off, hy=None):
    tr, tc = _dft_tiles(length)
    ncj = WB // tc
    nri = length // tr
    rb = row_off // tr
    aliased = hy is not None
    in_specs = [pl.BlockSpec((tr, length), lambda j, i: (i, 0)),
                pl.BlockSpec((tr, length), lambda j, i: (i, 0)),
                pl.BlockSpec((length, tc), lambda j, i: (0, j)),
                pl.BlockSpec((length, tc), lambda j, i: (0, j)),
                pl.BlockSpec((tr, tc), lambda j, i: (i, j)),
                pl.BlockSpec((tr, tc), lambda j, i: (i, j)),
                pl.BlockSpec((1, tc), lambda j, i: (0, j % ncj))]
    args = [ct, snt, re, im, u32, x0, bias.reshape(1, WB)]
    if aliased:
        in_specs.append(pl.BlockSpec(memory_space=pl.ANY))
        args.append(hy)
    return pl.pallas_call(
        _dft_inv_kernel,
        grid=(NB * ncj, nri),
        in_specs=in_specs,
        out_specs=pl.BlockSpec((tr, tc), lambda j, i: (rb + (j // ncj) * nri + i, j % ncj)),
        out_shape=jax.ShapeDtypeStruct((T_ALL, WB), BF16),
        input_output_aliases={7: 0} if aliased else {},
        compiler_params=_cparams(("parallel", "parallel")),
        name="hyena_inv_dft",
    )(*args)


def _hyena(z, hy_params, mats, length, row_off, hy=None):
    short_w, short_b, w0, b0, w1, b1, w2, b2, w3, freq, bias = hy_params
    ck, sk, ct, snt = mats
    u16, u32, x0 = _hy_pre(z, short_w, short_b, length, row_off)
    fs, fd = _hy_filter(length, w0, b0, w1, b1, w2, b2, w3, freq)
    a, bq = _dft_spec(ck, sk, fs, fd, length)
    re, im = _dft_fwd(ck, sk, u16, a, bq, length)
    return _dft_inv(ct, snt, re, im, u32, x0, bias, length, row_off, hy)


def _dft_mats(length):
    kt, tk = _dft_tables(length)
    ck, sk = _dftgen(kt, length, 1.0)
    ct, snt = _dftgen(tk, length, -1.0)
    return ck, sk, ct, snt


def _outproj_kernel(x_ref, mod_ref, a_ref, y_ref, wa_ref, wy_ref, o_ref):
    o = (jnp.dot(a_ref[...], wa_ref[...], preferred_element_type=F32)
         + jnp.dot(y_ref[...], wy_ref[...], preferred_element_type=F32))
    o_ref[...] = x_ref[...] + mod_ref[5:6, :] * o


def _outproj(xs, mods, att, hy, w_out, layer, e, nblk):
    return pl.pallas_call(
        _outproj_kernel,
        grid=(nblk,),
        in_specs=[
            pl.BlockSpec((TM, D), lambda i: (i, 0)),
            pl.BlockSpec((None, None, N_MOD, D), lambda i: (layer, _mod_index(i, TM), 0, 0)),
            pl.BlockSpec((TM, WA), lambda i: (i, 0)),
            pl.BlockSpec((TM, WB), lambda i: (i, 0)),
            pl.BlockSpec((None, WA, D), lambda i: (e, 0, 0)),
            pl.BlockSpec((None, WB, D), lambda i: (e, 1, 0)),
        ],
        out_specs=pl.BlockSpec((TM, D), lambda i: (i, 0)),
        out_shape=jax.ShapeDtypeStruct(xs.shape, F32),
        input_output_aliases={0: 0},
        compiler_params=_cparams(("parallel",)),
        name="mix_outproj",
    )(xs, mods, att, hy, w_out, w_out)


CONV_HALO = 16
CONV_RC = 64


def _conf_out_kernel(x_ref, mod_ref, up_ref, uc_ref, un_ref, wdw_ref, bdw_ref, lg_ref, lb_ref,
                     w2_ref, b2_ref, o_ref, ext_scr, cv_scr, ph_scr, *, tm, bps):
    i = pl.program_id(0)
    first = (i % bps) == 0
    last = (i % bps) == bps - 1
    halo = CONV_HALO
    zero = jnp.zeros((halo, D), F32)
    ext_scr[0:halo, :] = jnp.where(first, zero, up_ref[...])
    ext_scr[halo:halo + tm, :] = uc_ref[...]
    ext_scr[halo + tm:2 * halo + tm, :] = jnp.where(last, zero, un_ref[...])
    off = halo - CONV_W // 2

    def col_body(cc, carry):
        c0 = pl.multiple_of(cc * LANE, LANE)
        wv = wdw_ref[:, pl.ds(c0, LANE)]
        bv = bdw_ref[:, pl.ds(c0, LANE)]

        span = ph_scr.shape[1]
        for p in range(1, SUBLANE):
            ph_scr[p] = ext_scr[pl.ds(p, span), pl.ds(c0, LANE)]
        for r0 in range(0, tm, CONV_RC):
            acc = jnp.zeros((CONV_RC, LANE), F32)
            for j in range(CONV_W):
                p = (off + j) % SUBLANE
                base = off + j - p + r0
                if p == 0:
                    rows = ext_scr[pl.ds(base, CONV_RC), pl.ds(c0, LANE)]
                else:
                    rows = ph_scr[p, pl.ds(base, CONV_RC), :]
                acc = acc + rows * wv[j:j + 1, :]
            cv_scr[pl.ds(r0, CONV_RC), pl.ds(c0, LANE)] = acc + bv
        return carry

    lax.fori_loop(0, D // LANE, col_body, 0)

    v = cv_scr[...]
    mu = jnp.mean(v, axis=-1, keepdims=True)
    vc = v - mu
    var = jnp.mean(vc * vc, axis=-1, keepdims=True)
    t = vc * lax.rsqrt(var + LN_EPS) * lg_ref[...] + lb_ref[...]
    t = (t * jax.nn.sigmoid(t)).astype(BF16)
    o = jnp.dot(t, w2_ref[...], preferred_element_type=F32) + b2_ref[...]
    o_ref[...] = x_ref[...] + mod_ref[5:6, :] * o


def _conf_out(xs, mods, u, w_dw, b_dw, ln_g, ln_b, w_pw2, b_pw2, layer, o, tm, row_off, seq, nblk):
    bps = seq // tm
    rb = row_off // tm
    vec = lambda a: a.reshape(-1, 1, D)
    blk = lambda i: (rb + i, 0)
    hpb = tm // CONV_HALO
    prev = lambda i: (jnp.maximum((rb + i) * hpb - 1, 0), 0)
    nxt = lambda i: (jnp.minimum((rb + i + 1) * hpb, T_ALL // CONV_HALO - 1), 0)
    modmap = lambda i: (layer, _mod_index(rb + i, tm), 0, 0)
    return pl.pallas_call(
        functools.partial(_conf_out_kernel, tm=tm, bps=bps),
        grid=(nblk,),
        in_specs=[
            pl.BlockSpec((tm, D), blk),
            pl.BlockSpec((None, None, N_MOD, D), modmap),
            pl.BlockSpec((CONV_HALO, D), prev),
            pl.BlockSpec((tm, D), blk),
            pl.BlockSpec((CONV_HALO, D), nxt),
            pl.BlockSpec((None, CONV_W, D), lambda i: (o, 0, 0)),
            pl.BlockSpec((None, 1, D), lambda i: (o, 0, 0)),
            pl.BlockSpec((None, 1, D), lambda i: (o, 0, 0)),
            pl.BlockSpec((None, 1, D), lambda i: (o, 0, 0)),
            pl.BlockSpec((None, D, D), lambda i: (o, 0, 0)),
            pl.BlockSpec((None, 1, D), lambda i: (o, 0, 0)),
        ],
        out_specs=pl.BlockSpec((tm, D), blk),
        out_shape=jax.ShapeDtypeStruct(xs.shape, F32),
        scratch_shapes=[pltpu.VMEM((tm + 2 * CONV_HALO, D), F32), pltpu.VMEM((tm, D), F32),
                        pltpu.VMEM((SUBLANE, tm + 2 * CONV_HALO - SUBLANE, LANE), F32)],
        input_output_aliases={0: 0},
        compiler_params=_cparams(("parallel",)),
        name="conf_out",
    )(xs, mods, u, u, u, w_dw, vec(b_dw), vec(ln_g), vec(ln_b), w_pw2, vec(b_pw2))


def _cast_pad_kernel(w_ref, o_ref, *, n, axis):
    v = w_ref[...].astype(BF16)
    if axis == 0:
        o_ref[:n, :] = v
        o_ref[n:, :] = jnp.zeros((o_ref.shape[0] - n, o_ref.shape[1]), BF16)
    else:
        o_ref[:, :n] = v
        o_ref[:, n:] = jnp.zeros((o_ref.shape[0], o_ref.shape[1] - n), BF16)


def _cast_pad(w, axis, padded):
    nl, r, c = w.shape
    strip = 256
    if axis == 0:
        in_blk, out_blk, out_shape, n = (None, r, strip), (None, padded, strip), (nl, padded, c), r
        imap = lambda l, j: (l, 0, j)
        steps = c // strip
    else:
        in_blk, out_blk, out_shape, n = (None, strip, c), (None, strip, padded), (nl, r, padded), c
        imap = lambda l, j: (l, j, 0)
        steps = r // strip
    return pl.pallas_call(
        functools.partial(_cast_pad_kernel, n=n, axis=axis),
        grid=(nl, steps),
        in_specs=[pl.BlockSpec(in_blk, imap)],
        out_specs=pl.BlockSpec(out_blk, imap),
        out_shape=jax.ShapeDtypeStruct(out_shape, BF16),
        compiler_params=_cparams(("parallel", "parallel")),
        name="cast_pad",
    )(w)


def _ffn_weights(w1, w3, w2):
    return _cast_pad(w1, 1, DFF_PAD), _cast_pad(w3, 1, DFF_PAD), _cast_pad(w2, 0, DFF_PAD)


def kernel(x, c, ctx, c_ctx, ada_w, ada_b, norm_g, ff1_w1, ff1_w3, ff1_w2, ff2_w1, ff2_w3, ff2_w2,
           mix_w_in, mix_w_out, na_rpb, hy_short_w, hy_short_b, hy_w0, hy_b0, hy_w1, hy_b1,
           hy_w2, hy_b2, hy_w3, hy_freq, hy_bias, cv_w_pw1, cv_b_pw1, cv_w_dw, cv_b_dw,
           cv_ln_g, cv_ln_b, cv_w_pw2, cv_b_pw2, final_g):
    xs = jnp.concatenate([x.reshape(T_LAT, D), ctx.reshape(T_CTX, D),
                          jnp.zeros((T_PAD - T_ALL, D), F32)], axis=0)
    cond = jnp.concatenate([c, c_ctx[None, :], jnp.zeros((8 - NB - 1, D), F32)], axis=0)
    mods = _ada(cond, ada_w, ada_b)[:, :NB + 1].reshape(DEPTH, NB + 1, N_MOD, D)

    f1 = _ffn_weights(ff1_w1, ff1_w3, ff1_w2)
    f2 = _ffn_weights(ff2_w1, ff2_w3, ff2_w2)
    w_in = mix_w_in.astype(BF16)
    w_out = mix_w_out.astype(BF16)
    w_pw1 = cv_w_pw1.astype(BF16)
    w_pw2 = cv_w_pw2.astype(BF16)

    mats_lat = _dft_mats(S)
    mats_ctx = _dft_mats(LC)

    out = None
    for i in range(DEPTH):
        ctx_full = i < LAST_EVEN
        xs = _ffn(xs, mods, norm_g[i, 0], *f1, i, 0, i <= LAST_EVEN)
        if i % 2 == 0:
            e = i // 2
            hyp = (hy_short_w[e], hy_short_b[e], hy_w0[e], hy_b0[e], hy_w1[e], hy_b1[e],
                   hy_w2[e], hy_b2[e], hy_w3[e], hy_freq[e], hy_bias[e])
            qkv, z = _inproj(xs, mods, norm_g[i, 1], w_in, i, e, i <= LAST_EVEN)
            att = _natten(qkv, na_rpb[e])
            hy = _hyena(z, hyp, mats_lat, S, 0)
            if ctx_full:
                att = _ctx_attn(qkv, att)
                hy = _hyena(z, hyp, mats_ctx, LC, T_LAT, hy)
            xs = _outproj(xs, mods, att, hy, w_out, i, e, NBLK_ALL if ctx_full else NBLK_LAT)
        else:
            o = i // 2
            u = _pw1(xs, mods, norm_g[i, 1], w_pw1, cv_b_pw1, i, o, ctx_full)
            cv = (cv_w_dw, cv_b_dw, cv_ln_g, cv_ln_b, w_pw2, cv_b_pw2)
            xs = _conf_out(xs, mods, u, *cv, i, o, TM, 0, S, NBLK_LAT)
            if ctx_full:
                xs = _conf_out(xs, mods, u, *cv, i, o, LC, T_LAT, LC, T_CTX // LC)
        if i == DEPTH - 1:
            out = _ffn(xs, mods, norm_g[i, 2], *f2, i, 6, ctx_full, final_g=final_g)
        else:
            xs = _ffn(xs, mods, norm_g[i, 2], *f2, i, 6, ctx_full)
    return out.reshape(NB, S, D)
```

```python
import functools
import math

import numpy as np
import jax
import jax.numpy as jnp
from jax import lax
from jax.experimental import pallas as pl
from jax.experimental.pallas import tpu as pltpu

F32 = jnp.float32
BF16 = jnp.bfloat16

D = 2048
NB = 2
S = 4096
DEPTH = 4
GRID_W = 64
GRID_H = S // GRID_W
LC = 256
HD = 128
NH = 8
WA = NH * HD
WB = D - WA
WIN_R = 8
WIN_C = 16
DFF = 5504
HY_EMB = 33
HY_ORDER = 64
CONV_W = 31
N_MOD = 9
RMS_EPS = 1e-6
LN_EPS = 1e-5
LAST_EVEN = (DEPTH - 1) - ((DEPTH - 1) % 2)

T_LAT = NB * S
T_CTX = NB * LC
T_ALL = T_LAT + T_CTX
T_PAD = 9216

LANE = 128
SUBLANE = 8
TM = 512
TM_FFN = 1024
TM_PROJ = 1024
PROJ_ROWS = 512
FFN_ROWS = 512
TF = 512
DFF_PAD = TF * (-(-DFF // TF))
NBLK_LAT = T_LAT // TM
NBLK_ALL = T_ALL // TM
NEG = -1e30
VMEM_LIMIT = 60 * 1024 * 1024


def _cparams(sem, vmem=VMEM_LIMIT):
    return pltpu.CompilerParams(dimension_semantics=sem, vmem_limit_bytes=vmem)


def _mod_index(i, tm):
    return jnp.minimum(i // (S // tm), NB)


def _stream_blocks(with_ctx, tm):
    return pl.cdiv(T_ALL if with_ctx else T_LAT, tm)


def _norm_mod(x, g, shift, scale):
    ms = jnp.mean(x * x, axis=-1, keepdims=True)
    return (x * lax.rsqrt(ms + RMS_EPS)) * g * (1.0 + scale) + shift


def _ada_kernel(s_ref, w_ref, b_ref, o_ref):
    s = s_ref[...]
    s = (s * jax.nn.sigmoid(s)).astype(BF16)
    o_ref[...] = jnp.dot(s, w_ref[...].astype(BF16), preferred_element_type=F32) + b_ref[...]


def _ada(cond, ada_w, ada_b):
    tn = 1024
    n = N_MOD * D
    return pl.pallas_call(
        _ada_kernel,
        grid=(DEPTH, n // tn),
        in_specs=[
            pl.BlockSpec((8, D), lambda l, j: (0, 0)),
            pl.BlockSpec((None, D, tn), lambda l, j: (l, 0, j)),
            pl.BlockSpec((None, 1, tn), lambda l, j: (l, 0, j)),
        ],
        out_specs=pl.BlockSpec((None, 8, tn), lambda l, j: (l, 0, j)),
        out_shape=jax.ShapeDtypeStruct((DEPTH, 8, n), F32),
        compiler_params=_cparams(("parallel", "parallel")),
        name="ada_mod",
    )(cond, ada_w, ada_b.reshape(DEPTH, 1, n))


def _ffn_kernel(x_ref, mod_ref, g_ref, w1_ref, w3_ref, w2_ref, *rest, jbase, nf, final, full_blocks,
                tail_rows):
    if final:
        fg_ref, o_ref, h_scr = rest
    else:
        o_ref, h_scr = rest
    i = pl.program_id(0)
    f = pl.program_id(1)
    tm = h_scr.shape[0]

    def norm_rows(r):
        rows = slice(r, r + FFN_ROWS)
        h = _norm_mod(x_ref[rows, :], g_ref[...], mod_ref[jbase:jbase + 1, :],
                      mod_ref[jbase + 1:jbase + 2, :])
        h_scr[rows, :] = h.astype(BF16)

    def chain(r, first):
        rows = slice(r, r + FFN_ROWS)
        h = h_scr[rows, :]
        a = jnp.dot(h, w1_ref[...], preferred_element_type=F32)
        b = jnp.dot(h, w3_ref[...], preferred_element_type=F32)
        gate = (a * jax.nn.sigmoid(a) * b).astype(BF16)
        part = jnp.dot(gate, w2_ref[...], preferred_element_type=F32)
        if first:
            o_ref[rows, :] = part
        else:
            o_ref[rows, :] += part

    def epilogue(r):
        rows = slice(r, r + FFN_ROWS)
        y = x_ref[rows, :] + (0.5 * mod_ref[jbase + 2:jbase + 3, :]) * o_ref[rows, :]
        if final:
            ms = jnp.mean(y * y, axis=-1, keepdims=True)
            y = (y * lax.rsqrt(ms + RMS_EPS)) * fg_ref[...]
        o_ref[rows, :] = y

    def step(live):
        starts = range(0, live, FFN_ROWS)

        @pl.when(f == 0)
        def _():
            for r in starts:
                norm_rows(r)
                chain(r, True)
            if live < tm:
                o_ref[live:, :] = jnp.zeros((tm - live, o_ref.shape[1]), F32)

        @pl.when((f > 0) & (f < nf - 1))
        def _():
            for r in starts:
                chain(r, False)

        @pl.when(f == nf - 1)
        def _():
            for r in starts:
                chain(r, False)
                epilogue(r)

    if tail_rows == tm:
        step(tm)
    else:
        @pl.when(i < full_blocks)
        def _():
            step(tm)

        @pl.when(i >= full_blocks)
        def _():
            step(tail_rows)


def _ffn(xs, mods, g, w1, w3, w2, layer, jbase, with_ctx, final_g=None):
    nf = DFF_PAD // TF
    final = final_g is not None
    tm = TM_FFN
    nblk = _stream_blocks(with_ctx, tm)
    in_specs = [
        pl.BlockSpec((tm, D), lambda i, f: (i, 0)),
        pl.BlockSpec((None, None, N_MOD, D),
                     lambda i, f: (layer, _mod_index(i, tm), 0, 0)),
        pl.BlockSpec((1, D), lambda i, f: (0, 0)),
        pl.BlockSpec((None, D, TF), lambda i, f: (layer, 0, f)),
        pl.BlockSpec((None, D, TF), lambda i, f: (layer, 0, f)),
        pl.BlockSpec((None, TF, D), lambda i, f: (layer, f, 0)),
    ]
    args = [xs, mods, g.reshape(1, D), w1, w3, w2]
    if final:
        in_specs.append(pl.BlockSpec((1, D), lambda i, f: (0, 0)))
        args.append(final_g.reshape(1, D))
        out_shape = jax.ShapeDtypeStruct((nblk * tm, D), F32)
        aliases = {}
    else:
        out_shape = jax.ShapeDtypeStruct(xs.shape, F32)
        aliases = {0: 0}
    return pl.pallas_call(
        functools.partial(_ffn_kernel, jbase=jbase, nf=nf, final=final, full_blocks=T_LAT // tm,
                          tail_rows=T_ALL - T_LAT if with_ctx else tm),
        grid=(nblk, nf),
        in_specs=in_specs,
        out_specs=pl.BlockSpec((tm, D), lambda i, f: (i, 0)),
        out_shape=out_shape,
        scratch_shapes=[pltpu.VMEM((tm, D), BF16)],
        input_output_aliases=aliases,
        compiler_params=_cparams(("parallel", "arbitrary")),
        name="ffn",
    )(*args)


def _inproj_kernel(x_ref, mod_ref, g_ref, w_ref, qkv_ref, z_ref, h_scr, *, nq):
    j = pl.program_id(1)
    starts = range(0, h_scr.shape[0], PROJ_ROWS)

    def norm_rows(r):
        rows = slice(r, r + PROJ_ROWS)
        h = _norm_mod(x_ref[rows, :], g_ref[...], mod_ref[3:4, :], mod_ref[4:5, :])
        h_scr[rows, :] = h.astype(BF16)

    def proj(r):
        return jnp.dot(h_scr[r:r + PROJ_ROWS, :], w_ref[...], preferred_element_type=F32)

    @pl.when(j == 0)
    def _():
        for r in starts:
            norm_rows(r)
            qkv_ref[r:r + PROJ_ROWS, :] = (proj(r) * (HD ** -0.5)).astype(BF16)

    @pl.when((j > 0) & (j < nq))
    def _():
        for r in starts:
            qkv_ref[r:r + PROJ_ROWS, :] = proj(r).astype(BF16)

    @pl.when(j >= nq)
    def _():
        for r in starts:
            z_ref[r:r + PROJ_ROWS, :] = proj(r)


def _inproj(xs, mods, g, w_in, layer, e, with_ctx):
    tn = WA
    nq = 3 * WA // tn
    nz = 3 * WB // tn
    tm = TM_PROJ
    return pl.pallas_call(
        functools.partial(_inproj_kernel, nq=nq),
        grid=(_stream_blocks(with_ctx, tm), nq + nz),
        in_specs=[
            pl.BlockSpec((tm, D), lambda i, j: (i, 0)),
            pl.BlockSpec((None, None, N_MOD, D), lambda i, j: (layer, _mod_index(i, tm), 0, 0)),
            pl.BlockSpec((1, D), lambda i, j: (0, 0)),
            pl.BlockSpec((None, D, tn), lambda i, j: (e, 0, j)),
        ],
        out_specs=[
            pl.BlockSpec((tm, tn), lambda i, j: (i, jnp.minimum(j, nq - 1))),
            pl.BlockSpec((tm, tn), lambda i, j: (i, jnp.maximum(j - nq, 0))),
        ],
        out_shape=[
            jax.ShapeDtypeStruct((T_PAD, 3 * WA), BF16),
            jax.ShapeDtypeStruct((T_PAD, 3 * WB), F32),
        ],
        scratch_shapes=[pltpu.VMEM((tm, D), BF16)],
        compiler_params=_cparams(("parallel", "arbitrary")),
        name="mix_inproj",
    )(xs, mods, g.reshape(1, D), w_in)


def _pw1_kernel(x_ref, mod_ref, g_ref, wa_ref, wg_ref, ba_ref, bg_ref, u_ref, h_scr):
    j = pl.program_id(1)
    starts = range(0, h_scr.shape[0], PROJ_ROWS)

    def norm_rows(r):
        rows = slice(r, r + PROJ_ROWS)
        h = _norm_mod(x_ref[rows, :], g_ref[...], mod_ref[3:4, :], mod_ref[4:5, :])
        h_scr[rows, :] = h.astype(BF16)

    def glu(r):
        rows = slice(r, r + PROJ_ROWS)
        h = h_scr[rows, :]
        a = jnp.dot(h, wa_ref[...], preferred_element_type=F32) + ba_ref[...]
        gt = jnp.dot(h, wg_ref[...], preferred_element_type=F32) + bg_ref[...]
        u_ref[rows, :] = a * jax.nn.sigmoid(gt)

    @pl.when(j == 0)
    def _():
        for r in starts:
            norm_rows(r)
            glu(r)

    @pl.when(j > 0)
    def _():
        for r in starts:
            glu(r)


def _pw1(xs, mods, g, w_pw1, b_pw1, layer, o, with_ctx):
    tn = 512
    nj = D // tn
    tm = TM_PROJ
    b3 = b_pw1.reshape(-1, 1, 2 * D)
    return pl.pallas_call(
        _pw1_kernel,
        grid=(_stream_blocks(with_ctx, tm), nj),
        in_specs=[
            pl.BlockSpec((tm, D), lambda i, j: (i, 0)),
            pl.BlockSpec((None, None, N_MOD, D), lambda i, j: (layer, _mod_index(i, tm), 0, 0)),
            pl.BlockSpec((1, D), lambda i, j: (0, 0)),
            pl.BlockSpec((None, D, tn), lambda i, j: (o, 0, j)),
            pl.BlockSpec((None, D, tn), lambda i, j: (o, 0, j + nj)),
            pl.BlockSpec((None, 1, tn), lambda i, j: (o, 0, j)),
            pl.BlockSpec((None, 1, tn), lambda i, j: (o, 0, j + nj)),
        ],
        out_specs=pl.BlockSpec((tm, tn), lambda i, j: (i, j)),
        out_shape=jax.ShapeDtypeStruct((T_PAD, D), F32),
        scratch_shapes=[pltpu.VMEM((tm, D), BF16)],
        compiler_params=_cparams(("parallel", "arbitrary")),
        name="conf_pw1",
    )(xs, mods, g.reshape(1, D), w_pw1, w_pw1, b3, b3)


NAT_G = 8
NAT_KR = {1: 8, 2: 10, 4: 12, 8: 16}[NAT_G]


def _natten_geometry():
    kstart, types, type_id = [], [], []
    gi = np.arange(NAT_G)[:, None]
    kj = np.arange(NAT_KR)[None, :]
    for g in range(GRID_H // NAT_G):
        r0 = g * NAT_G
        ks = min(max(r0 - WIN_R // 2, 0), GRID_H - NAT_KR)
        qrow, krow = r0 + gi, ks + kj
        rs = np.clip(qrow - WIN_R // 2, 0, GRID_H - WIN_R)
        valid = (krow >= rs) & (krow < rs + WIN_R)
        assert (valid.sum(axis=1) == WIN_R).all()
        dr = np.where(valid, krow - qrow + WIN_R - 1, 2 * WIN_R - 1).astype(np.int32)
        for t, d0 in enumerate(types):
            if (d0 == dr).all():
                type_id.append(t)
                break
        else:
            type_id.append(len(types))
            types.append(dr)
        kstart.append(ks)
    return np.asarray(kstart, np.int32), np.asarray(type_id, np.int32), np.stack(types)


def _natten_bias(rpb):
    _, _, dr = _natten_geometry()
    ntypes = dr.shape[0]
    c = np.arange(GRID_W)[:, None]
    kc = np.arange(GRID_W)[None, :]
    cs = np.clip(c - WIN_C // 2, 0, GRID_W - WIN_C)
    cvalid = (kc >= cs) & (kc < cs + WIN_C)
    dc = kc - c + WIN_C - 1
    onehot = ((np.arange(2 * WIN_C - 1)[:, None, None] == dc[None]) & cvalid[None]).astype(np.float32)
    colmask = np.where(cvalid, 0.0, NEG).astype(np.float32)
    bcols = jnp.einsum("hrd,dck->hrck", rpb, jnp.asarray(onehot), precision=lax.Precision.HIGHEST)
    bcols = (bcols + jnp.asarray(colmask)).transpose(0, 2, 1, 3)
    slabs = []
    for ty in range(ntypes):
        for gi in range(NAT_G):
            kj = np.nonzero(dr[ty, gi] != 2 * WIN_R - 1)[0]
            a, r0 = int(kj[0]), int(dr[ty, gi, kj[0]])
            assert (kj == a + np.arange(WIN_R)).all() and (dr[ty, gi, kj] == r0 + np.arange(WIN_R)).all()
            s = bcols[:, :, r0:r0 + WIN_R, :].reshape(NH, GRID_W, WIN_R * GRID_W)
            slabs.append(jnp.pad(s, ((0, 0), (0, 0), (a * GRID_W, (NAT_KR - a - WIN_R) * GRID_W)),
                                 constant_values=NEG))
    return jnp.stack(slabs, axis=1).reshape(NH, ntypes, NAT_G * GRID_W, NAT_KR * GRID_W)


def _natten_kernel(ks_ref, ty_ref, q_ref, k_ref, v_ref, kc_ref, vc_ref, bias_ref, o_ref):
    gq, kk = NAT_G * GRID_W, NAT_KR * GRID_W
    kc = kc_ref[...]
    vc = vc_ref[...]
    nt = (((1,), (1,)), ((), ()))

    def body(g, carry):
        q0 = pl.multiple_of(g * gq, gq)
        k0 = pl.multiple_of(ks_ref[g] * GRID_W, GRID_W)
        q = q_ref[pl.ds(q0, gq), :]
        k = k_ref[pl.ds(k0, kk), :]
        v = v_ref[pl.ds(k0, kk), :]
        s_loc = lax.dot_general(q, k, nt, preferred_element_type=F32) + bias_ref[ty_ref[g]]
        s_ctx = lax.dot_general(q, kc, nt, preferred_element_type=F32)
        m = jnp.maximum(jnp.max(s_loc, axis=-1, keepdims=True),
                        jnp.max(s_ctx, axis=-1, keepdims=True))
        p_loc = jnp.exp(s_loc - m)
        p_ctx = jnp.exp(s_ctx - m)
        den = jnp.sum(p_loc, axis=-1, keepdims=True) + jnp.sum(p_ctx, axis=-1, keepdims=True)
        o = (jnp.dot(p_loc.astype(BF16), v, preferred_element_type=F32)
             + jnp.dot(p_ctx.astype(BF16), vc, preferred_element_type=F32))
        o_ref[pl.ds(q0, gq), :] = (o / den).astype(BF16)
        return carry

    lax.fori_loop(0, GRID_H // NAT_G, body, 0, unroll=True)


def _natten(qkv, rpb):
    kstart, type_id, _ = _natten_geometry()
    bias = _natten_bias(rpb)
    _, ntypes, gq, kk = bias.shape
    cb = T_LAT // LC
    grid_spec = pltpu.PrefetchScalarGridSpec(
        num_scalar_prefetch=2,
        grid=(NB, NH),
        in_specs=[
            pl.BlockSpec((S, HD), lambda b, h, *_: (b, h)),
            pl.BlockSpec((S, HD), lambda b, h, *_: (b, NH + h)),
            pl.BlockSpec((S, HD), lambda b, h, *_: (b, 2 * NH + h)),
            pl.BlockSpec((LC, HD), lambda b, h, *_: (cb + b, NH + h)),
            pl.BlockSpec((LC, HD), lambda b, h, *_: (cb + b, 2 * NH + h)),
            pl.BlockSpec((None, ntypes, gq, kk), lambda b, h, *_: (h, 0, 0, 0)),
        ],
        out_specs=pl.BlockSpec((S, HD), lambda b, h, *_: (b, h)),
    )
    return pl.pallas_call(
        _natten_kernel,
        grid_spec=grid_spec,
        out_shape=jax.ShapeDtypeStruct((T_ALL, WA), BF16),
        compiler_params=_cparams(("parallel", "parallel")),
        name="natten",
    )(jnp.asarray(kstart), jnp.asarray(type_id), qkv, qkv, qkv, qkv, qkv, bias)


def _ctx_attn_kernel(q_ref, k_ref, v_ref, att_hbm, o_ref):
    del att_hbm
    s = lax.dot_general(q_ref[...], k_ref[...], (((1,), (1,)), ((), ())),
                        preferred_element_type=F32)
    m = jnp.max(s, axis=-1, keepdims=True)
    p = jnp.exp(s - m)
    den = jnp.sum(p, axis=-1, keepdims=True)
    o = jnp.dot(p.astype(BF16), v_ref[...], preferred_element_type=F32)
    o_ref[...] = (o / den).astype(BF16)


def _ctx_attn(qkv, att):
    cb = T_LAT // LC
    return pl.pallas_call(
        _ctx_attn_kernel,
        grid=(NB, NH),
        in_specs=[
            pl.BlockSpec((LC, HD), lambda b, h: (cb + b, h)),
            pl.BlockSpec((LC, HD), lambda b, h: (cb + b, NH + h)),
            pl.BlockSpec((LC, HD), lambda b, h: (cb + b, 2 * NH + h)),
            pl.BlockSpec(memory_space=pl.ANY),
        ],
        out_specs=pl.BlockSpec((LC, HD), lambda b, h: (cb + b, h)),
        out_shape=jax.ShapeDtypeStruct(att.shape, att.dtype),
        input_output_aliases={3: 0},
        compiler_params=_cparams(("parallel", "parallel")),
        name="ctx_attn",
    )(qkv, qkv, qkv, att)


def _dft_tables(length):
    n2 = 4 * length
    r = jnp.arange(length, dtype=jnp.int32)[:, None]
    hi = jnp.arange(length // LANE, dtype=jnp.int32)[None, :]
    lo = jnp.arange(LANE, dtype=jnp.int32)[None, :]

    def cs(phase):
        ang = (phase % n2).astype(F32) * (2.0 * math.pi / n2)
        return jnp.cos(ang), jnp.sin(ang)

    kt = cs((2 * r + 1) * (LANE * hi)) + cs((2 * r + 1) * lo)
    tk = cs(r * (2 * LANE * hi)) + cs(r * (2 * lo + 1))
    return kt, tk


def _dftgen_kernel(ca_ref, sa_ref, cb_ref, sb_ref, c_ref, s_ref, *, nt, sgn):
    cb = cb_ref[...]
    sb = sb_ref[...]
    for t1 in range(nt):
        ca = ca_ref[:, t1:t1 + 1]
        sa = sa_ref[:, t1:t1 + 1]
        sl = slice(t1 * LANE, (t1 + 1) * LANE)
        c_ref[:, sl] = (ca * cb - sa * sb).astype(BF16)
        s_ref[:, sl] = (sgn * (sa * cb + ca * sb)).astype(BF16)


def _dftgen(tables, length, sgn):
    tr = min(256, length)
    nt = length // LANE
    row = lambda i: (i, 0)
    return pl.pallas_call(
        functools.partial(_dftgen_kernel, nt=nt, sgn=sgn),
        grid=(length // tr,),
        in_specs=[pl.BlockSpec((tr, nt), row), pl.BlockSpec((tr, nt), row),
                  pl.BlockSpec((tr, LANE), row), pl.BlockSpec((tr, LANE), row)],
        out_specs=[pl.BlockSpec((tr, length), row), pl.BlockSpec((tr, length), row)],
        out_shape=[jax.ShapeDtypeStruct((length, length), BF16)] * 2,
        compiler_params=_cparams(("parallel",)),
        name="dft_gen",
    )(*tables)


def _hy_pre_kernel(z0_ref, z1_ref, z2_ref, w_ref, b_ref, u16_ref, u32_ref, x0_ref):
    def conv(z_ref, part):
        z = z_ref[...]
        n = z.shape[0]
        row = lax.broadcasted_iota(jnp.int32, z.shape, 0)
        zm = jnp.where(row == 0, 0.0, pltpu.roll(z, 1, 0))
        zp = jnp.where(row == n - 1, 0.0, pltpu.roll(z, n - 1, 0))
        w = w_ref[part]
        return zm * w[0:1, :] + z * w[1:2, :] + zp * w[2:3, :] + b_ref[part]

    x0_ref[...] = conv(z0_ref, 0)
    u = conv(z2_ref, 2) * conv(z1_ref, 1)
    u32_ref[...] = u
    u16_ref[...] = u.astype(BF16)


def _hy_pre(z, short_w, short_b, length, row_off):
    tc = LANE
    nc = WB // tc
    rb = row_off // length
    w = short_w.reshape(3, 3, WB).transpose(1, 0, 2)
    bb = short_b.reshape(3, 1, WB)
    out_spec = pl.BlockSpec((length, tc), lambda b, c: (0, b * nc + c))
    return pl.pallas_call(
        _hy_pre_kernel,
        grid=(NB, nc),
        in_specs=[
            pl.BlockSpec((length, tc), lambda b, c: (rb + b, c)),
            pl.BlockSpec((length, tc), lambda b, c: (rb + b, nc + c)),
            pl.BlockSpec((length, tc), lambda b, c: (rb + b, 2 * nc + c)),
            pl.BlockSpec((3, 3, tc), lambda b, c: (0, 0, c)),
            pl.BlockSpec((3, 1, tc), lambda b, c: (0, 0, c)),
        ],
        out_specs=[out_spec, out_spec, out_spec],
        out_shape=[jax.ShapeDtypeStruct((length, NB * WB), BF16),
                   jax.ShapeDtypeStruct((length, NB * WB), F32),
                   jax.ShapeDtypeStruct((length, NB * WB), F32)],
        compiler_params=_cparams(("parallel", "parallel")),
        name="hyena_pre",
    )(z, z, z, w, bb)


def _hy_filter_kernel(z_ref, w0_ref, b0_ref, w1_ref, b1_ref, w2_ref, b2_ref, w3_ref, fr_ref,
                      dl_ref, fs_ref, fd_ref):
    hp = lax.Precision.HIGHEST
    z = z_ref[...]
    fr = fr_ref[...]
    h = jnp.sin(fr * (jnp.dot(z, w0_ref[...], precision=hp, preferred_element_type=F32) + b0_ref[...]))
    h = jnp.sin(fr * (jnp.dot(h, w1_ref[...], precision=hp, preferred_element_type=F32) + b1_ref[...]))
    h = jnp.sin(fr * (jnp.dot(h, w2_ref[...], precision=hp, preferred_element_type=F32) + b2_ref[...]))
    hh = jnp.dot(h.astype(BF16), w3_ref[...].astype(BF16), preferred_element_type=F32)
    win = jnp.exp(-z[:, 0:1] * dl_ref[...])
    fwd = hh[:, :WB] * win
    bwd = hh[:, WB:] * win
    row = lax.broadcasted_iota(jnp.int32, bwd.shape, 0) + pl.program_id(0) * z.shape[0]
    bwd = jnp.where(row == 0, 0.0, bwd)
    fs_ref[...] = (fwd + bwd).astype(BF16)
    fd_ref[...] = (bwd - fwd).astype(BF16)


def _hy_filter(length, w0, b0, w1, b1, w2, b2, w3, freq):
    t = jnp.linspace(0.0, 1.0, length, dtype=F32)[:, None]
    bands = (HY_EMB - 1) // 2
    f = jnp.linspace(1e-4, bands - 1, bands, dtype=F32)
    w = 2 * math.pi * jnp.arange(length, dtype=F32)[:, None] / length
    z = jnp.concatenate([t, jnp.cos(f * w), -jnp.sin(f * w)], axis=-1)
    emb = HY_ORDER
    z = jnp.pad(z, ((0, 0), (0, emb - HY_EMB)))
    w0p = jnp.pad(w0, ((0, emb - HY_EMB), (0, 0)))
    max_decay = math.log(1e-2) / 0.3
    min_decay = math.log(1e-2) / 1.5
    deltas = jnp.abs(jnp.linspace(min_decay, max_decay, WB, dtype=F32))[None, :]
    tt = min(256, length)
    full = lambda shape: pl.BlockSpec(shape, lambda i: (0,) * len(shape))
    row = lambda i: (i, 0)
    vec = lambda a: a.reshape(1, -1)
    return pl.pallas_call(
        _hy_filter_kernel,
        grid=(length // tt,),
        in_specs=[pl.BlockSpec((tt, emb), row),
                  full((emb, HY_ORDER)), full((1, HY_ORDER)),
                  full((HY_ORDER, HY_ORDER)), full((1, HY_ORDER)),
                  full((HY_ORDER, HY_ORDER)), full((1, HY_ORDER)),
                  full((HY_ORDER, 2 * WB)), full((1, HY_ORDER)), full((1, WB))],
        out_specs=[pl.BlockSpec((tt, WB), row), pl.BlockSpec((tt, WB), row)],
        out_shape=[jax.ShapeDtypeStruct((length, WB), BF16)] * 2,
        compiler_params=_cparams(("parallel",)),
        name="hyena_filter",
    )(z, w0p, vec(b0), w1, vec(b1), w2, vec(b2), w3, vec(freq), deltas)


def _dft_tiles(length):
    return min(512, length), 512


def _dft_spec_kernel(c_ref, s_ref, fs_ref, fd_ref, a_ref, b_ref, *, scale):
    a_ref[...] = jnp.dot(c_ref[...], fs_ref[...], preferred_element_type=F32) * scale
    b_ref[...] = jnp.dot(s_ref[...], fd_ref[...], preferred_element_type=F32) * scale


def _dft_spec(cm, sm, fs, fd, length):
    tr, tc = _dft_tiles(length)
    return pl.pallas_call(
        functools.partial(_dft_spec_kernel, scale=1.0 / length),
        grid=(WB // tc, length // tr),
        in_specs=[pl.BlockSpec((tr, length), lambda j, i: (i, 0)),
                  pl.BlockSpec((tr, length), lambda j, i: (i, 0)),
                  pl.BlockSpec((length, tc), lambda j, i: (0, j)),
                  pl.BlockSpec((length, tc), lambda j, i: (0, j))],
        out_specs=[pl.BlockSpec((tr, tc), lambda j, i: (i, j))] * 2,
        out_shape=[jax.ShapeDtypeStruct((length, WB), F32)] * 2,
        compiler_params=_cparams(("parallel", "parallel")),
        name="hyena_filter_dft",
    )(cm, sm, fs, fd)


def _dft_fwd_kernel(c_ref, s_ref, u_ref, a_ref, b_ref, re_ref, im_ref):
    u = u_ref[...]
    p = jnp.dot(c_ref[...], u, preferred_element_type=F32)
    q = jnp.dot(s_ref[...], u, preferred_element_type=F32)
    a = a_ref[...]
    b = b_ref[...]
    re_ref[...] = (p * a + q * b).astype(BF16)
    im_ref[...] = (p * b - q * a).astype(BF16)


def _dft_fwd(cm, sm, u16, a, bq, length):
    tr, tc = _dft_tiles(length)
    ncj = WB // tc
    return pl.pallas_call(
        _dft_fwd_kernel,
        grid=(NB * ncj, length // tr),
        in_specs=[pl.BlockSpec((tr, length), lambda j, i: (i, 0)),
                  pl.BlockSpec((tr, length), lambda j, i: (i, 0)),
                  pl.BlockSpec((length, tc), lambda j, i: (0, j)),
                  pl.BlockSpec((tr, tc), lambda j, i: (i, j % ncj)),
                  pl.BlockSpec((tr, tc), lambda j, i: (i, j % ncj))],
        out_specs=[pl.BlockSpec((tr, tc), lambda j, i: (i, j))] * 2,
        out_shape=[jax.ShapeDtypeStruct((length, NB * WB), BF16)] * 2,
        compiler_params=_cparams(("parallel", "parallel")),
        name="hyena_fwd_dft",
    )(cm, sm, u16, a, bq)


def _dft_inv_kernel(c_ref, s_ref, re_ref, im_ref, u_ref, x0_ref, bias_ref, *rest):
    o_ref = rest[-1]
    y = (jnp.dot(c_ref[...], re_ref[...], preferred_element_type=F32)
         + jnp.dot(s_ref[...], im_ref[...], preferred_element_type=F32))
    o_ref[...] = ((y + u_ref[...] * bias_ref[...]) * x0_ref[...]).astype(BF16)


def _dft_inv(ct, snt, re, im, u32, x0, bias, length, row_off, hy=None):
    tr, tc = _dft_tiles(length)
    ncj = WB // tc
    nri = length // tr
    rb = row_off // tr
    aliased = hy is not None
    in_specs = [pl.BlockSpec((tr, length), lambda j, i: (i, 0)),
                pl.BlockSpec((tr, length), lambda j, i: (i, 0)),
                pl.BlockSpec((length, tc), lambda j, i: (0, j)),
                pl.BlockSpec((length, tc), lambda j, i: (0, j)),
                pl.BlockSpec((tr, tc), lambda j, i: (i, j)),
                pl.BlockSpec((tr, tc), lambda j, i: (i, j)),
                pl.BlockSpec((1, tc), lambda j, i: (0, j % ncj))]
    args = [ct, snt, re, im, u32, x0, bias.reshape(1, WB)]
    if aliased:
        in_specs.append(pl.BlockSpec(memory_space=pl.ANY))
        args.append(hy)
    return pl.pallas_call(
        _dft_inv_kernel,
        grid=(NB * ncj, nri),
        in_specs=in_specs,
        out_specs=pl.BlockSpec((tr, tc), lambda j, i: (rb + (j // ncj) * nri + i, j % ncj)),
        out_shape=jax.ShapeDtypeStruct((T_ALL, WB), BF16),
        input_output_aliases={7: 0} if aliased else {},
        compiler_params=_cparams(("parallel", "parallel")),
        name="hyena_inv_dft",
    )(*args)


def _hyena(z, hy_params, mats, length, row_off, hy=None):
    short_w, short_b, w0, b0, w1, b1, w2, b2, w3, freq, bias = hy_params
    ck, sk, ct, snt = mats
    u16, u32, x0 = _hy_pre(z, short_w, short_b, length, row_off)
    fs, fd = _hy_filter(length, w0, b0, w1, b1, w2, b2, w3, freq)
    a, bq = _dft_spec(ck, sk, fs, fd, length)
    re, im = _dft_fwd(ck, sk, u16, a, bq, length)
    return _dft_inv(ct, snt, re, im, u32, x0, bias, length, row_off, hy)


def _dft_mats(length):
    kt, tk = _dft_tables(length)
    ck, sk = _dftgen(kt, length, 1.0)
    ct, snt = _dftgen(tk, length, -1.0)
    return ck, sk, ct, snt


def _outproj_kernel(x_ref, mod_ref, a_ref, y_ref, wa_ref, wy_ref, o_ref):
    o = (jnp.dot(a_ref[...], wa_ref[...], preferred_element_type=F32)
         + jnp.dot(y_ref[...], wy_ref[...], preferred_element_type=F32))
    o_ref[...] = x_ref[...] + mod_ref[5:6, :] * o


def _outproj(xs, mods, att, hy, w_out, layer, e, nblk):
    return pl.pallas_call(
        _outproj_kernel,
        grid=(nblk,),
        in_specs=[
            pl.BlockSpec((TM, D), lambda i: (i, 0)),
            pl.BlockSpec((None, None, N_MOD, D), lambda i: (layer, _mod_index(i, TM), 0, 0)),
            pl.BlockSpec((TM, WA), lambda i: (i, 0)),
            pl.BlockSpec((TM, WB), lambda i: (i, 0)),
            pl.BlockSpec((None, WA, D), lambda i: (e, 0, 0)),
            pl.BlockSpec((None, WB, D), lambda i: (e, 1, 0)),
        ],
        out_specs=pl.BlockSpec((TM, D), lambda i: (i, 0)),
        out_shape=jax.ShapeDtypeStruct(xs.shape, F32),
        input_output_aliases={0: 0},
        compiler_params=_cparams(("parallel",)),
        name="mix_outproj",
    )(xs, mods, att, hy, w_out, w_out)


CONV_HALO = 16
CONV_RC = 64


def _conf_out_kernel(x_ref, mod_ref, up_ref, uc_ref, un_ref, wdw_ref, bdw_ref, lg_ref, lb_ref,
                     w2_ref, b2_ref, o_ref, ext_scr, cv_scr, ph_scr, *, tm, bps):
    i = pl.program_id(0)
    first = (i % bps) == 0
    last = (i % bps) == bps - 1
    halo = CONV_HALO
    zero = jnp.zeros((halo, D), F32)
    ext_scr[0:halo, :] = jnp.where(first, zero, up_ref[...])
    ext_scr[halo:halo + tm, :] = uc_ref[...]
    ext_scr[halo + tm:2 * halo + tm, :] = jnp.where(last, zero, un_ref[...])
    off = halo - CONV_W // 2

    def col_body(cc, carry):
        c0 = pl.multiple_of(cc * LANE, LANE)
        wv = wdw_ref[:, pl.ds(c0, LANE)]
        bv = bdw_ref[:, pl.ds(c0, LANE)]

        span = ph_scr.shape[1]
        for p in range(1, SUBLANE):
            ph_scr[p] = ext_scr[pl.ds(p, span), pl.ds(c0, LANE)]
        for r0 in range(0, tm, CONV_RC):
            acc = jnp.zeros((CONV_RC, LANE), F32)
            for j in range(CONV_W):
                p = (off + j) % SUBLANE
                base = off + j - p + r0
                if p == 0:
                    rows = ext_scr[pl.ds(base, CONV_RC), pl.ds(c0, LANE)]
                else:
                    rows = ph_scr[p, pl.ds(base, CONV_RC), :]
                acc = acc + rows * wv[j:j + 1, :]
            cv_scr[pl.ds(r0, CONV_RC), pl.ds(c0, LANE)] = acc + bv
        return carry

    lax.fori_loop(0, D // LANE, col_body, 0)

    v = cv_scr[...]
    mu = jnp.mean(v, axis=-1, keepdims=True)
    vc = v - mu
    var = jnp.mean(vc * vc, axis=-1, keepdims=True)
    t = vc * lax.rsqrt(var + LN_EPS) * lg_ref[...] + lb_ref[...]
    t = (t * jax.nn.sigmoid(t)).astype(BF16)
    o = jnp.dot(t, w2_ref[...], preferred_element_type=F32) + b2_ref[...]
    o_ref[...] = x_ref[...] + mod_ref[5:6, :] * o


def _conf_out(xs, mods, u, w_dw, b_dw, ln_g, ln_b, w_pw2, b_pw2, layer, o, tm, row_off, seq, nblk):
    bps = seq // tm
    rb = row_off // tm
    vec = lambda a: a.reshape(-1, 1, D)
    blk = lambda i: (rb + i, 0)
    hpb = tm // CONV_HALO
    prev = lambda i: (jnp.maximum((rb + i) * hpb - 1, 0), 0)
    nxt = lambda i: (jnp.minimum((rb + i + 1) * hpb, T_ALL // CONV_HALO - 1), 0)
    modmap = lambda i: (layer, _mod_index(rb + i, tm), 0, 0)
    return pl.pallas_call(
        functools.partial(_conf_out_kernel, tm=tm, bps=bps),
        grid=(nblk,),
        in_specs=[
            pl.BlockSpec((tm, D), blk),
            pl.BlockSpec((None, None, N_MOD, D), modmap),
            pl.BlockSpec((CONV_HALO, D), prev),
            pl.BlockSpec((tm, D), blk),
            pl.BlockSpec((CONV_HALO, D), nxt),
            pl.BlockSpec((None, CONV_W, D), lambda i: (o, 0, 0)),
            pl.BlockSpec((None, 1, D), lambda i: (o, 0, 0)),
            pl.BlockSpec((None, 1, D), lambda i: (o, 0, 0)),
            pl.BlockSpec((None, 1, D), lambda i: (o, 0, 0)),
            pl.BlockSpec((None, D, D), lambda i: (o, 0, 0)),
            pl.BlockSpec((None, 1, D), lambda i: (o, 0, 0)),
        ],
        out_specs=pl.BlockSpec((tm, D), blk),
        out_shape=jax.ShapeDtypeStruct(xs.shape, F32),
        scratch_shapes=[pltpu.VMEM((tm + 2 * CONV_HALO, D), F32), pltpu.VMEM((tm, D), F32),
                        pltpu.VMEM((SUBLANE, tm + 2 * CONV_HALO - SUBLANE, LANE), F32)],
        input_output_aliases={0: 0},
        compiler_params=_cparams(("parallel",)),
        name="conf_out",
    )(xs, mods, u, u, u, w_dw, vec(b_dw), vec(ln_g), vec(ln_b), w_pw2, vec(b_pw2))


def _cast_pad_kernel(w_ref, o_ref, *, n, axis):
    v = w_ref[...].astype(BF16)
    if axis == 0:
        o_ref[:n, :] = v
        o_ref[n:, :] = jnp.zeros((o_ref.shape[0] - n, o_ref.shape[1]), BF16)
    else:
        o_ref[:, :n] = v
        o_ref[:, n:] = jnp.zeros((o_ref.shape[0], o_ref.shape[1] - n), BF16)


def _cast_pad(w, axis, padded):
    nl, r, c = w.shape
    strip = 256
    if axis == 0:
        in_blk, out_blk, out_shape, n = (None, r, strip), (None, padded, strip), (nl, padded, c), r
        imap = lambda l, j: (l, 0, j)
        steps = c // strip
    else:
        in_blk, out_blk, out_shape, n = (None, strip, c), (None, strip, padded), (nl, r, padded), c
        imap = lambda l, j: (l, j, 0)
        steps = r // strip
    return pl.pallas_call(
        functools.partial(_cast_pad_kernel, n=n, axis=axis),
        grid=(nl, steps),
        in_specs=[pl.BlockSpec(in_blk, imap)],
        out_specs=pl.BlockSpec(out_blk, imap),
        out_shape=jax.ShapeDtypeStruct(out_shape, BF16),
        compiler_params=_cparams(("parallel", "parallel")),
        name="cast_pad",
    )(w)


def _ffn_weights(w1, w3, w2):
    return _cast_pad(w1, 1, DFF_PAD), _cast_pad(w3, 1, DFF_PAD), _cast_pad(w2, 0, DFF_PAD)


def kernel(x, c, ctx, c_ctx, ada_w, ada_b, norm_g, ff1_w1, ff1_w3, ff1_w2, ff2_w1, ff2_w3, ff2_w2,
           mix_w_in, mix_w_out, na_rpb, hy_short_w, hy_short_b, hy_w0, hy_b0, hy_w1, hy_b1,
           hy_w2, hy_b2, hy_w3, hy_freq, hy_bias, cv_w_pw1, cv_b_pw1, cv_w_dw, cv_b_dw,
           cv_ln_g, cv_ln_b, cv_w_pw2, cv_b_pw2, final_g):
    xs = jnp.concatenate([x.reshape(T_LAT, D), ctx.reshape(T_CTX, D),
                          jnp.zeros((T_PAD - T_ALL, D), F32)], axis=0)
    cond = jnp.concatenate([c, c_ctx[None, :], jnp.zeros((8 - NB - 1, D), F32)], axis=0)
    mods = _ada(cond, ada_w, ada_b)[:, :NB + 1].reshape(DEPTH, NB + 1, N_MOD, D)

    f1 = _ffn_weights(ff1_w1, ff1_w3, ff1_w2)
    f2 = _ffn_weights(ff2_w1, ff2_w3, ff2_w2)
    w_in = mix_w_in.astype(BF16)
    w_out = mix_w_out.astype(BF16)
    w_pw1 = cv_w_pw1.astype(BF16)
    w_pw2 = cv_w_pw2.astype(BF16)

    mats_lat = _dft_mats(S)
    mats_ctx = _dft_mats(LC)

    out = None
    for i in range(DEPTH):
        ctx_full = i < LAST_EVEN
        xs = _ffn(xs, mods, norm_g[i, 0], *f1, i, 0, i <= LAST_EVEN)
        if i % 2 == 0:
            e = i // 2
            hyp = (hy_short_w[e], hy_short_b[e], hy_w0[e], hy_b0[e], hy_w1[e], hy_b1[e],
                   hy_w2[e], hy_b2[e], hy_w3[e], hy_freq[e], hy_bias[e])
            qkv, z = _inproj(xs, mods, norm_g[i, 1], w_in, i, e, i <= LAST_EVEN)
            att = _natten(qkv, na_rpb[e])
            hy = _hyena(z, hyp, mats_lat, S, 0)
            if ctx_full:
                att = _ctx_attn(qkv, att)
                hy = _hyena(z, hyp, mats_ctx, LC, T_LAT, hy)
            xs = _outproj(xs, mods, att, hy, w_out, i, e, NBLK_ALL if ctx_full else NBLK_LAT)
        else:
            o = i // 2
            u = _pw1(xs, mods, norm_g[i, 1], w_pw1, cv_b_pw1, i, o, ctx_full)
            cv = (cv_w_dw, cv_b_dw, cv_ln_g, cv_ln_b, w_pw2, cv_b_pw2)
            xs = _conf_out(xs, mods, u, *cv, i, o, TM, 0, S, NBLK_LAT)
            if ctx_full:
                xs = _conf_out(xs, mods, u, *cv, i, o, LC, T_LAT, LC, T_CTX // LC)
        if i == DEPTH - 1:
            out = _ffn(xs, mods, norm_g[i, 2], *f2, i, 6, ctx_full, final_g=final_g)
        else:
            xs = _ffn(xs, mods, norm_g[i, 2], *f2, i, 6, ctx_full)
    return out.reshape(NB, S, D)
```

```python
import functools
import math

import numpy as np
import jax
import jax.numpy as jnp
from jax import lax
from jax.experimental import pallas as pl
from jax.experimental.pallas import tpu as pltpu

F32 = jnp.float32
BF16 = jnp.bfloat16

D = 2048
NB = 2
S = 4096
DEPTH = 4
GRID_W = 64
GRID_H = S // GRID_W
LC = 256
HD = 128
NH = 8
WA = NH * HD
WB = D - WA
WIN_R = 8
WIN_C = 16
DFF = 5504
HY_EMB = 33
HY_ORDER = 64
CONV_W = 31
N_MOD = 9
RMS_EPS = 1e-6
LN_EPS = 1e-5
LAST_EVEN = (DEPTH - 1) - ((DEPTH - 1) % 2)

T_LAT = NB * S
T_CTX = NB * LC
T_ALL = T_LAT + T_CTX
T_PAD = 9216

LANE = 128
SUBLANE = 8
TM = 512
TM_FFN = 1024
TM_PROJ = 1024
PROJ_ROWS = 512
FFN_ROWS = 512
CAST_COL_SLABS = 64
CAST_ROW = 128
TF = 512
DFF_PAD = TF * (-(-DFF // TF))
CAST_ROW_SLABS = DFF_PAD // CAST_ROW
NBLK_LAT = T_LAT // TM
NBLK_ALL = T_ALL // TM
NEG = -1e30
VMEM_LIMIT = 60 * 1024 * 1024


def _cparams(sem, vmem=VMEM_LIMIT):
    return pltpu.CompilerParams(dimension_semantics=sem, vmem_limit_bytes=vmem)


def _mod_index(i, tm):
    return jnp.minimum(i // (S // tm), NB)


def _stream_blocks(with_ctx, tm):
    return pl.cdiv(T_ALL if with_ctx else T_LAT, tm)


def _norm_mod(x, g, shift, scale):
    ms = jnp.mean(x * x, axis=-1, keepdims=True)
    return (x * lax.rsqrt(ms + RMS_EPS)) * g * (1.0 + scale) + shift


def _ada_kernel(s_ref, w_ref, b_ref, o_ref):
    s = s_ref[...]
    s = (s * jax.nn.sigmoid(s)).astype(BF16)
    o_ref[...] = jnp.dot(s, w_ref[...].astype(BF16), preferred_element_type=F32) + b_ref[...]


def _ada(cond, ada_w, ada_b):
    tn = 1024
    n = N_MOD * D
    return pl.pallas_call(
        _ada_kernel,
        grid=(DEPTH, n // tn),
        in_specs=[
            pl.BlockSpec((8, D), lambda l, j: (0, 0)),
            pl.BlockSpec((None, D, tn), lambda l, j: (l, 0, j)),
            pl.BlockSpec((None, 1, tn), lambda l, j: (l, 0, j)),
        ],
        out_specs=pl.BlockSpec((None, 8, tn), lambda l, j: (l, 0, j)),
        out_shape=jax.ShapeDtypeStruct((DEPTH, 8, n), F32),
        compiler_params=_cparams(("parallel", "parallel")),
        name="ada_mod",
    )(cond, ada_w, ada_b.reshape(DEPTH, 1, n))


def _ffn_kernel(x_ref, mod_ref, g_ref, w1_ref, w3_ref, w2_ref, *rest, jbase, nf, final, full_blocks,
                tail_rows, casting):
    rest = list(rest)
    fg_ref = rest.pop(0) if final else None
    cast_in = [rest.pop(0) for _ in range(3)] if casting else []
    o_ref = rest.pop(0)
    cast_out = [rest.pop(0) for _ in range(3)] if casting else []
    (h_scr,) = rest
    i = pl.program_id(0)
    f = pl.program_id(1)
    tm = h_scr.shape[0]

    if casting:
        s = i * nf + f

        @pl.when(s < CAST_COL_SLABS)
        def _():
            for src, dst in zip(cast_in[:2], cast_out[:2]):
                dst[:, :DFF] = src[...].astype(BF16)
                dst[:, DFF:] = jnp.zeros((dst.shape[0], DFF_PAD - DFF), BF16)

        @pl.when(s < DFF // CAST_ROW)
        def _():
            cast_out[2][...] = cast_in[2][...].astype(BF16)

        @pl.when((s >= DFF // CAST_ROW) & (s < CAST_ROW_SLABS))
        def _():
            cast_out[2][...] = jnp.zeros(cast_out[2].shape, BF16)

    def norm_rows(r):
        rows = slice(r, r + FFN_ROWS)
        h = _norm_mod(x_ref[rows, :], g_ref[...], mod_ref[jbase:jbase + 1, :],
                      mod_ref[jbase + 1:jbase + 2, :])
        h_scr[rows, :] = h.astype(BF16)

    def chain(r, first):
        rows = slice(r, r + FFN_ROWS)
        h = h_scr[rows, :]
        a = jnp.dot(h, w1_ref[...], preferred_element_type=F32)
        b = jnp.dot(h, w3_ref[...], preferred_element_type=F32)
        gate = (a * jax.nn.sigmoid(a) * b).astype(BF16)
        part = jnp.dot(gate, w2_ref[...], preferred_element_type=F32)
        if first:
            o_ref[rows, :] = part
        else:
            o_ref[rows, :] += part

    def epilogue(r):
        rows = slice(r, r + FFN_ROWS)
        y = x_ref[rows, :] + (0.5 * mod_ref[jbase + 2:jbase + 3, :]) * o_ref[rows, :]
        if final:
            ms = jnp.mean(y * y, axis=-1, keepdims=True)
            y = (y * lax.rsqrt(ms + RMS_EPS)) * fg_ref[...]
        o_ref[rows, :] = y

    def step(live):
        starts = range(0, live, FFN_ROWS)

        @pl.when(f == 0)
        def _():
            for r in starts:
                norm_rows(r)
                chain(r, True)
            if live < tm:
                o_ref[live:, :] = jnp.zeros((tm - live, o_ref.shape[1]), F32)

        @pl.when((f > 0) & (f < nf - 1))
        def _():
            for r in starts:
                chain(r, False)

        @pl.when(f == nf - 1)
        def _():
            for r in starts:
                chain(r, False)
                epilogue(r)

    if tail_rows == tm:
        step(tm)
    else:
        @pl.when(i < full_blocks)
        def _():
            step(tm)

        @pl.when(i >= full_blocks)
        def _():
            step(tail_rows)


def _ffn(xs, mods, g, w1, w3, w2, layer, jbase, with_ctx, final_g=None, cast_next=None):
    nf = DFF_PAD // TF
    final = final_g is not None
    tm = TM_FFN
    nblk = _stream_blocks(with_ctx, tm)
    in_specs = [
        pl.BlockSpec((tm, D), lambda i, f: (i, 0)),
        pl.BlockSpec((None, None, N_MOD, D),
                     lambda i, f: (layer, _mod_index(i, tm), 0, 0)),
        pl.BlockSpec((1, D), lambda i, f: (0, 0)),
        pl.BlockSpec((D, TF), lambda i, f: (0, f)),
        pl.BlockSpec((D, TF), lambda i, f: (0, f)),
        pl.BlockSpec((TF, D), lambda i, f: (f, 0)),
    ]
    args = [xs, mods, g.reshape(1, D), w1, w3, w2]
    out_specs = [pl.BlockSpec((tm, D), lambda i, f: (i, 0))]
    if final:
        in_specs.append(pl.BlockSpec((1, D), lambda i, f: (0, 0)))
        args.append(final_g.reshape(1, D))
        out_shape = [jax.ShapeDtypeStruct((nblk * tm, D), F32)]
        aliases = {}
    else:
        out_shape = [jax.ShapeDtypeStruct(xs.shape, F32)]
        aliases = {0: 0}
    if cast_next is not None:
        c1, c3, c2, nlayer = cast_next
        assert nblk * nf >= max(CAST_COL_SLABS, CAST_ROW_SLABS)
        step = lambda i, f: i * nf + f
        col_in = pl.BlockSpec((None, D // CAST_COL_SLABS, DFF),
                              lambda i, f: (nlayer, jnp.minimum(step(i, f), CAST_COL_SLABS - 1), 0))
        col_out = pl.BlockSpec((D // CAST_COL_SLABS, DFF_PAD),
                               lambda i, f: (jnp.minimum(step(i, f), CAST_COL_SLABS - 1), 0))
        row_in = pl.BlockSpec((None, CAST_ROW, D),
                              lambda i, f: (nlayer, jnp.minimum(step(i, f), DFF // CAST_ROW - 1), 0))
        row_out = pl.BlockSpec((CAST_ROW, D),
                               lambda i, f: (jnp.minimum(step(i, f), CAST_ROW_SLABS - 1), 0))
        in_specs += [col_in, col_in, row_in]
        args += [c1, c3, c2]
        out_specs += [col_out, col_out, row_out]
        out_shape += [jax.ShapeDtypeStruct((D, DFF_PAD), BF16), jax.ShapeDtypeStruct((D, DFF_PAD), BF16),
                      jax.ShapeDtypeStruct((DFF_PAD, D), BF16)]
    res = pl.pallas_call(
        functools.partial(_ffn_kernel, jbase=jbase, nf=nf, final=final, full_blocks=T_LAT // tm,
                          tail_rows=T_ALL - T_LAT if with_ctx else tm, casting=cast_next is not None),
        grid=(nblk, nf),
        in_specs=in_specs,
        out_specs=out_specs,
        out_shape=out_shape,
        scratch_shapes=[pltpu.VMEM((tm, D), BF16)],
        input_output_aliases=aliases,
        compiler_params=_cparams(("arbitrary", "arbitrary")),
        name="ffn",
    )(*args)
    return (res[0], tuple(res[1:])) if cast_next is not None else res[0]


def _inproj_kernel(x_ref, mod_ref, g_ref, w_ref, qkv_ref, z_ref, h_scr, *, nq):
    j = pl.program_id(1)
    starts = range(0, h_scr.shape[0], PROJ_ROWS)

    def norm_rows(r):
        rows = slice(r, r + PROJ_ROWS)
        h = _norm_mod(x_ref[rows, :], g_ref[...], mod_ref[3:4, :], mod_ref[4:5, :])
        h_scr[rows, :] = h.astype(BF16)

    def proj(r):
        return jnp.dot(h_scr[r:r + PROJ_ROWS, :], w_ref[...], preferred_element_type=F32)

    @pl.when(j == 0)
    def _():
        for r in starts:
            norm_rows(r)
            qkv_ref[r:r + PROJ_ROWS, :] = (proj(r) * (HD ** -0.5)).astype(BF16)

    @pl.when((j > 0) & (j < nq))
    def _():
        for r in starts:
            qkv_ref[r:r + PROJ_ROWS, :] = proj(r).astype(BF16)

    @pl.when(j >= nq)
    def _():
        for r in starts:
            z_ref[r:r + PROJ_ROWS, :] = proj(r)


def _inproj(xs, mods, g, w_in, layer, e, with_ctx):
    tn = WA
    nq = 3 * WA // tn
    nz = 3 * WB // tn
    tm = TM_PROJ
    return pl.pallas_call(
        functools.partial(_inproj_kernel, nq=nq),
        grid=(_stream_blocks(with_ctx, tm), nq + nz),
        in_specs=[
            pl.BlockSpec((tm, D), lambda i, j: (i, 0)),
            pl.BlockSpec((None, None, N_MOD, D), lambda i, j: (layer, _mod_index(i, tm), 0, 0)),
            pl.BlockSpec((1, D), lambda i, j: (0, 0)),
            pl.BlockSpec((None, D, tn), lambda i, j: (e, 0, j)),
        ],
        out_specs=[
            pl.BlockSpec((tm, tn), lambda i, j: (i, jnp.minimum(j, nq - 1))),
            pl.BlockSpec((tm, tn), lambda i, j: (i, jnp.maximum(j - nq, 0))),
        ],
        out_shape=[
            jax.ShapeDtypeStruct((T_PAD, 3 * WA), BF16),
            jax.ShapeDtypeStruct((T_PAD, 3 * WB), F32),
        ],
        scratch_shapes=[pltpu.VMEM((tm, D), BF16)],
        compiler_params=_cparams(("parallel", "arbitrary")),
        name="mix_inproj",
    )(xs, mods, g.reshape(1, D), w_in)


def _pw1_kernel(x_ref, mod_ref, g_ref, wa_ref, wg_ref, ba_ref, bg_ref, u_ref, h_scr):
    j = pl.program_id(1)
    starts = range(0, h_scr.shape[0], PROJ_ROWS)

    def norm_rows(r):
        rows = slice(r, r + PROJ_ROWS)
        h = _norm_mod(x_ref[rows, :], g_ref[...], mod_ref[3:4, :], mod_ref[4:5, :])
        h_scr[rows, :] = h.astype(BF16)

    def glu(r):
        rows = slice(r, r + PROJ_ROWS)
        h = h_scr[rows, :]
        a = jnp.dot(h, wa_ref[...], preferred_element_type=F32) + ba_ref[...]
        gt = jnp.dot(h, wg_ref[...], preferred_element_type=F32) + bg_ref[...]
        u_ref[rows, :] = a * jax.nn.sigmoid(gt)

    @pl.when(j == 0)
    def _():
        for r in starts:
            norm_rows(r)
            glu(r)

    @pl.when(j > 0)
    def _():
        for r in starts:
            glu(r)


def _pw1(xs, mods, g, w_pw1, b_pw1, layer, o, with_ctx):
    tn = 512
    nj = D // tn
    tm = TM_PROJ
    b3 = b_pw1.reshape(-1, 1, 2 * D)
    return pl.pallas_call(
        _pw1_kernel,
        grid=(_stream_blocks(with_ctx, tm), nj),
        in_specs=[
            pl.BlockSpec((tm, D), lambda i, j: (i, 0)),
            pl.BlockSpec((None, None, N_MOD, D), lambda i, j: (layer, _mod_index(i, tm), 0, 0)),
            pl.BlockSpec((1, D), lambda i, j: (0, 0)),
            pl.BlockSpec((None, D, tn), lambda i, j: (o, 0, j)),
            pl.BlockSpec((None, D, tn), lambda i, j: (o, 0, j + nj)),
            pl.BlockSpec((None, 1, tn), lambda i, j: (o, 0, j)),
            pl.BlockSpec((None, 1, tn), lambda i, j: (o, 0, j + nj)),
        ],
        out_specs=pl.BlockSpec((tm, tn), lambda i, j: (i, j)),
        out_shape=jax.ShapeDtypeStruct((T_PAD, D), F32),
        scratch_shapes=[pltpu.VMEM((tm, D), BF16)],
        compiler_params=_cparams(("parallel", "arbitrary")),
        name="conf_pw1",
    )(xs, mods, g.reshape(1, D), w_pw1, w_pw1, b3, b3)


NAT_G = 8
NAT_KR = {1: 8, 2: 10, 4: 12, 8: 16}[NAT_G]


def _natten_geometry():
    kstart, types, type_id = [], [], []
    gi = np.arange(NAT_G)[:, None]
    kj = np.arange(NAT_KR)[None, :]
    for g in range(GRID_H // NAT_G):
        r0 = g * NAT_G
        ks = min(max(r0 - WIN_R // 2, 0), GRID_H - NAT_KR)
        qrow, krow = r0 + gi, ks + kj
        rs = np.clip(qrow - WIN_R // 2, 0, GRID_H - WIN_R)
        valid = (krow >= rs) & (krow < rs + WIN_R)
        assert (valid.sum(axis=1) == WIN_R).all()
        dr = np.where(valid, krow - qrow + WIN_R - 1, 2 * WIN_R - 1).astype(np.int32)
        for t, d0 in enumerate(types):
            if (d0 == dr).all():
                type_id.append(t)
                break
        else:
            type_id.append(len(types))
            types.append(dr)
        kstart.append(ks)
    return np.asarray(kstart, np.int32), np.asarray(type_id, np.int32), np.stack(types)


def _natten_bias(rpb):
    _, _, dr = _natten_geometry()
    ntypes = dr.shape[0]
    c = np.arange(GRID_W)[:, None]
    kc = np.arange(GRID_W)[None, :]
    cs = np.clip(c - WIN_C // 2, 0, GRID_W - WIN_C)
    cvalid = (kc >= cs) & (kc < cs + WIN_C)
    dc = kc - c + WIN_C - 1
    onehot = ((np.arange(2 * WIN_C - 1)[:, None, None] == dc[None]) & cvalid[None]).astype(np.float32)
    colmask = np.where(cvalid, 0.0, NEG).astype(np.float32)
    bcols = jnp.einsum("hrd,dck->hrck", rpb, jnp.asarray(onehot), precision=lax.Precision.HIGHEST)
    bcols = (bcols + jnp.asarray(colmask)).transpose(0, 2, 1, 3)
    slabs = []
    for ty in range(ntypes):
        for gi in range(NAT_G):
            kj = np.nonzero(dr[ty, gi] != 2 * WIN_R - 1)[0]
            a, r0 = int(kj[0]), int(dr[ty, gi, kj[0]])
            assert (kj == a + np.arange(WIN_R)).all() and (dr[ty, gi, kj] == r0 + np.arange(WIN_R)).all()
            s = bcols[:, :, r0:r0 + WIN_R, :].reshape(NH, GRID_W, WIN_R * GRID_W)
            slabs.append(jnp.pad(s, ((0, 0), (0, 0), (a * GRID_W, (NAT_KR - a - WIN_R) * GRID_W)),
                                 constant_values=NEG))
    return jnp.stack(slabs, axis=1).reshape(NH, ntypes, NAT_G * GRID_W, NAT_KR * GRID_W)


def _natten_kernel(ks_ref, ty_ref, q_ref, k_ref, v_ref, kc_ref, vc_ref, bias_ref, o_ref):
    gq, kk = NAT_G * GRID_W, NAT_KR * GRID_W
    kc = kc_ref[...]
    vc = vc_ref[...]
    nt = (((1,), (1,)), ((), ()))

    def body(g, carry):
        q0 = pl.multiple_of(g * gq, gq)
        k0 = pl.multiple_of(ks_ref[g] * GRID_W, GRID_W)
        q = q_ref[pl.ds(q0, gq), :]
        k = k_ref[pl.ds(k0, kk), :]
        v = v_ref[pl.ds(k0, kk), :]
        s_loc = lax.dot_general(q, k, nt, preferred_element_type=F32) + bias_ref[ty_ref[g]]
        s_ctx = lax.dot_general(q, kc, nt, preferred_element_type=F32)
        m = jnp.maximum(jnp.max(s_loc, axis=-1, keepdims=True),
                        jnp.max(s_ctx, axis=-1, keepdims=True))
        p_loc = jnp.exp(s_loc - m)
        p_ctx = jnp.exp(s_ctx - m)
        den = jnp.sum(p_loc, axis=-1, keepdims=True) + jnp.sum(p_ctx, axis=-1, keepdims=True)
        o = (jnp.dot(p_loc.astype(BF16), v, preferred_element_type=F32)
             + jnp.dot(p_ctx.astype(BF16), vc, preferred_element_type=F32))
        o_ref[pl.ds(q0, gq), :] = (o / den).astype(BF16)
        return carry

    lax.fori_loop(0, GRID_H // NAT_G, body, 0, unroll=True)


def _natten(qkv, rpb):
    kstart, type_id, _ = _natten_geometry()
    bias = _natten_bias(rpb)
    _, ntypes, gq, kk = bias.shape
    cb = T_LAT // LC
    grid_spec = pltpu.PrefetchScalarGridSpec(
        num_scalar_prefetch=2,
        grid=(NB, NH),
        in_specs=[
            pl.BlockSpec((S, HD), lambda b, h, *_: (b, h)),
            pl.BlockSpec((S, HD), lambda b, h, *_: (b, NH + h)),
            pl.BlockSpec((S, HD), lambda b, h, *_: (b, 2 * NH + h)),
            pl.BlockSpec((LC, HD), lambda b, h, *_: (cb + b, NH + h)),
            pl.BlockSpec((LC, HD), lambda b, h, *_: (cb + b, 2 * NH + h)),
            pl.BlockSpec((None, ntypes, gq, kk), lambda b, h, *_: (h, 0, 0, 0)),
        ],
        out_specs=pl.BlockSpec((S, HD), lambda b, h, *_: (b, h)),
    )
    return pl.pallas_call(
        _natten_kernel,
        grid_spec=grid_spec,
        out_shape=jax.ShapeDtypeStruct((T_ALL, WA), BF16),
        compiler_params=_cparams(("parallel", "parallel")),
        name="natten",
    )(jnp.asarray(kstart), jnp.asarray(type_id), qkv, qkv, qkv, qkv, qkv, bias)


def _ctx_attn_kernel(q_ref, k_ref, v_ref, att_hbm, o_ref):
    del att_hbm
    s = lax.dot_general(q_ref[...], k_ref[...], (((1,), (1,)), ((), ())),
                        preferred_element_type=F32)
    m = jnp.max(s, axis=-1, keepdims=True)
    p = jnp.exp(s - m)
    den = jnp.sum(p, axis=-1, keepdims=True)
    o = jnp.dot(p.astype(BF16), v_ref[...], preferred_element_type=F32)
    o_ref[...] = (o / den).astype(BF16)


def _ctx_attn(qkv, att):
    cb = T_LAT // LC
    return pl.pallas_call(
        _ctx_attn_kernel,
        grid=(NB, NH),
        in_specs=[
            pl.BlockSpec((LC, HD), lambda b, h: (cb + b, h)),
            pl.BlockSpec((LC, HD), lambda b, h: (cb + b, NH + h)),
            pl.BlockSpec((LC, HD), lambda b, h: (cb + b, 2 * NH + h)),
            pl.BlockSpec(memory_space=pl.ANY),
        ],
        out_specs=pl.BlockSpec((LC, HD), lambda b, h: (cb + b, h)),
        out_shape=jax.ShapeDtypeStruct(att.shape, att.dtype),
        input_output_aliases={3: 0},
        compiler_params=_cparams(("parallel", "parallel")),
        name="ctx_attn",
    )(qkv, qkv, qkv, att)


def _dft_tables(length):
    n2 = 4 * length
    r = jnp.arange(length, dtype=jnp.int32)[:, None]
    hi = jnp.arange(length // LANE, dtype=jnp.int32)[None, :]
    lo = jnp.arange(LANE, dtype=jnp.int32)[None, :]

    def cs(phase):
        ang = (phase % n2).astype(F32) * (2.0 * math.pi / n2)
        return jnp.cos(ang), jnp.sin(ang)

    kt = cs((2 * r + 1) * (LANE * hi)) + cs((2 * r + 1) * lo)
    tk = cs(r * (2 * LANE * hi)) + cs(r * (2 * lo + 1))
    return kt, tk


def _dftgen_kernel(ca_ref, sa_ref, cb_ref, sb_ref, c_ref, s_ref, *, nt, sgn):
    cb = cb_ref[...]
    sb = sb_ref[...]
    for t1 in range(nt):
        ca = ca_ref[:, t1:t1 + 1]
        sa = sa_ref[:, t1:t1 + 1]
        sl = slice(t1 * LANE, (t1 + 1) * LANE)
        c_ref[:, sl] = (ca * cb - sa * sb).astype(BF16)
        s_ref[:, sl] = (sgn * (sa * cb + ca * sb)).astype(BF16)


def _dftgen(tables, length, sgn):
    tr = min(256, length)
    nt = length // LANE
    row = lambda i: (i, 0)
    return pl.pallas_call(
        functools.partial(_dftgen_kernel, nt=nt, sgn=sgn),
        grid=(length // tr,),
        in_specs=[pl.BlockSpec((tr, nt), row), pl.BlockSpec((tr, nt), row),
                  pl.BlockSpec((tr, LANE), row), pl.BlockSpec((tr, LANE), row)],
        out_specs=[pl.BlockSpec((tr, length), row), pl.BlockSpec((tr, length), row)],
        out_shape=[jax.ShapeDtypeStruct((length, length), BF16)] * 2,
        compiler_params=_cparams(("parallel",)),
        name="dft_gen",
    )(*tables)


def _hy_pre_kernel(z0_ref, z1_ref, z2_ref, w_ref, b_ref, u16_ref, u32_ref, x0_ref):
    def conv(z_ref, part):
        z = z_ref[...]
        n = z.shape[0]
        row = lax.broadcasted_iota(jnp.int32, z.shape, 0)
        zm = jnp.where(row == 0, 0.0, pltpu.roll(z, 1, 0))
        zp = jnp.where(row == n - 1, 0.0, pltpu.roll(z, n - 1, 0))
        w = w_ref[part]
        return zm * w[0:1, :] + z * w[1:2, :] + zp * w[2:3, :] + b_ref[part]

    x0_ref[...] = conv(z0_ref, 0)
    u = conv(z2_ref, 2) * conv(z1_ref, 1)
    u32_ref[...] = u
    u16_ref[...] = u.astype(BF16)


def _hy_pre(z, short_w, short_b, length, row_off):
    tc = LANE
    nc = WB // tc
    rb = row_off // length
    w = short_w.reshape(3, 3, WB).transpose(1, 0, 2)
    bb = short_b.reshape(3, 1, WB)
    out_spec = pl.BlockSpec((length, tc), lambda b, c: (0, b * nc + c))
    return pl.pallas_call(
        _hy_pre_kernel,
        grid=(NB, nc),
        in_specs=[
            pl.BlockSpec((length, tc), lambda b, c: (rb + b, c)),
            pl.BlockSpec((length, tc), lambda b, c: (rb + b, nc + c)),
            pl.BlockSpec((length, tc), lambda b, c: (rb + b, 2 * nc + c)),
            pl.BlockSpec((3, 3, tc), lambda b, c: (0, 0, c)),
            pl.BlockSpec((3, 1, tc), lambda b, c: (0, 0, c)),
        ],
        out_specs=[out_spec, out_spec, out_spec],
        out_shape=[jax.ShapeDtypeStruct((length, NB * WB), BF16),
                   jax.ShapeDtypeStruct((length, NB * WB), F32),
                   jax.ShapeDtypeStruct((length, NB * WB), F32)],
        compiler_params=_cparams(("parallel", "parallel")),
        name="hyena_pre",
    )(z, z, z, w, bb)


def _hy_filter_kernel(z_ref, w0_ref, b0_ref, w1_ref, b1_ref, w2_ref, b2_ref, w3_ref, fr_ref,
                      dl_ref, fs_ref, fd_ref):
    hp = lax.Precision.HIGHEST
    z = z_ref[...]
    fr = fr_ref[...]
    h = jnp.sin(fr * (jnp.dot(z, w0_ref[...], precision=hp, preferred_element_type=F32) + b0_ref[...]))
    h = jnp.sin(fr * (jnp.dot(h, w1_ref[...], precision=hp, preferred_element_type=F32) + b1_ref[...]))
    h = jnp.sin(fr * (jnp.dot(h, w2_ref[...], precision=hp, preferred_element_type=F32) + b2_ref[...]))
    hh = jnp.dot(h.astype(BF16), w3_ref[...].astype(BF16), preferred_element_type=F32)
    win = jnp.exp(-z[:, 0:1] * dl_ref[...])
    fwd = hh[:, :WB] * win
    bwd = hh[:, WB:] * win
    row = lax.broadcasted_iota(jnp.int32, bwd.shape, 0) + pl.program_id(0) * z.shape[0]
    bwd = jnp.where(row == 0, 0.0, bwd)
    fs_ref[...] = (fwd + bwd).astype(BF16)
    fd_ref[...] = (bwd - fwd).astype(BF16)


def _hy_filter(length, w0, b0, w1, b1, w2, b2, w3, freq):
    t = jnp.linspace(0.0, 1.0, length, dtype=F32)[:, None]
    bands = (HY_EMB - 1) // 2
    f = jnp.linspace(1e-4, bands - 1, bands, dtype=F32)
    w = 2 * math.pi * jnp.arange(length, dtype=F32)[:, None] / length
    z = jnp.concatenate([t, jnp.cos(f * w), -jnp.sin(f * w)], axis=-1)
    emb = HY_ORDER
    z = jnp.pad(z, ((0, 0), (0, emb - HY_EMB)))
    w0p = jnp.pad(w0, ((0, emb - HY_EMB), (0, 0)))
    max_decay = math.log(1e-2) / 0.3
    min_decay = math.log(1e-2) / 1.5
    deltas = jnp.abs(jnp.linspace(min_decay, max_decay, WB, dtype=F32))[None, :]
    tt = min(256, length)
    full = lambda shape: pl.BlockSpec(shape, lambda i: (0,) * len(shape))
    row = lambda i: (i, 0)
    vec = lambda a: a.reshape(1, -1)
    return pl.pallas_call(
        _hy_filter_kernel,
        grid=(length // tt,),
        in_specs=[pl.BlockSpec((tt, emb), row),
                  full((emb, HY_ORDER)), full((1, HY_ORDER)),
                  full((HY_ORDER, HY_ORDER)), full((1, HY_ORDER)),
                  full((HY_ORDER, HY_ORDER)), full((1, HY_ORDER)),
                  full((HY_ORDER, 2 * WB)), full((1, HY_ORDER)), full((1, WB))],
        out_specs=[pl.BlockSpec((tt, WB), row), pl.BlockSpec((tt, WB), row)],
        out_shape=[jax.ShapeDtypeStruct((length, WB), BF16)] * 2,
        compiler_params=_cparams(("parallel",)),
        name="hyena_filter",
    )(z, w0p, vec(b0), w1, vec(b1), w2, vec(b2), w3, vec(freq), deltas)


def _dft_tiles(length):
    return min(512, length), 512


def _dft_spec_kernel(c_ref, s_ref, fs_ref, fd_ref, a_ref, b_ref, *, scale):
    a_ref[...] = jnp.dot(c_ref[...], fs_ref[...], preferred_element_type=F32) * scale
    b_ref[...] = jnp.dot(s_ref[...], fd_ref[...], preferred_element_type=F32) * scale


def _dft_spec(cm, sm, fs, fd, length):
    tr, tc = _dft_tiles(length)
    return pl.pallas_call(
        functools.partial(_dft_spec_kernel, scale=1.0 / length),
        grid=(WB // tc, length // tr),
        in_specs=[pl.BlockSpec((tr, length), lambda j, i: (i, 0)),
                  pl.BlockSpec((tr, length), lambda j, i: (i, 0)),
                  pl.BlockSpec((length, tc), lambda j, i: (0, j)),
                  pl.BlockSpec((length, tc), lambda j, i: (0, j))],
        out_specs=[pl.BlockSpec((tr, tc), lambda j, i: (i, j))] * 2,
        out_shape=[jax.ShapeDtypeStruct((length, WB), F32)] * 2,
        compiler_params=_cparams(("parallel", "parallel")),
        name="hyena_filter_dft",
    )(cm, sm, fs, fd)


def _dft_fwd_kernel(c_ref, s_ref, u_ref, a_ref, b_ref, re_ref, im_ref):
    u = u_ref[...]
    p = jnp.dot(c_ref[...], u, preferred_element_type=F32)
    q = jnp.dot(s_ref[...], u, preferred_element_type=F32)
    a = a_ref[...]
    b = b_ref[...]
    re_ref[...] = (p * a + q * b).astype(BF16)
    im_ref[...] = (p * b - q * a).astype(BF16)


def _dft_fwd(cm, sm, u16, a, bq, length):
    tr, tc = _dft_tiles(length)
    ncj = WB // tc
    return pl.pallas_call(
        _dft_fwd_kernel,
        grid=(NB * ncj, length // tr),
        in_specs=[pl.BlockSpec((tr, length), lambda j, i: (i, 0)),
                  pl.BlockSpec((tr, length), lambda j, i: (i, 0)),
                  pl.BlockSpec((length, tc), lambda j, i: (0, j)),
                  pl.BlockSpec((tr, tc), lambda j, i: (i, j % ncj)),
                  pl.BlockSpec((tr, tc), lambda j, i: (i, j % ncj))],
        out_specs=[pl.BlockSpec((tr, tc), lambda j, i: (i, j))] * 2,
        out_shape=[jax.ShapeDtypeStruct((length, NB * WB), BF16)] * 2,
        compiler_params=_cparams(("parallel", "parallel")),
        name="hyena_fwd_dft",
    )(cm, sm, u16, a, bq)


def _dft_inv_kernel(c_ref, s_ref, re_ref, im_ref, u_ref, x0_ref, bias_ref, *rest):
    o_ref = rest[-1]
    y = (jnp.dot(c_ref[...], re_ref[...], preferred_element_type=F32)
         + jnp.dot(s_ref[...], im_ref[...], preferred_element_type=F32))
    o_ref[...] = ((y + u_ref[...] * bias_ref[...]) * x0_ref[...]).astype(BF16)


def _dft_inv(ct, snt, re, im, u32, x0, bias, length, row_off, hy=None):
    tr, tc = _dft_tiles(length)
    ncj = WB // tc
    nri = length // tr
    rb = row_off // tr
    aliased = hy is not None
    in_specs = [pl.BlockSpec((tr, length), lambda j, i: (i, 0)),
                pl.BlockSpec((tr, length), lambda j, i: (i, 0)),
                pl.BlockSpec((length, tc), lambda j, i: (0, j)),
                pl.BlockSpec((length, tc), lambda j, i: (0, j)),
                pl.BlockSpec((tr, tc), lambda j, i: (i, j)),
                pl.BlockSpec((tr, tc), lambda j, i: (i, j)),
                pl.BlockSpec((1, tc), lambda j, i: (0, j % ncj))]
    args = [ct, snt, re, im, u32, x0, bias.reshape(1, WB)]
    if aliased:
        in_specs.append(pl.BlockSpec(memory_space=pl.ANY))
        args.append(hy)
    return pl.pallas_call(
        _dft_inv_kernel,
        grid=(NB * ncj, nri),
        in_specs=in_specs,
        out_specs=pl.BlockSpec((tr, tc), lambda j, i: (rb + (j // ncj) * nri + i, j % ncj)),
        out_shape=jax.ShapeDtypeStruct((T_ALL, WB), BF16),
        input_output_aliases={7: 0} if aliased else {},
        compiler_params=_cparams(("parallel", "parallel")),
        name="hyena_inv_dft",
    )(*args)


def _hyena(z, hy_params, mats, length, row_off, hy=None):
    short_w, short_b, w0, b0, w1, b1, w2, b2, w3, freq, bias = hy_params
    ck, sk, ct, snt = mats
    u16, u32, x0 = _hy_pre(z, short_w, short_b, length, row_off)
    fs, fd = _hy_filter(length, w0, b0, w1, b1, w2, b2, w3, freq)
    a, bq = _dft_spec(ck, sk, fs, fd, length)
    re, im = _dft_fwd(ck, sk, u16, a, bq, length)
    return _dft_inv(ct, snt, re, im, u32, x0, bias, length, row_off, hy)


def _dft_mats(length):
    kt, tk = _dft_tables(length)
    ck, sk = _dftgen(kt, length, 1.0)
    ct, snt = _dftgen(tk, length, -1.0)
    return ck, sk, ct, snt


def _outproj_kernel(x_ref, mod_ref, a_ref, y_ref, wa_ref, wy_ref, o_ref):
    o = (jnp.dot(a_ref[...], wa_ref[...], preferred_element_type=F32)
         + jnp.dot(y_ref[...], wy_ref[...], preferred_element_type=F32))
    o_ref[...] = x_ref[...] + mod_ref[5:6, :] * o


def _outproj(xs, mods, att, hy, w_out, layer, e, nblk):
    return pl.pallas_call(
        _outproj_kernel,
        grid=(nblk,),
        in_specs=[
            pl.BlockSpec((TM, D), lambda i: (i, 0)),
            pl.BlockSpec((None, None, N_MOD, D), lambda i: (layer, _mod_index(i, TM), 0, 0)),
            pl.BlockSpec((TM, WA), lambda i: (i, 0)),
            pl.BlockSpec((TM, WB), lambda i: (i, 0)),
            pl.BlockSpec((None, WA, D), lambda i: (e, 0, 0)),
            pl.BlockSpec((None, WB, D), lambda i: (e, 1, 0)),
        ],
        out_specs=pl.BlockSpec((TM, D), lambda i: (i, 0)),
        out_shape=jax.ShapeDtypeStruct(xs.shape, F32),
        input_output_aliases={0: 0},
        compiler_params=_cparams(("parallel",)),
        name="mix_outproj",
    )(xs, mods, att, hy, w_out, w_out)


CONV_HALO = 16
CONV_RC = 64


def _conf_out_kernel(x_ref, mod_ref, up_ref, uc_ref, un_ref, wdw_ref, bdw_ref, lg_ref, lb_ref,
                     w2_ref, b2_ref, o_ref, ext_scr, cv_scr, ph_scr, *, tm, bps):
    i = pl.program_id(0)
    first = (i % bps) == 0
    last = (i % bps) == bps - 1
    halo = CONV_HALO
    zero = jnp.zeros((halo, D), F32)
    ext_scr[0:halo, :] = jnp.where(first, zero, up_ref[...])
    ext_scr[halo:halo + tm, :] = uc_ref[...]
    ext_scr[halo + tm:2 * halo + tm, :] = jnp.where(last, zero, un_ref[...])
    off = halo - CONV_W // 2

    def col_body(cc, carry):
        c0 = pl.multiple_of(cc * LANE, LANE)
        wv = wdw_ref[:, pl.ds(c0, LANE)]
        bv = bdw_ref[:, pl.ds(c0, LANE)]

        span = ph_scr.shape[1]
        for p in range(1, SUBLANE):
            ph_scr[p] = ext_scr[pl.ds(p, span), pl.ds(c0, LANE)]
        for r0 in range(0, tm, CONV_RC):
            acc = jnp.zeros((CONV_RC, LANE), F32)
            for j in range(CONV_W):
                p = (off + j) % SUBLANE
                base = off + j - p + r0
                if p == 0:
                    rows = ext_scr[pl.ds(base, CONV_RC), pl.ds(c0, LANE)]
                else:
                    rows = ph_scr[p, pl.ds(base, CONV_RC), :]
                acc = acc + rows * wv[j:j + 1, :]
            cv_scr[pl.ds(r0, CONV_RC), pl.ds(c0, LANE)] = acc + bv
        return carry

    lax.fori_loop(0, D // LANE, col_body, 0)

    v = cv_scr[...]
    mu = jnp.mean(v, axis=-1, keepdims=True)
    vc = v - mu
    var = jnp.mean(vc * vc, axis=-1, keepdims=True)
    t = vc * lax.rsqrt(var + LN_EPS) * lg_ref[...] + lb_ref[...]
    t = (t * jax.nn.sigmoid(t)).astype(BF16)
    o = jnp.dot(t, w2_ref[...], preferred_element_type=F32) + b2_ref[...]
    o_ref[...] = x_ref[...] + mod_ref[5:6, :] * o


def _conf_out(xs, mods, u, w_dw, b_dw, ln_g, ln_b, w_pw2, b_pw2, layer, o, tm, row_off, seq, nblk):
    bps = seq // tm
    rb = row_off // tm
    vec = lambda a: a.reshape(-1, 1, D)
    blk = lambda i: (rb + i, 0)
    hpb = tm // CONV_HALO
    prev = lambda i: (jnp.maximum((rb + i) * hpb - 1, 0), 0)
    nxt = lambda i: (jnp.minimum((rb + i + 1) * hpb, T_ALL // CONV_HALO - 1), 0)
    modmap = lambda i: (layer, _mod_index(rb + i, tm), 0, 0)
    return pl.pallas_call(
        functools.partial(_conf_out_kernel, tm=tm, bps=bps),
        grid=(nblk,),
        in_specs=[
            pl.BlockSpec((tm, D), blk),
            pl.BlockSpec((None, None, N_MOD, D), modmap),
            pl.BlockSpec((CONV_HALO, D), prev),
            pl.BlockSpec((tm, D), blk),
            pl.BlockSpec((CONV_HALO, D), nxt),
            pl.BlockSpec((None, CONV_W, D), lambda i: (o, 0, 0)),
            pl.BlockSpec((None, 1, D), lambda i: (o, 0, 0)),
            pl.BlockSpec((None, 1, D), lambda i: (o, 0, 0)),
            pl.BlockSpec((None, 1, D), lambda i: (o, 0, 0)),
            pl.BlockSpec((None, D, D), lambda i: (o, 0, 0)),
            pl.BlockSpec((None, 1, D), lambda i: (o, 0, 0)),
        ],
        out_specs=pl.BlockSpec((tm, D), blk),
        out_shape=jax.ShapeDtypeStruct(xs.shape, F32),
        scratch_shapes=[pltpu.VMEM((tm + 2 * CONV_HALO, D), F32), pltpu.VMEM((tm, D), F32),
                        pltpu.VMEM((SUBLANE, tm + 2 * CONV_HALO - SUBLANE, LANE), F32)],
        input_output_aliases={0: 0},
        compiler_params=_cparams(("parallel",)),
        name="conf_out",
    )(xs, mods, u, u, u, w_dw, vec(b_dw), vec(ln_g), vec(ln_b), w_pw2, vec(b_pw2))


def _cast_pad_kernel(w_ref, o_ref, *, n, axis):
    v = w_ref[...].astype(BF16)
    if axis == 0:
        o_ref[:n, :] = v
        o_ref[n:, :] = jnp.zeros((o_ref.shape[0] - n, o_ref.shape[1]), BF16)
    else:
        o_ref[:, :n] = v
        o_ref[:, n:] = jnp.zeros((o_ref.shape[0], o_ref.shape[1] - n), BF16)


def _cast_pad(w, layer, axis, padded):
    _, r, c = w.shape
    strip = 256
    if axis == 0:
        in_blk, out_blk, out_shape, n = (None, r, strip), (padded, strip), (padded, c), r
        imap, omap = (lambda j: (layer, 0, j)), (lambda j: (0, j))
        steps = c // strip
    else:
        in_blk, out_blk, out_shape, n = (None, strip, c), (strip, padded), (r, padded), c
        imap, omap = (lambda j: (layer, j, 0)), (lambda j: (j, 0))
        steps = r // strip
    return pl.pallas_call(
        functools.partial(_cast_pad_kernel, n=n, axis=axis),
        grid=(steps,),
        in_specs=[pl.BlockSpec(in_blk, imap)],
        out_specs=pl.BlockSpec(out_blk, omap),
        out_shape=jax.ShapeDtypeStruct(out_shape, BF16),
        compiler_params=_cparams(("parallel",)),
        name="cast_pad",
    )(w)


def kernel(x, c, ctx, c_ctx, ada_w, ada_b, norm_g, ff1_w1, ff1_w3, ff1_w2, ff2_w1, ff2_w3, ff2_w2,
           mix_w_in, mix_w_out, na_rpb, hy_short_w, hy_short_b, hy_w0, hy_b0, hy_w1, hy_b1,
           hy_w2, hy_b2, hy_w3, hy_freq, hy_bias, cv_w_pw1, cv_b_pw1, cv_w_dw, cv_b_dw,
           cv_ln_g, cv_ln_b, cv_w_pw2, cv_b_pw2, final_g):
    xs = jnp.concatenate([x.reshape(T_LAT, D), ctx.reshape(T_CTX, D),
                          jnp.zeros((T_PAD - T_ALL, D), F32)], axis=0)
    cond = jnp.concatenate([c, c_ctx[None, :], jnp.zeros((8 - NB - 1, D), F32)], axis=0)
    mods = _ada(cond, ada_w, ada_b)[:, :NB + 1].reshape(DEPTH, NB + 1, N_MOD, D)

    ff1 = (ff1_w1, ff1_w3, ff1_w2)
    ff2 = (ff2_w1, ff2_w3, ff2_w2)
    wts = (_cast_pad(ff1_w1, 0, 1, DFF_PAD), _cast_pad(ff1_w3, 0, 1, DFF_PAD),
           _cast_pad(ff1_w2, 0, 0, DFF_PAD))
    w_in = mix_w_in.astype(BF16)
    w_out = mix_w_out.astype(BF16)
    w_pw1 = cv_w_pw1.astype(BF16)
    w_pw2 = cv_w_pw2.astype(BF16)

    mats_lat = _dft_mats(S)
    mats_ctx = _dft_mats(LC)

    out = None
    for i in range(DEPTH):
        ctx_full = i < LAST_EVEN
        xs, wts = _ffn(xs, mods, norm_g[i, 0], *wts, i, 0, i <= LAST_EVEN, cast_next=(*ff2, i))
        if i % 2 == 0:
            e = i // 2
            hyp = (hy_short_w[e], hy_short_b[e], hy_w0[e], hy_b0[e], hy_w1[e], hy_b1[e],
                   hy_w2[e], hy_b2[e], hy_w3[e], hy_freq[e], hy_bias[e])
            qkv, z = _inproj(xs, mods, norm_g[i, 1], w_in, i, e, i <= LAST_EVEN)
            att = _natten(qkv, na_rpb[e])
            hy = _hyena(z, hyp, mats_lat, S, 0)
            if ctx_full:
                att = _ctx_attn(qkv, att)
                hy = _hyena(z, hyp, mats_ctx, LC, T_LAT, hy)
            xs = _outproj(xs, mods, att, hy, w_out, i, e, NBLK_ALL if ctx_full else NBLK_LAT)
        else:
            o = i // 2
            u = _pw1(xs, mods, norm_g[i, 1], w_pw1, cv_b_pw1, i, o, ctx_full)
            cv = (cv_w_dw, cv_b_dw, cv_ln_g, cv_ln_b, w_pw2, cv_b_pw2)
            xs = _conf_out(xs, mods, u, *cv, i, o, TM, 0, S, NBLK_LAT)
            if ctx_full:
                xs = _conf_out(xs, mods, u, *cv, i, o, LC, T_LAT, LC, T_CTX // LC)
        if i == DEPTH - 1:
            out = _ffn(xs, mods, norm_g[i, 2], *wts, i, 6, ctx_full, final_g=final_g)
        else:
            xs, wts = _ffn(xs, mods, norm_g[i, 2], *wts, i, 6, ctx_full, cast_next=(*ff1, i + 1))
    return out.reshape(NB, S, D)
```

```python
import functools
import math

import numpy as np
import jax
import jax.numpy as jnp
from jax import lax
from jax.experimental import pallas as pl
from jax.experimental.pallas import tpu as pltpu

F32 = jnp.float32
BF16 = jnp.bfloat16

D = 2048
NB = 2
S = 4096
DEPTH = 4
GRID_W = 64
GRID_H = S // GRID_W
LC = 256
HD = 128
NH = 8
WA = NH * HD
WB = D - WA
WIN_R = 8
WIN_C = 16
DFF = 5504
HY_EMB = 33
HY_ORDER = 64
CONV_W = 31
N_MOD = 9
RMS_EPS = 1e-6
LN_EPS = 1e-5
LAST_EVEN = (DEPTH - 1) - ((DEPTH - 1) % 2)

T_LAT = NB * S
T_CTX = NB * LC
T_ALL = T_LAT + T_CTX
T_PAD = 9216

LANE = 128
SUBLANE = 8
TM = 512
TM_FFN = 1024
TM_PROJ = 1024
PROJ_ROWS = 512
FFN_ROWS = 512
CAST_COL_SLABS = 64
CAST_ROW = 128
TF = 512
DFF_PAD = TF * (-(-DFF // TF))
CAST_ROW_SLABS = DFF_PAD // CAST_ROW
NBLK_LAT = T_LAT // TM
NBLK_ALL = T_ALL // TM
NEG = -1e30
VMEM_LIMIT = 60 * 1024 * 1024


def _cparams(sem, vmem=VMEM_LIMIT):
    return pltpu.CompilerParams(dimension_semantics=sem, vmem_limit_bytes=vmem)


def _mod_index(i, tm):
    return jnp.minimum(i // (S // tm), NB)


def _stream_blocks(with_ctx, tm):
    return pl.cdiv(T_ALL if with_ctx else T_LAT, tm)


def _norm_mod(x, g, shift, scale):
    ms = jnp.mean(x * x, axis=-1, keepdims=True)
    return (x * lax.rsqrt(ms + RMS_EPS)) * g * (1.0 + scale) + shift


def _ada_kernel(s_ref, w_ref, b_ref, o_ref):
    s = s_ref[...]
    s = (s * jax.nn.sigmoid(s)).astype(BF16)
    o_ref[...] = jnp.dot(s, w_ref[...].astype(BF16), preferred_element_type=F32) + b_ref[...]


def _ada(cond, ada_w, ada_b):
    tn = 1024
    n = N_MOD * D
    return pl.pallas_call(
        _ada_kernel,
        grid=(DEPTH, n // tn),
        in_specs=[
            pl.BlockSpec((8, D), lambda l, j: (0, 0)),
            pl.BlockSpec((None, D, tn), lambda l, j: (l, 0, j)),
            pl.BlockSpec((None, 1, tn), lambda l, j: (l, 0, j)),
        ],
        out_specs=pl.BlockSpec((None, 8, tn), lambda l, j: (l, 0, j)),
        out_shape=jax.ShapeDtypeStruct((DEPTH, 8, n), F32),
        compiler_params=_cparams(("parallel", "parallel")),
        name="ada_mod",
    )(cond, ada_w, ada_b.reshape(DEPTH, 1, n))


def _ffn_kernel(x_ref, mod_ref, g_ref, w1_ref, w3_ref, w2_ref, *rest, jbase, nf, final, full_blocks,
                tail_rows, casting):
    rest = list(rest)
    fg_ref = rest.pop(0) if final else None
    cast_in = [rest.pop(0) for _ in range(3)] if casting else []
    o_ref = rest.pop(0)
    cast_out = [rest.pop(0) for _ in range(3)] if casting else []
    (h_scr,) = rest
    i = pl.program_id(0)
    f = pl.program_id(1)
    tm = h_scr.shape[0]

    if casting:
        s = i * nf + f

        @pl.when(s < CAST_COL_SLABS)
        def _():
            for src, dst in zip(cast_in[:2], cast_out[:2]):
                dst[:, :DFF] = src[...].astype(BF16)
                dst[:, DFF:] = jnp.zeros((dst.shape[0], DFF_PAD - DFF), BF16)

        @pl.when(s < DFF // CAST_ROW)
        def _():
            cast_out[2][...] = cast_in[2][...].astype(BF16)

        @pl.when((s >= DFF // CAST_ROW) & (s < CAST_ROW_SLABS))
        def _():
            cast_out[2][...] = jnp.zeros(cast_out[2].shape, BF16)

    def norm_rows(r):
        rows = slice(r, r + FFN_ROWS)
        h = _norm_mod(x_ref[rows, :], g_ref[...], mod_ref[jbase:jbase + 1, :],
                      mod_ref[jbase + 1:jbase + 2, :])
        h_scr[rows, :] = h.astype(BF16)

    def chain(r, first):
        rows = slice(r, r + FFN_ROWS)
        h = h_scr[rows, :]
        a = jnp.dot(h, w1_ref[...], preferred_element_type=F32)
        b = jnp.dot(h, w3_ref[...], preferred_element_type=F32)
        gate = (a * jax.nn.sigmoid(a) * b).astype(BF16)
        part = jnp.dot(gate, w2_ref[...], preferred_element_type=F32)
        if first:
            o_ref[rows, :] = part
        else:
            o_ref[rows, :] += part

    def epilogue(r):
        rows = slice(r, r + FFN_ROWS)
        y = x_ref[rows, :] + (0.5 * mod_ref[jbase + 2:jbase + 3, :]) * o_ref[rows, :]
        if final:
            ms = jnp.mean(y * y, axis=-1, keepdims=True)
            y = (y * lax.rsqrt(ms + RMS_EPS)) * fg_ref[...]
        o_ref[rows, :] = y

    def step(live):
        starts = range(0, live, FFN_ROWS)

        @pl.when(f == 0)
        def _():
            for r in starts:
                norm_rows(r)
                chain(r, True)
            if live < tm:
                o_ref[live:, :] = jnp.zeros((tm - live, o_ref.shape[1]), F32)

        @pl.when((f > 0) & (f < nf - 1))
        def _():
            for r in starts:
                chain(r, False)

        @pl.when(f == nf - 1)
        def _():
            for r in starts:
                chain(r, False)
                epilogue(r)

    if tail_rows == tm:
        step(tm)
    else:
        @pl.when(i < full_blocks)
        def _():
            step(tm)

        @pl.when(i >= full_blocks)
        def _():
            step(tail_rows)


def _ffn(xs, mods, g, w1, w3, w2, layer, jbase, with_ctx, final_g=None, cast_next=None):
    nf = DFF_PAD // TF
    final = final_g is not None
    tm = TM_FFN
    nblk = _stream_blocks(with_ctx, tm)
    in_specs = [
        pl.BlockSpec((tm, D), lambda i, f: (i, 0)),
        pl.BlockSpec((None, None, N_MOD, D),
                     lambda i, f: (layer, _mod_index(i, tm), 0, 0)),
        pl.BlockSpec((1, D), lambda i, f: (0, 0)),
        pl.BlockSpec((D, TF), lambda i, f: (0, f)),
        pl.BlockSpec((D, TF), lambda i, f: (0, f)),
        pl.BlockSpec((TF, D), lambda i, f: (f, 0)),
    ]
    args = [xs, mods, g.reshape(1, D), w1, w3, w2]
    out_specs = [pl.BlockSpec((tm, D), lambda i, f: (i, 0))]
    if final:
        in_specs.append(pl.BlockSpec((1, D), lambda i, f: (0, 0)))
        args.append(final_g.reshape(1, D))
        out_shape = [jax.ShapeDtypeStruct((nblk * tm, D), F32)]
        aliases = {}
    else:
        out_shape = [jax.ShapeDtypeStruct(xs.shape, F32)]
        aliases = {0: 0}
    if cast_next is not None:
        c1, c3, c2, nlayer = cast_next
        assert nblk * nf >= max(CAST_COL_SLABS, CAST_ROW_SLABS)
        step = lambda i, f: i * nf + f
        col_in = pl.BlockSpec((None, D // CAST_COL_SLABS, DFF),
                              lambda i, f: (nlayer, jnp.minimum(step(i, f), CAST_COL_SLABS - 1), 0))
        col_out = pl.BlockSpec((D // CAST_COL_SLABS, DFF_PAD),
                               lambda i, f: (jnp.minimum(step(i, f), CAST_COL_SLABS - 1), 0))
        row_in = pl.BlockSpec((None, CAST_ROW, D),
                              lambda i, f: (nlayer, jnp.minimum(step(i, f), DFF // CAST_ROW - 1), 0))
        row_out = pl.BlockSpec((CAST_ROW, D),
                               lambda i, f: (jnp.minimum(step(i, f), CAST_ROW_SLABS - 1), 0))
        in_specs += [col_in, col_in, row_in]
        args += [c1, c3, c2]
        out_specs += [col_out, col_out, row_out]
        out_shape += [jax.ShapeDtypeStruct((D, DFF_PAD), BF16), jax.ShapeDtypeStruct((D, DFF_PAD), BF16),
                      jax.ShapeDtypeStruct((DFF_PAD, D), BF16)]
    res = pl.pallas_call(
        functools.partial(_ffn_kernel, jbase=jbase, nf=nf, final=final, full_blocks=T_LAT // tm,
                          tail_rows=T_ALL - T_LAT if with_ctx else tm, casting=cast_next is not None),
        grid=(nblk, nf),
        in_specs=in_specs,
        out_specs=out_specs,
        out_shape=out_shape,
        scratch_shapes=[pltpu.VMEM((tm, D), BF16)],
        input_output_aliases=aliases,
        compiler_params=_cparams(("arbitrary", "arbitrary")),
        name="ffn",
    )(*args)
    return (res[0], tuple(res[1:])) if cast_next is not None else res[0]


def _inproj_kernel(x_ref, mod_ref, g_ref, w_ref, qkv_ref, z_ref, h_scr, *, nq):
    j = pl.program_id(1)
    starts = range(0, h_scr.shape[0], PROJ_ROWS)

    def norm_rows(r):
        rows = slice(r, r + PROJ_ROWS)
        h = _norm_mod(x_ref[rows, :], g_ref[...], mod_ref[3:4, :], mod_ref[4:5, :])
        h_scr[rows, :] = h.astype(BF16)

    def proj(r):
        return jnp.dot(h_scr[r:r + PROJ_ROWS, :], w_ref[...], preferred_element_type=F32)

    @pl.when(j == 0)
    def _():
        for r in starts:
            norm_rows(r)
            qkv_ref[r:r + PROJ_ROWS, :] = (proj(r) * (HD ** -0.5)).astype(BF16)

    @pl.when((j > 0) & (j < nq))
    def _():
        for r in starts:
            qkv_ref[r:r + PROJ_ROWS, :] = proj(r).astype(BF16)

    @pl.when(j >= nq)
    def _():
        for r in starts:
            z_ref[r:r + PROJ_ROWS, :] = proj(r)


def _inproj(xs, mods, g, w_in, layer, e, with_ctx):
    tn = WA
    nq = 3 * WA // tn
    nz = 3 * WB // tn
    tm = TM_PROJ
    return pl.pallas_call(
        functools.partial(_inproj_kernel, nq=nq),
        grid=(_stream_blocks(with_ctx, tm), nq + nz),
        in_specs=[
            pl.BlockSpec((tm, D), lambda i, j: (i, 0)),
            pl.BlockSpec((None, None, N_MOD, D), lambda i, j: (layer, _mod_index(i, tm), 0, 0)),
            pl.BlockSpec((1, D), lambda i, j: (0, 0)),
            pl.BlockSpec((None, D, tn), lambda i, j: (e, 0, j)),
        ],
        out_specs=[
            pl.BlockSpec((tm, tn), lambda i, j: (i, jnp.minimum(j, nq - 1))),
            pl.BlockSpec((tm, tn), lambda i, j: (i, jnp.maximum(j - nq, 0))),
        ],
        out_shape=[
            jax.ShapeDtypeStruct((T_PAD, 3 * WA), BF16),
            jax.ShapeDtypeStruct((T_PAD, 3 * WB), F32),
        ],
        scratch_shapes=[pltpu.VMEM((tm, D), BF16)],
        compiler_params=_cparams(("parallel", "arbitrary")),
        name="mix_inproj",
    )(xs, mods, g.reshape(1, D), w_in)


def _pw1_kernel(x_ref, mod_ref, g_ref, wa_ref, wg_ref, ba_ref, bg_ref, u_ref, h_scr):
    j = pl.program_id(1)
    starts = range(0, h_scr.shape[0], PROJ_ROWS)

    def norm_rows(r):
        rows = slice(r, r + PROJ_ROWS)
        h = _norm_mod(x_ref[rows, :], g_ref[...], mod_ref[3:4, :], mod_ref[4:5, :])
        h_scr[rows, :] = h.astype(BF16)

    def glu(r):
        rows = slice(r, r + PROJ_ROWS)
        h = h_scr[rows, :]
        a = jnp.dot(h, wa_ref[...], preferred_element_type=F32) + ba_ref[...]
        gt = jnp.dot(h, wg_ref[...], preferred_element_type=F32) + bg_ref[...]
        u_ref[rows, :] = a * jax.nn.sigmoid(gt)

    @pl.when(j == 0)
    def _():
        for r in starts:
            norm_rows(r)
            glu(r)

    @pl.when(j > 0)
    def _():
        for r in starts:
            glu(r)


def _pw1(xs, mods, g, w_pw1, b_pw1, layer, o, with_ctx):
    tn = 512
    nj = D // tn
    tm = TM_PROJ
    b3 = b_pw1.reshape(-1, 1, 2 * D)
    return pl.pallas_call(
        _pw1_kernel,
        grid=(_stream_blocks(with_ctx, tm), nj),
        in_specs=[
            pl.BlockSpec((tm, D), lambda i, j: (i, 0)),
            pl.BlockSpec((None, None, N_MOD, D), lambda i, j: (layer, _mod_index(i, tm), 0, 0)),
            pl.BlockSpec((1, D), lambda i, j: (0, 0)),
            pl.BlockSpec((None, D, tn), lambda i, j: (o, 0, j)),
            pl.BlockSpec((None, D, tn), lambda i, j: (o, 0, j + nj)),
            pl.BlockSpec((None, 1, tn), lambda i, j: (o, 0, j)),
            pl.BlockSpec((None, 1, tn), lambda i, j: (o, 0, j + nj)),
        ],
        out_specs=pl.BlockSpec((tm, tn), lambda i, j: (i, j)),
        out_shape=jax.ShapeDtypeStruct((T_PAD, D), F32),
        scratch_shapes=[pltpu.VMEM((tm, D), BF16)],
        compiler_params=_cparams(("parallel", "arbitrary")),
        name="conf_pw1",
    )(xs, mods, g.reshape(1, D), w_pw1, w_pw1, b3, b3)


NAT_G = 8
NAT_KR = {1: 8, 2: 10, 4: 12, 8: 16}[NAT_G]


def _natten_geometry():
    kstart, types, type_id = [], [], []
    gi = np.arange(NAT_G)[:, None]
    kj = np.arange(NAT_KR)[None, :]
    for g in range(GRID_H // NAT_G):
        r0 = g * NAT_G
        ks = min(max(r0 - WIN_R // 2, 0), GRID_H - NAT_KR)
        qrow, krow = r0 + gi, ks + kj
        rs = np.clip(qrow - WIN_R // 2, 0, GRID_H - WIN_R)
        valid = (krow >= rs) & (krow < rs + WIN_R)
        assert (valid.sum(axis=1) == WIN_R).all()
        dr = np.where(valid, krow - qrow + WIN_R - 1, 2 * WIN_R - 1).astype(np.int32)
        for t, d0 in enumerate(types):
            if (d0 == dr).all():
                type_id.append(t)
                break
        else:
            type_id.append(len(types))
            types.append(dr)
        kstart.append(ks)
    return np.asarray(kstart, np.int32), np.asarray(type_id, np.int32), np.stack(types)


def _natten_bias(rpb):
    _, _, dr = _natten_geometry()
    ntypes = dr.shape[0]
    c = np.arange(GRID_W)[:, None]
    kc = np.arange(GRID_W)[None, :]
    cs = np.clip(c - WIN_C // 2, 0, GRID_W - WIN_C)
    cvalid = (kc >= cs) & (kc < cs + WIN_C)
    dc = kc - c + WIN_C - 1
    onehot = ((np.arange(2 * WIN_C - 1)[:, None, None] == dc[None]) & cvalid[None]).astype(np.float32)
    colmask = np.where(cvalid, 0.0, NEG).astype(np.float32)
    bcols = jnp.einsum("hrd,dck->hrck", rpb, jnp.asarray(onehot), precision=lax.Precision.HIGHEST)
    bcols = (bcols + jnp.asarray(colmask)).transpose(0, 2, 1, 3)
    slabs = []
    for ty in range(ntypes):
        for gi in range(NAT_G):
            kj = np.nonzero(dr[ty, gi] != 2 * WIN_R - 1)[0]
            a, r0 = int(kj[0]), int(dr[ty, gi, kj[0]])
            assert (kj == a + np.arange(WIN_R)).all() and (dr[ty, gi, kj] == r0 + np.arange(WIN_R)).all()
            s = bcols[:, :, r0:r0 + WIN_R, :].reshape(NH, GRID_W, WIN_R * GRID_W)
            slabs.append(jnp.pad(s, ((0, 0), (0, 0), (a * GRID_W, (NAT_KR - a - WIN_R) * GRID_W)),
                                 constant_values=NEG))
    return jnp.stack(slabs, axis=1).reshape(NH, ntypes, NAT_G * GRID_W, NAT_KR * GRID_W)


def _natten_kernel(ks_ref, ty_ref, q_ref, k_ref, v_ref, kc_ref, vc_ref, bias_ref, o_ref):
    gq, kk = NAT_G * GRID_W, NAT_KR * GRID_W
    kc = kc_ref[...]
    vc = vc_ref[...]
    nt = (((1,), (1,)), ((), ()))

    def body(g, carry):
        q0 = pl.multiple_of(g * gq, gq)
        k0 = pl.multiple_of(ks_ref[g] * GRID_W, GRID_W)
        q = q_ref[pl.ds(q0, gq), :]
        k = k_ref[pl.ds(k0, kk), :]
        v = v_ref[pl.ds(k0, kk), :]
        s_loc = lax.dot_general(q, k, nt, preferred_element_type=F32) + bias_ref[ty_ref[g]]
        s_ctx = lax.dot_general(q, kc, nt, preferred_element_type=F32)
        m = jnp.maximum(jnp.max(s_loc, axis=-1, keepdims=True),
                        jnp.max(s_ctx, axis=-1, keepdims=True))
        p_loc = jnp.exp(s_loc - m)
        p_ctx = jnp.exp(s_ctx - m)
        den = jnp.sum(p_loc, axis=-1, keepdims=True) + jnp.sum(p_ctx, axis=-1, keepdims=True)
        o = (jnp.dot(p_loc.astype(BF16), v, preferred_element_type=F32)
             + jnp.dot(p_ctx.astype(BF16), vc, preferred_element_type=F32))
        o_ref[pl.ds(q0, gq), :] = (o / den).astype(BF16)
        return carry

    lax.fori_loop(0, GRID_H // NAT_G, body, 0, unroll=True)


def _natten(qkv, rpb):
    kstart, type_id, _ = _natten_geometry()
    bias = _natten_bias(rpb)
    _, ntypes, gq, kk = bias.shape
    cb = T_LAT // LC
    grid_spec = pltpu.PrefetchScalarGridSpec(
        num_scalar_prefetch=2,
        grid=(NB, NH),
        in_specs=[
            pl.BlockSpec((S, HD), lambda b, h, *_: (b, h)),
            pl.BlockSpec((S, HD), lambda b, h, *_: (b, NH + h)),
            pl.BlockSpec((S, HD), lambda b, h, *_: (b, 2 * NH + h)),
            pl.BlockSpec((LC, HD), lambda b, h, *_: (cb + b, NH + h)),
            pl.BlockSpec((LC, HD), lambda b, h, *_: (cb + b, 2 * NH + h)),
            pl.BlockSpec((None, ntypes, gq, kk), lambda b, h, *_: (h, 0, 0, 0)),
        ],
        out_specs=pl.BlockSpec((S, HD), lambda b, h, *_: (b, h)),
    )
    return pl.pallas_call(
        _natten_kernel,
        grid_spec=grid_spec,
        out_shape=jax.ShapeDtypeStruct((T_ALL, WA), BF16),
        compiler_params=_cparams(("parallel", "parallel")),
        name="natten",
    )(jnp.asarray(kstart), jnp.asarray(type_id), qkv, qkv, qkv, qkv, qkv, bias)


def _ctx_attn_kernel(q_ref, k_ref, v_ref, att_hbm, o_ref):
    del att_hbm
    s = lax.dot_general(q_ref[...], k_ref[...], (((1,), (1,)), ((), ())),
                        preferred_element_type=F32)
    m = jnp.max(s, axis=-1, keepdims=True)
    p = jnp.exp(s - m)
    den = jnp.sum(p, axis=-1, keepdims=True)
    o = jnp.dot(p.astype(BF16), v_ref[...], preferred_element_type=F32)
    o_ref[...] = (o / den).astype(BF16)


def _ctx_attn(qkv, att):
    cb = T_LAT // LC
    return pl.pallas_call(
        _ctx_attn_kernel,
        grid=(NB, NH),
        in_specs=[
            pl.BlockSpec((LC, HD), lambda b, h: (cb + b, h)),
            pl.BlockSpec((LC, HD), lambda b, h: (cb + b, NH + h)),
            pl.BlockSpec((LC, HD), lambda b, h: (cb + b, 2 * NH + h)),
            pl.BlockSpec(memory_space=pl.ANY),
        ],
        out_specs=pl.BlockSpec((LC, HD), lambda b, h: (cb + b, h)),
        out_shape=jax.ShapeDtypeStruct(att.shape, att.dtype),
        input_output_aliases={3: 0},
        compiler_params=_cparams(("parallel", "parallel")),
        name="ctx_attn",
    )(qkv, qkv, qkv, att)


def _dft_tables(length, parity):
    n2 = 4 * length
    half = length // 2
    r = jnp.arange(half, dtype=jnp.int32)[:, None]
    hi = jnp.arange(half // LANE, dtype=jnp.int32)[None, :]
    lo = jnp.arange(LANE, dtype=jnp.int32)[None, :]

    def cs(phase):
        ang = (phase % n2).astype(F32) * (2.0 * math.pi / n2)
        return jnp.cos(ang), jnp.sin(ang)

    km = cs((2 * r + 1) * (2 * LANE * hi)) + cs((2 * r + 1) * (2 * lo + parity))
    mk = cs((2 * r + parity) * (2 * LANE * hi)) + cs((2 * r + parity) * (2 * lo + 1))
    return km, mk


def _dftgen_kernel(ca_ref, sa_ref, cb_ref, sb_ref, c_ref, s_ref, *, nt, sgn):
    cb = cb_ref[...]
    sb = sb_ref[...]
    for t1 in range(nt):
        ca = ca_ref[:, t1:t1 + 1]
        sa = sa_ref[:, t1:t1 + 1]
        sl = slice(t1 * LANE, (t1 + 1) * LANE)
        c_ref[:, sl] = (ca * cb - sa * sb).astype(BF16)
        s_ref[:, sl] = (sgn * (sa * cb + ca * sb)).astype(BF16)


def _dftgen(tables, length, sgn):
    tr = min(256, length)
    nt = length // LANE
    row = lambda i: (i, 0)
    return pl.pallas_call(
        functools.partial(_dftgen_kernel, nt=nt, sgn=sgn),
        grid=(length // tr,),
        in_specs=[pl.BlockSpec((tr, nt), row), pl.BlockSpec((tr, nt), row),
                  pl.BlockSpec((tr, LANE), row), pl.BlockSpec((tr, LANE), row)],
        out_specs=[pl.BlockSpec((tr, length), row), pl.BlockSpec((tr, length), row)],
        out_shape=[jax.ShapeDtypeStruct((length, length), BF16)] * 2,
        compiler_params=_cparams(("parallel",)),
        name="dft_gen",
    )(*tables)


def _hy_pre_kernel(z0_ref, z1_ref, z2_ref, w_ref, b_ref, u16_ref, u32_ref, x0_ref):
    half = z0_ref.shape[0] // 2

    def conv(z_ref, part):
        ze = z_ref[pl.ds(0, half, stride=2), :]
        zo = z_ref[pl.ds(1, half, stride=2), :]
        row = lax.broadcasted_iota(jnp.int32, ze.shape, 0)
        zo_prev = jnp.where(row == 0, 0.0, pltpu.roll(zo, 1, 0))
        ze_next = jnp.where(row == half - 1, 0.0, pltpu.roll(ze, half - 1, 0))
        w = w_ref[part]
        bias = b_ref[part]
        even = zo_prev * w[0:1, :] + ze * w[1:2, :] + zo * w[2:3, :] + bias
        odd = ze * w[0:1, :] + zo * w[1:2, :] + ze_next * w[2:3, :] + bias
        return even, odd

    x0 = conv(z0_ref, 0)
    x1 = conv(z1_ref, 1)
    v = conv(z2_ref, 2)
    for p in range(2):
        u = v[p] * x1[p]
        x0_ref[p] = x0[p]
        u32_ref[p] = u
        u16_ref[p] = u.astype(BF16)


def _hy_pre(z, short_w, short_b, length, row_off):
    tc = LANE
    nc = WB // tc
    rb = row_off // length
    half = length // 2
    w = short_w.reshape(3, 3, WB).transpose(1, 0, 2)
    bb = short_b.reshape(3, 1, WB)
    out_spec = pl.BlockSpec((2, half, tc), lambda b, c: (0, 0, b * nc + c))
    return pl.pallas_call(
        _hy_pre_kernel,
        grid=(NB, nc),
        in_specs=[
            pl.BlockSpec((length, tc), lambda b, c: (rb + b, c)),
            pl.BlockSpec((length, tc), lambda b, c: (rb + b, nc + c)),
            pl.BlockSpec((length, tc), lambda b, c: (rb + b, 2 * nc + c)),
            pl.BlockSpec((3, 3, tc), lambda b, c: (0, 0, c)),
            pl.BlockSpec((3, 1, tc), lambda b, c: (0, 0, c)),
        ],
        out_specs=[out_spec, out_spec, out_spec],
        out_shape=[jax.ShapeDtypeStruct((2, half, NB * WB), BF16),
                   jax.ShapeDtypeStruct((2, half, NB * WB), F32),
                   jax.ShapeDtypeStruct((2, half, NB * WB), F32)],
        compiler_params=_cparams(("parallel", "parallel")),
        name="hyena_pre",
    )(z, z, z, w, bb)


def _hy_filter_kernel(z_ref, w0_ref, b0_ref, w1_ref, b1_ref, w2_ref, b2_ref, w3_ref, fr_ref,
                      dl_ref, fs_ref, fd_ref):
    hp = lax.Precision.HIGHEST
    z = z_ref[...]
    fr = fr_ref[...]
    h = jnp.sin(fr * (jnp.dot(z, w0_ref[...], precision=hp, preferred_element_type=F32) + b0_ref[...]))
    h = jnp.sin(fr * (jnp.dot(h, w1_ref[...], precision=hp, preferred_element_type=F32) + b1_ref[...]))
    h = jnp.sin(fr * (jnp.dot(h, w2_ref[...], precision=hp, preferred_element_type=F32) + b2_ref[...]))
    hh = jnp.dot(h.astype(BF16), w3_ref[...].astype(BF16), preferred_element_type=F32)
    win = jnp.exp(-z[:, 0:1] * dl_ref[...])
    fwd = hh[:, :WB] * win
    bwd = hh[:, WB:] * win
    row = lax.broadcasted_iota(jnp.int32, bwd.shape, 0) + pl.program_id(0) * z.shape[0]
    bwd = jnp.where(row == 0, 0.0, bwd)
    fs_ref[...] = (fwd + bwd).astype(BF16)
    fd_ref[...] = (bwd - fwd).astype(BF16)


def _hy_filter(length, w0, b0, w1, b1, w2, b2, w3, freq):
    t = jnp.linspace(0.0, 1.0, length, dtype=F32)[:, None]
    bands = (HY_EMB - 1) // 2
    f = jnp.linspace(1e-4, bands - 1, bands, dtype=F32)
    w = 2 * math.pi * jnp.arange(length, dtype=F32)[:, None] / length
    z = jnp.concatenate([t, jnp.cos(f * w), -jnp.sin(f * w)], axis=-1)
    z = jnp.concatenate([z[0::2], z[1::2]], axis=0)
    emb = HY_ORDER
    z = jnp.pad(z, ((0, 0), (0, emb - HY_EMB)))
    w0p = jnp.pad(w0, ((0, emb - HY_EMB), (0, 0)))
    max_decay = math.log(1e-2) / 0.3
    min_decay = math.log(1e-2) / 1.5
    deltas = jnp.abs(jnp.linspace(min_decay, max_decay, WB, dtype=F32))[None, :]
    tt = min(256, length)
    full = lambda shape: pl.BlockSpec(shape, lambda i: (0,) * len(shape))
    row = lambda i: (i, 0)
    vec = lambda a: a.reshape(1, -1)
    return pl.pallas_call(
        _hy_filter_kernel,
        grid=(length // tt,),
        in_specs=[pl.BlockSpec((tt, emb), row),
                  full((emb, HY_ORDER)), full((1, HY_ORDER)),
                  full((HY_ORDER, HY_ORDER)), full((1, HY_ORDER)),
                  full((HY_ORDER, HY_ORDER)), full((1, HY_ORDER)),
                  full((HY_ORDER, 2 * WB)), full((1, HY_ORDER)), full((1, WB))],
        out_specs=[pl.BlockSpec((tt, WB), row), pl.BlockSpec((tt, WB), row)],
        out_shape=[jax.ShapeDtypeStruct((length, WB), BF16)] * 2,
        compiler_params=_cparams(("parallel",)),
        name="hyena_filter",
    )(z, w0p, vec(b0), w1, vec(b1), w2, vec(b2), w3, vec(freq), deltas)


def _dft_tiles(length):
    return min(512, length // 2), 512


def _dft_mat_specs(tr, half):
    return [pl.BlockSpec((tr, half), lambda j, i: (i, 0))] * 4


def _dft_spec_kernel(ce_ref, se_ref, co_ref, so_ref, fs_ref, fd_ref, a_ref, b_ref, *, scale):
    ae = jnp.dot(ce_ref[...], fs_ref[0], preferred_element_type=F32)
    ao = jnp.dot(co_ref[...], fs_ref[1], preferred_element_type=F32)
    be = jnp.dot(se_ref[...], fd_ref[0], preferred_element_type=F32)
    bo = jnp.dot(so_ref[...], fd_ref[1], preferred_element_type=F32)
    a_ref[0] = (ae + ao) * scale
    a_ref[1] = (ae - ao) * scale
    b_ref[0] = (be + bo) * scale
    b_ref[1] = (bo - be) * scale


def _dft_spec(mats, fs, fd, length):
    tr, tc = _dft_tiles(length)
    half = length // 2
    return pl.pallas_call(
        functools.partial(_dft_spec_kernel, scale=1.0 / length),
        grid=(WB // tc, half // tr),
        in_specs=_dft_mat_specs(tr, half) + [pl.BlockSpec((2, half, tc), lambda j, i: (0, 0, j))] * 2,
        out_specs=[pl.BlockSpec((2, tr, tc), lambda j, i: (0, i, j))] * 2,
        out_shape=[jax.ShapeDtypeStruct((2, half, WB), F32)] * 2,
        compiler_params=_cparams(("parallel", "parallel")),
        name="hyena_filter_dft",
    )(*mats, fs, fd)


def _dft_fwd_kernel(ce_ref, se_ref, co_ref, so_ref, u_ref, a_ref, b_ref, re_ref, im_ref):
    ue = u_ref[0]
    uo = u_ref[1]
    pe = jnp.dot(ce_ref[...], ue, preferred_element_type=F32)
    po = jnp.dot(co_ref[...], uo, preferred_element_type=F32)
    qe = jnp.dot(se_ref[...], ue, preferred_element_type=F32)
    qo = jnp.dot(so_ref[...], uo, preferred_element_type=F32)
    for h, (p, q) in enumerate(((pe + po, qe + qo), (pe - po, qo - qe))):
        a = a_ref[h]
        b = b_ref[h]
        re_ref[h] = (p * a + q * b).astype(BF16)
        im_ref[h] = (p * b - q * a).astype(BF16)


def _dft_fwd(mats, u16, a, bq, length):
    tr, tc = _dft_tiles(length)
    half = length // 2
    ncj = WB // tc
    return pl.pallas_call(
        _dft_fwd_kernel,
        grid=(NB * ncj, half // tr),
        in_specs=_dft_mat_specs(tr, half) + [
            pl.BlockSpec((2, half, tc), lambda j, i: (0, 0, j)),
            pl.BlockSpec((2, tr, tc), lambda j, i: (0, i, j % ncj)),
            pl.BlockSpec((2, tr, tc), lambda j, i: (0, i, j % ncj))],
        out_specs=[pl.BlockSpec((2, tr, tc), lambda j, i: (0, i, j))] * 2,
        out_shape=[jax.ShapeDtypeStruct((2, half, NB * WB), BF16)] * 2,
        compiler_params=_cparams(("parallel", "parallel")),
        name="hyena_fwd_dft",
    )(*mats, u16, a, bq)


def _dft_inv_kernel(ce_ref, se_ref, co_ref, so_ref, re_ref, im_ref, u_ref, x0_ref, bias_ref, *rest):
    o_ref, sum_scr, y_scr = rest[-3:]
    tr = ce_ref.shape[0]

    @pl.when(pl.program_id(1) == 0)
    def _():
        re0, re1 = re_ref[0].astype(F32), re_ref[1].astype(F32)
        im0, im1 = im_ref[0].astype(F32), im_ref[1].astype(F32)
        sum_scr[0] = (re0 + re1).astype(BF16)
        sum_scr[1] = (im0 - im1).astype(BF16)
        sum_scr[2] = (re0 - re1).astype(BF16)
        sum_scr[3] = (im0 + im1).astype(BF16)

    ye = (jnp.dot(ce_ref[...], sum_scr[0], preferred_element_type=F32)
          + jnp.dot(se_ref[...], sum_scr[1], preferred_element_type=F32))
    yo = (jnp.dot(co_ref[...], sum_scr[2], preferred_element_type=F32)
          + jnp.dot(so_ref[...], sum_scr[3], preferred_element_type=F32))
    bias = bias_ref[...]
    oe = (ye + u_ref[0] * bias) * x0_ref[0]
    oo = (yo + u_ref[1] * bias) * x0_ref[1]
    for c in range(y_scr.shape[0]):
        lanes = slice(c * LANE, (c + 1) * LANE)
        y_scr[c, pl.ds(0, tr, stride=2), :] = oe[:, lanes]
        y_scr[c, pl.ds(1, tr, stride=2), :] = oo[:, lanes]
        o_ref[:, lanes] = y_scr[c].astype(BF16)


def _dft_inv(mats, re, im, u32, x0, bias, length, row_off, hy=None):
    tr, tc = _dft_tiles(length)
    tr = min(tr, 256)
    half = length // 2
    ncj = WB // tc
    nri = half // tr
    rb = row_off // (2 * tr)
    aliased = hy is not None
    in_specs = _dft_mat_specs(tr, half) + [
        pl.BlockSpec((2, half, tc), lambda j, i: (0, 0, j)),
        pl.BlockSpec((2, half, tc), lambda j, i: (0, 0, j)),
        pl.BlockSpec((2, tr, tc), lambda j, i: (0, i, j)),
        pl.BlockSpec((2, tr, tc), lambda j, i: (0, i, j)),
        pl.BlockSpec((1, tc), lambda j, i: (0, j % ncj))]
    args = [*mats, re, im, u32, x0, bias.reshape(1, WB)]
    if aliased:
        in_specs.append(pl.BlockSpec(memory_space=pl.ANY))
        args.append(hy)
    return pl.pallas_call(
        _dft_inv_kernel,
        grid=(NB * ncj, nri),
        in_specs=in_specs,
        out_specs=pl.BlockSpec((2 * tr, tc), lambda j, i: (rb + (j // ncj) * nri + i, j % ncj)),
        out_shape=jax.ShapeDtypeStruct((T_ALL, WB), BF16),
        scratch_shapes=[pltpu.VMEM((4, half, tc), BF16), pltpu.VMEM((tc // LANE, 2 * tr, LANE), F32)],
        input_output_aliases={len(args) - 1: 0} if aliased else {},
        compiler_params=_cparams(("parallel", "arbitrary")),
        name="hyena_inv_dft",
    )(*args)


def _hyena(z, hy_params, mats, length, row_off, hy=None):
    short_w, short_b, w0, b0, w1, b1, w2, b2, w3, freq, bias = hy_params
    fwd_mats, inv_mats = mats
    half = length // 2
    u16, u32, x0 = _hy_pre(z, short_w, short_b, length, row_off)
    fs, fd = _hy_filter(length, w0, b0, w1, b1, w2, b2, w3, freq)
    a, bq = _dft_spec(fwd_mats, fs.reshape(2, half, WB), fd.reshape(2, half, WB), length)
    re, im = _dft_fwd(fwd_mats, u16, a, bq, length)
    return _dft_inv(inv_mats, re, im, u32, x0, bias, length, row_off, hy)


def _dft_mats(length):
    half = length // 2
    km_e, mk_e = _dft_tables(length, 0)
    km_o, mk_o = _dft_tables(length, 1)
    fwd = _dftgen(km_e, half, 1.0) + _dftgen(km_o, half, 1.0)
    inv = _dftgen(mk_e, half, -1.0) + _dftgen(mk_o, half, -1.0)
    return tuple(fwd), tuple(inv)


def _outproj_kernel(x_ref, mod_ref, a_ref, y_ref, wa_ref, wy_ref, o_ref):
    o = (jnp.dot(a_ref[...], wa_ref[...], preferred_element_type=F32)
         + jnp.dot(y_ref[...], wy_ref[...], preferred_element_type=F32))
    o_ref[...] = x_ref[...] + mod_ref[5:6, :] * o


def _outproj(xs, mods, att, hy, w_out, layer, e, nblk):
    return pl.pallas_call(
        _outproj_kernel,
        grid=(nblk,),
        in_specs=[
            pl.BlockSpec((TM, D), lambda i: (i, 0)),
            pl.BlockSpec((None, None, N_MOD, D), lambda i: (layer, _mod_index(i, TM), 0, 0)),
            pl.BlockSpec((TM, WA), lambda i: (i, 0)),
            pl.BlockSpec((TM, WB), lambda i: (i, 0)),
            pl.BlockSpec((None, WA, D), lambda i: (e, 0, 0)),
            pl.BlockSpec((None, WB, D), lambda i: (e, 1, 0)),
        ],
        out_specs=pl.BlockSpec((TM, D), lambda i: (i, 0)),
        out_shape=jax.ShapeDtypeStruct(xs.shape, F32),
        input_output_aliases={0: 0},
        compiler_params=_cparams(("parallel",)),
        name="mix_outproj",
    )(xs, mods, att, hy, w_out, w_out)


CONV_HALO = 16
CONV_RC = 64


def _conf_out_kernel(x_ref, mod_ref, up_ref, uc_ref, un_ref, wdw_ref, bdw_ref, lg_ref, lb_ref,
                     w2_ref, b2_ref, o_ref, ext_scr, cv_scr, ph_scr, *, tm, bps):
    i = pl.program_id(0)
    first = (i % bps) == 0
    last = (i % bps) == bps - 1
    halo = CONV_HALO
    zero = jnp.zeros((halo, D), F32)
    ext_scr[0:halo, :] = jnp.where(first, zero, up_ref[...])
    ext_scr[halo:halo + tm, :] = uc_ref[...]
    ext_scr[halo + tm:2 * halo + tm, :] = jnp.where(last, zero, un_ref[...])
    off = halo - CONV_W // 2

    def col_body(cc, carry):
        c0 = pl.multiple_of(cc * LANE, LANE)
        wv = wdw_ref[:, pl.ds(c0, LANE)]
        bv = bdw_ref[:, pl.ds(c0, LANE)]

        span = ph_scr.shape[1]
        for p in range(1, SUBLANE):
            ph_scr[p] = ext_scr[pl.ds(p, span), pl.ds(c0, LANE)]
        for r0 in range(0, tm, CONV_RC):
            acc = jnp.zeros((CONV_RC, LANE), F32)
            for j in range(CONV_W):
                p = (off + j) % SUBLANE
                base = off + j - p + r0
                if p == 0:
                    rows = ext_scr[pl.ds(base, CONV_RC), pl.ds(c0, LANE)]
                else:
                    rows = ph_scr[p, pl.ds(base, CONV_RC), :]
                acc = acc + rows * wv[j:j + 1, :]
            cv_scr[pl.ds(r0, CONV_RC), pl.ds(c0, LANE)] = acc + bv
        return carry

    lax.fori_loop(0, D // LANE, col_body, 0)

    v = cv_scr[...]
    mu = jnp.mean(v, axis=-1, keepdims=True)
    vc = v - mu
    var = jnp.mean(vc * vc, axis=-1, keepdims=True)
    t = vc * lax.rsqrt(var + LN_EPS) * lg_ref[...] + lb_ref[...]
    t = (t * jax.nn.sigmoid(t)).astype(BF16)
    o = jnp.dot(t, w2_ref[...], preferred_element_type=F32) + b2_ref[...]
    o_ref[...] = x_ref[...] + mod_ref[5:6, :] * o


def _conf_out(xs, mods, u, w_dw, b_dw, ln_g, ln_b, w_pw2, b_pw2, layer, o, tm, row_off, seq, nblk):
    bps = seq // tm
    rb = row_off // tm
    vec = lambda a: a.reshape(-1, 1, D)
    blk = lambda i: (rb + i, 0)
    hpb = tm // CONV_HALO
    prev = lambda i: (jnp.maximum((rb + i) * hpb - 1, 0), 0)
    nxt = lambda i: (jnp.minimum((rb + i + 1) * hpb, T_ALL // CONV_HALO - 1), 0)
    modmap = lambda i: (layer, _mod_index(rb + i, tm), 0, 0)
    return pl.pallas_call(
        functools.partial(_conf_out_kernel, tm=tm, bps=bps),
        grid=(nblk,),
        in_specs=[
            pl.BlockSpec((tm, D), blk),
            pl.BlockSpec((None, None, N_MOD, D), modmap),
            pl.BlockSpec((CONV_HALO, D), prev),
            pl.BlockSpec((tm, D), blk),
            pl.BlockSpec((CONV_HALO, D), nxt),
            pl.BlockSpec((None, CONV_W, D), lambda i: (o, 0, 0)),
            pl.BlockSpec((None, 1, D), lambda i: (o, 0, 0)),
            pl.BlockSpec((None, 1, D), lambda i: (o, 0, 0)),
            pl.BlockSpec((None, 1, D), lambda i: (o, 0, 0)),
            pl.BlockSpec((None, D, D), lambda i: (o, 0, 0)),
            pl.BlockSpec((None, 1, D), lambda i: (o, 0, 0)),
        ],
        out_specs=pl.BlockSpec((tm, D), blk),
        out_shape=jax.ShapeDtypeStruct(xs.shape, F32),
        scratch_shapes=[pltpu.VMEM((tm + 2 * CONV_HALO, D), F32), pltpu.VMEM((tm, D), F32),
                        pltpu.VMEM((SUBLANE, tm + 2 * CONV_HALO - SUBLANE, LANE), F32)],
        input_output_aliases={0: 0},
        compiler_params=_cparams(("parallel",)),
        name="conf_out",
    )(xs, mods, u, u, u, w_dw, vec(b_dw), vec(ln_g), vec(ln_b), w_pw2, vec(b_pw2))


def _cast_pad_kernel(w_ref, o_ref, *, n, axis):
    v = w_ref[...].astype(BF16)
    if axis == 0:
        o_ref[:n, :] = v
        o_ref[n:, :] = jnp.zeros((o_ref.shape[0] - n, o_ref.shape[1]), BF16)
    else:
        o_ref[:, :n] = v
        o_ref[:, n:] = jnp.zeros((o_ref.shape[0], o_ref.shape[1] - n), BF16)


def _cast_pad(w, layer, axis, padded):
    _, r, c = w.shape
    strip = 256
    if axis == 0:
        in_blk, out_blk, out_shape, n = (None, r, strip), (padded, strip), (padded, c), r
        imap, omap = (lambda j: (layer, 0, j)), (lambda j: (0, j))
        steps = c // strip
    else:
        in_blk, out_blk, out_shape, n = (None, strip, c), (strip, padded), (r, padded), c
        imap, omap = (lambda j: (layer, j, 0)), (lambda j: (j, 0))
        steps = r // strip
    return pl.pallas_call(
        functools.partial(_cast_pad_kernel, n=n, axis=axis),
        grid=(steps,),
        in_specs=[pl.BlockSpec(in_blk, imap)],
        out_specs=pl.BlockSpec(out_blk, omap),
        out_shape=jax.ShapeDtypeStruct(out_shape, BF16),
        compiler_params=_cparams(("parallel",)),
        name="cast_pad",
    )(w)


def kernel(x, c, ctx, c_ctx, ada_w, ada_b, norm_g, ff1_w1, ff1_w3, ff1_w2, ff2_w1, ff2_w3, ff2_w2,
           mix_w_in, mix_w_out, na_rpb, hy_short_w, hy_short_b, hy_w0, hy_b0, hy_w1, hy_b1,
           hy_w2, hy_b2, hy_w3, hy_freq, hy_bias, cv_w_pw1, cv_b_pw1, cv_w_dw, cv_b_dw,
           cv_ln_g, cv_ln_b, cv_w_pw2, cv_b_pw2, final_g):
    xs = jnp.concatenate([x.reshape(T_LAT, D), ctx.reshape(T_CTX, D),
                          jnp.zeros((T_PAD - T_ALL, D), F32)], axis=0)
    cond = jnp.concatenate([c, c_ctx[None, :], jnp.zeros((8 - NB - 1, D), F32)], axis=0)
    mods = _ada(cond, ada_w, ada_b)[:, :NB + 1].reshape(DEPTH, NB + 1, N_MOD, D)

    ff1 = (ff1_w1, ff1_w3, ff1_w2)
    ff2 = (ff2_w1, ff2_w3, ff2_w2)
    wts = (_cast_pad(ff1_w1, 0, 1, DFF_PAD), _cast_pad(ff1_w3, 0, 1, DFF_PAD),
           _cast_pad(ff1_w2, 0, 0, DFF_PAD))
    w_in = mix_w_in.astype(BF16)
    w_out = mix_w_out.astype(BF16)
    w_pw1 = cv_w_pw1.astype(BF16)
    w_pw2 = cv_w_pw2.astype(BF16)

    mats_lat = _dft_mats(S)
    mats_ctx = _dft_mats(LC)

    out = None
    for i in range(DEPTH):
        ctx_full = i < LAST_EVEN
        xs, wts = _ffn(xs, mods, norm_g[i, 0], *wts, i, 0, i <= LAST_EVEN, cast_next=(*ff2, i))
        if i % 2 == 0:
            e = i // 2
            hyp = (hy_short_w[e], hy_short_b[e], hy_w0[e], hy_b0[e], hy_w1[e], hy_b1[e],
                   hy_w2[e], hy_b2[e], hy_w3[e], hy_freq[e], hy_bias[e])
            qkv, z = _inproj(xs, mods, norm_g[i, 1], w_in, i, e, i <= LAST_EVEN)
            att = _natten(qkv, na_rpb[e])
            hy = _hyena(z, hyp, mats_lat, S, 0)
            if ctx_full:
                att = _ctx_attn(qkv, att)
                hy = _hyena(z, hyp, mats_ctx, LC, T_LAT, hy)
            xs = _outproj(xs, mods, att, hy, w_out, i, e, NBLK_ALL if ctx_full else NBLK_LAT)
        else:
            o = i // 2
            u = _pw1(xs, mods, norm_g[i, 1], w_pw1, cv_b_pw1, i, o, ctx_full)
            cv = (cv_w_dw, cv_b_dw, cv_ln_g, cv_ln_b, w_pw2, cv_b_pw2)
            xs = _conf_out(xs, mods, u, *cv, i, o, TM, 0, S, NBLK_LAT)
            if ctx_full:
                xs = _conf_out(xs, mods, u, *cv, i, o, LC, T_LAT, LC, T_CTX // LC)
        if i == DEPTH - 1:
            out = _ffn(xs, mods, norm_g[i, 2], *wts, i, 6, ctx_full, final_g=final_g)
        else:
            xs, wts = _ffn(xs, mods, norm_g[i, 2], *wts, i, 6, ctx_full, cast_next=(*ff1, i + 1))
    return out.reshape(NB, S, D)
```

```python
import functools
import math

import numpy as np
import jax
import jax.numpy as jnp
from jax import lax
from jax.experimental import pallas as pl
from jax.experimental.pallas import tpu as pltpu

F32 = jnp.float32
BF16 = jnp.bfloat16

D = 2048
NB = 2
S = 4096
DEPTH = 4
GRID_W = 64
GRID_H = S // GRID_W
LC = 256
HD = 128
NH = 8
WA = NH * HD
WB = D - WA
WIN_R = 8
WIN_C = 16
DFF = 5504
HY_EMB = 33
HY_ORDER = 64
CONV_W = 31
N_MOD = 9
RMS_EPS = 1e-6
LN_EPS = 1e-5
LAST_EVEN = (DEPTH - 1) - ((DEPTH - 1) % 2)

T_LAT = NB * S
T_CTX = NB * LC
T_ALL = T_LAT + T_CTX
T_PAD = 9216

LANE = 128
SUBLANE = 8
TM = 512
TM_FFN = 1024
TM_PROJ = 1024
PROJ_ROWS = 512
FFN_ROWS = 512
CAST_COL_SLABS = 64
CAST_ROW = 64
TF = 512
DFF_PAD = TF * (-(-DFF // TF))
CAST_ROW_SLABS = DFF_PAD // CAST_ROW
NBLK_LAT = T_LAT // TM
NBLK_ALL = T_ALL // TM
NEG = -1e30
VMEM_LIMIT = 60 * 1024 * 1024


def _cparams(sem, vmem=VMEM_LIMIT):
    return pltpu.CompilerParams(dimension_semantics=sem, vmem_limit_bytes=vmem)


def _mod_index(i, tm):
    return jnp.minimum(i // (S // tm), NB)


def _stream_blocks(with_ctx, tm):
    return pl.cdiv(T_ALL if with_ctx else T_LAT, tm)


def _norm_mod(x, g, shift, scale):
    ms = jnp.mean(x * x, axis=-1, keepdims=True)
    return (x * lax.rsqrt(ms + RMS_EPS)) * g * (1.0 + scale) + shift


def _ada_kernel(s_ref, w_ref, b_ref, o_ref):
    s = s_ref[...]
    s = (s * jax.nn.sigmoid(s)).astype(BF16)
    o_ref[...] = jnp.dot(s, w_ref[...].astype(BF16), preferred_element_type=F32) + b_ref[...]


def _ada(cond, ada_w, ada_b):
    tn = 1024
    n = N_MOD * D
    return pl.pallas_call(
        _ada_kernel,
        grid=(DEPTH, n // tn),
        in_specs=[
            pl.BlockSpec((8, D), lambda l, j: (0, 0)),
            pl.BlockSpec((None, D, tn), lambda l, j: (l, 0, j)),
            pl.BlockSpec((None, 1, tn), lambda l, j: (l, 0, j)),
        ],
        out_specs=pl.BlockSpec((None, 8, tn), lambda l, j: (l, 0, j)),
        out_shape=jax.ShapeDtypeStruct((DEPTH, 8, n), F32),
        compiler_params=_cparams(("parallel", "parallel")),
        name="ada_mod",
    )(cond, ada_w, ada_b.reshape(DEPTH, 1, n))


def _ffn_kernel(x_ref, mod_ref, g_ref, w1_ref, w3_ref, w2_ref, *rest, jbase, nf, final, full_blocks,
                tail_rows, casting, n_plain):
    rest = list(rest)
    fg_ref = rest.pop(0) if final else None
    cast_in = [rest.pop(0) for _ in range(3)] if casting else []
    plain_in = [rest.pop(0) for _ in range(n_plain)]
    o_ref = rest.pop(0)
    cast_out = [rest.pop(0) for _ in range(3)] if casting else []
    plain_out = [rest.pop(0) for _ in range(n_plain)]
    (h_scr,) = rest
    i = pl.program_id(0)
    f = pl.program_id(1)
    tm = h_scr.shape[0]
    s = i * nf + f

    if n_plain:
        @pl.when(s < CAST_COL_SLABS)
        def _():
            for src, dst in zip(plain_in, plain_out):
                dst[...] = src[...].astype(BF16)

    if casting:
        @pl.when(s < CAST_COL_SLABS)
        def _():
            for src, dst in zip(cast_in[:2], cast_out[:2]):
                dst[:, :DFF] = src[...].astype(BF16)
                dst[:, DFF:] = jnp.zeros((dst.shape[0], DFF_PAD - DFF), BF16)

        @pl.when(s < DFF // CAST_ROW)
        def _():
            cast_out[2][...] = cast_in[2][...].astype(BF16)

        @pl.when((s >= DFF // CAST_ROW) & (s < CAST_ROW_SLABS))
        def _():
            cast_out[2][...] = jnp.zeros(cast_out[2].shape, BF16)

    def norm_rows(r, n):
        rows = slice(r, r + n)
        h = _norm_mod(x_ref[rows, :], g_ref[...], mod_ref[jbase:jbase + 1, :],
                      mod_ref[jbase + 1:jbase + 2, :])
        h_scr[rows, :] = h.astype(BF16)

    def chain(r, n, first):
        rows = slice(r, r + n)
        h = h_scr[rows, :]
        a = jnp.dot(h, w1_ref[...], preferred_element_type=F32)
        b = jnp.dot(h, w3_ref[...], preferred_element_type=F32)
        gate = (a * jax.nn.sigmoid(a) * b).astype(BF16)
        part = jnp.dot(gate, w2_ref[...], preferred_element_type=F32)
        if first:
            o_ref[rows, :] = part
        else:
            o_ref[rows, :] += part

    def epilogue(r, n):
        rows = slice(r, r + n)
        y = x_ref[rows, :] + (0.5 * mod_ref[jbase + 2:jbase + 3, :]) * o_ref[rows, :]
        if final:
            ms = jnp.mean(y * y, axis=-1, keepdims=True)
            y = (y * lax.rsqrt(ms + RMS_EPS)) * fg_ref[...]
        o_ref[rows, :] = y

    def step(live):
        chains = [(r, FFN_ROWS) for r in range(0, live, FFN_ROWS)]

        @pl.when(f == 0)
        def _():
            for r, n in chains:
                norm_rows(r, n)
                chain(r, n, True)
            if live < tm:
                o_ref[live:, :] = jnp.zeros((tm - live, o_ref.shape[1]), F32)

        @pl.when((f > 0) & (f < nf - 1))
        def _():
            for r, n in chains:
                chain(r, n, False)

        @pl.when(f == nf - 1)
        def _():
            for r, n in chains:
                chain(r, n, False)
                epilogue(r, n)

    if tail_rows == tm:
        step(tm)
    else:
        @pl.when(i < full_blocks)
        def _():
            step(tm)

        @pl.when(i >= full_blocks)
        def _():
            step(tail_rows)


def _ffn(xs, mods, g, w1, w3, w2, layer, jbase, with_ctx, final_g=None, cast_next=None, cast_plain=()):
    nf = DFF_PAD // TF
    final = final_g is not None
    tm = TM_FFN
    nblk = _stream_blocks(with_ctx, tm)
    in_specs = [
        pl.BlockSpec((tm, D), lambda i, f: (i, 0)),
        pl.BlockSpec((None, None, N_MOD, D),
                     lambda i, f: (layer, _mod_index(i, tm), 0, 0)),
        pl.BlockSpec((1, D), lambda i, f: (0, 0)),
        pl.BlockSpec((D, TF), lambda i, f: (0, f)),
        pl.BlockSpec((D, TF), lambda i, f: (0, f)),
        pl.BlockSpec((TF, D), lambda i, f: (f, 0)),
    ]
    args = [xs, mods, g.reshape(1, D), w1, w3, w2]
    out_specs = [pl.BlockSpec((tm, D), lambda i, f: (i, 0))]
    if final:
        in_specs.append(pl.BlockSpec((1, D), lambda i, f: (0, 0)))
        args.append(final_g.reshape(1, D))
        out_shape = [jax.ShapeDtypeStruct((nblk * tm, D), F32)]
        aliases = {}
    else:
        out_shape = [jax.ShapeDtypeStruct(xs.shape, F32)]
        aliases = {0: 0}
    if cast_next is not None:
        c1, c3, c2, nlayer = cast_next
        assert nblk * nf >= max(CAST_COL_SLABS, CAST_ROW_SLABS)
        step = lambda i, f: i * nf + f
        col_in = pl.BlockSpec((None, D // CAST_COL_SLABS, DFF),
                              lambda i, f: (nlayer, jnp.minimum(step(i, f), CAST_COL_SLABS - 1), 0))
        col_out = pl.BlockSpec((D // CAST_COL_SLABS, DFF_PAD),
                               lambda i, f: (jnp.minimum(step(i, f), CAST_COL_SLABS - 1), 0))
        row_in = pl.BlockSpec((None, CAST_ROW, D),
                              lambda i, f: (nlayer, jnp.minimum(step(i, f), DFF // CAST_ROW - 1), 0))
        row_out = pl.BlockSpec((CAST_ROW, D),
                               lambda i, f: (jnp.minimum(step(i, f), CAST_ROW_SLABS - 1), 0))
        in_specs += [col_in, col_in, row_in]
        args += [c1, c3, c2]
        out_specs += [col_out, col_out, row_out]
        out_shape += [jax.ShapeDtypeStruct((D, DFF_PAD), BF16), jax.ShapeDtypeStruct((D, DFF_PAD), BF16),
                      jax.ShapeDtypeStruct((DFF_PAD, D), BF16)]
    for src, slayer in cast_plain:
        _, r, c = src.shape
        slab = r // CAST_COL_SLABS
        in_specs.insert(len(args), pl.BlockSpec(
            (None, slab, c), lambda i, f, slayer=slayer: (slayer, jnp.minimum(i * nf + f, CAST_COL_SLABS - 1), 0)))
        args.append(src)
        out_specs.append(pl.BlockSpec((slab, c), lambda i, f: (jnp.minimum(i * nf + f, CAST_COL_SLABS - 1), 0)))
        out_shape.append(jax.ShapeDtypeStruct((r, c), BF16))
    res = pl.pallas_call(
        functools.partial(_ffn_kernel, jbase=jbase, nf=nf, final=final, full_blocks=T_LAT // tm,
                          tail_rows=T_ALL - T_LAT if with_ctx else tm, casting=cast_next is not None,
                          n_plain=len(cast_plain)),
        grid=(nblk, nf),
        in_specs=in_specs,
        out_specs=out_specs,
        out_shape=out_shape,
        scratch_shapes=[pltpu.VMEM((tm, D), BF16)],
        input_output_aliases=aliases,
        compiler_params=_cparams(("arbitrary", "arbitrary")),
        name="ffn",
    )(*args)
    if cast_next is None and not cast_plain:
        return res[0]
    n_next = 3 if cast_next is not None else 0
    return res[0], tuple(res[1:1 + n_next]), tuple(res[1 + n_next:])


def _inproj_kernel(x_ref, mod_ref, g_ref, w_ref, qkv_ref, z_ref, h_scr, *, nq):
    j = pl.program_id(1)
    starts = range(0, h_scr.shape[0], PROJ_ROWS)

    def norm_rows(r):
        rows = slice(r, r + PROJ_ROWS)
        h = _norm_mod(x_ref[rows, :], g_ref[...], mod_ref[3:4, :], mod_ref[4:5, :])
        h_scr[rows, :] = h.astype(BF16)

    def proj(r):
        return jnp.dot(h_scr[r:r + PROJ_ROWS, :], w_ref[...], preferred_element_type=F32)

    @pl.when(j == 0)
    def _():
        for r in starts:
            norm_rows(r)
            qkv_ref[r:r + PROJ_ROWS, :] = (proj(r) * (HD ** -0.5)).astype(BF16)

    @pl.when((j > 0) & (j < nq))
    def _():
        for r in starts:
            qkv_ref[r:r + PROJ_ROWS, :] = proj(r).astype(BF16)

    @pl.when(j >= nq)
    def _():
        for r in starts:
            z_ref[r:r + PROJ_ROWS, :] = proj(r)


def _inproj(xs, mods, g, w_in, layer, e, with_ctx):
    tn = WA
    nq = 3 * WA // tn
    nz = 3 * WB // tn
    tm = TM_PROJ
    return pl.pallas_call(
        functools.partial(_inproj_kernel, nq=nq),
        grid=(_stream_blocks(with_ctx, tm), nq + nz),
        in_specs=[
            pl.BlockSpec((tm, D), lambda i, j: (i, 0)),
            pl.BlockSpec((None, None, N_MOD, D), lambda i, j: (layer, _mod_index(i, tm), 0, 0)),
            pl.BlockSpec((1, D), lambda i, j: (0, 0)),
            pl.BlockSpec((D, tn), lambda i, j: (0, j)),
        ],
        out_specs=[
            pl.BlockSpec((tm, tn), lambda i, j: (i, jnp.minimum(j, nq - 1))),
            pl.BlockSpec((tm, tn), lambda i, j: (i, jnp.maximum(j - nq, 0))),
        ],
        out_shape=[
            jax.ShapeDtypeStruct((T_PAD, 3 * WA), BF16),
            jax.ShapeDtypeStruct((T_PAD, 3 * WB), F32),
        ],
        scratch_shapes=[pltpu.VMEM((tm, D), BF16)],
        compiler_params=_cparams(("parallel", "arbitrary")),
        name="mix_inproj",
    )(xs, mods, g.reshape(1, D), w_in)


def _pw1_kernel(x_ref, mod_ref, g_ref, wa_ref, wg_ref, ba_ref, bg_ref, u_ref, h_scr):
    j = pl.program_id(1)
    starts = range(0, h_scr.shape[0], PROJ_ROWS)

    def norm_rows(r):
        rows = slice(r, r + PROJ_ROWS)
        h = _norm_mod(x_ref[rows, :], g_ref[...], mod_ref[3:4, :], mod_ref[4:5, :])
        h_scr[rows, :] = h.astype(BF16)

    def glu(r):
        rows = slice(r, r + PROJ_ROWS)
        h = h_scr[rows, :]
        a = jnp.dot(h, wa_ref[...], preferred_element_type=F32) + ba_ref[...]
        gt = jnp.dot(h, wg_ref[...], preferred_element_type=F32) + bg_ref[...]
        u_ref[rows, :] = a * jax.nn.sigmoid(gt)

    @pl.when(j == 0)
    def _():
        for r in starts:
            norm_rows(r)
            glu(r)

    @pl.when(j > 0)
    def _():
        for r in starts:
            glu(r)


def _pw1(xs, mods, g, w_pw1, b_pw1, layer, o, with_ctx):
    tn = 512
    nj = D // tn
    tm = TM_PROJ
    b3 = b_pw1.reshape(-1, 1, 2 * D)
    return pl.pallas_call(
        _pw1_kernel,
        grid=(_stream_blocks(with_ctx, tm), nj),
        in_specs=[
            pl.BlockSpec((tm, D), lambda i, j: (i, 0)),
            pl.BlockSpec((None, None, N_MOD, D), lambda i, j: (layer, _mod_index(i, tm), 0, 0)),
            pl.BlockSpec((1, D), lambda i, j: (0, 0)),
            pl.BlockSpec((D, tn), lambda i, j: (0, j)),
            pl.BlockSpec((D, tn), lambda i, j: (0, j + nj)),
            pl.BlockSpec((None, 1, tn), lambda i, j: (o, 0, j)),
            pl.BlockSpec((None, 1, tn), lambda i, j: (o, 0, j + nj)),
        ],
        out_specs=pl.BlockSpec((tm, tn), lambda i, j: (i, j)),
        out_shape=jax.ShapeDtypeStruct((T_PAD, D), F32),
        scratch_shapes=[pltpu.VMEM((tm, D), BF16)],
        compiler_params=_cparams(("parallel", "arbitrary")),
        name="conf_pw1",
    )(xs, mods, g.reshape(1, D), w_pw1, w_pw1, b3, b3)


NAT_G = 8
NAT_KR = {1: 8, 2: 10, 4: 12, 8: 16}[NAT_G]


def _natten_geometry():
    kstart, types, type_id = [], [], []
    gi = np.arange(NAT_G)[:, None]
    kj = np.arange(NAT_KR)[None, :]
    for g in range(GRID_H // NAT_G):
        r0 = g * NAT_G
        ks = min(max(r0 - WIN_R // 2, 0), GRID_H - NAT_KR)
        qrow, krow = r0 + gi, ks + kj
        rs = np.clip(qrow - WIN_R // 2, 0, GRID_H - WIN_R)
        valid = (krow >= rs) & (krow < rs + WIN_R)
        assert (valid.sum(axis=1) == WIN_R).all()
        dr = np.where(valid, krow - qrow + WIN_R - 1, 2 * WIN_R - 1).astype(np.int32)
        for t, d0 in enumerate(types):
            if (d0 == dr).all():
                type_id.append(t)
                break
        else:
            type_id.append(len(types))
            types.append(dr)
        kstart.append(ks)
    return np.asarray(kstart, np.int32), np.asarray(type_id, np.int32), np.stack(types)


def _natten_bias(rpb):
    _, _, dr = _natten_geometry()
    ntypes = dr.shape[0]
    c = np.arange(GRID_W)[:, None]
    kc = np.arange(GRID_W)[None, :]
    cs = np.clip(c - WIN_C // 2, 0, GRID_W - WIN_C)
    cvalid = (kc >= cs) & (kc < cs + WIN_C)
    dc = kc - c + WIN_C - 1
    onehot = ((np.arange(2 * WIN_C - 1)[:, None, None] == dc[None]) & cvalid[None]).astype(np.float32)
    colmask = np.where(cvalid, 0.0, NEG).astype(np.float32)
    bcols = jnp.einsum("hrd,dck->hrck", rpb, jnp.asarray(onehot), precision=lax.Precision.HIGHEST)
    bcols = (bcols + jnp.asarray(colmask)).transpose(0, 2, 1, 3)
    slabs = []
    for ty in range(ntypes):
        for gi in range(NAT_G):
            kj = np.nonzero(dr[ty, gi] != 2 * WIN_R - 1)[0]
            a, r0 = int(kj[0]), int(dr[ty, gi, kj[0]])
            assert (kj == a + np.arange(WIN_R)).all() and (dr[ty, gi, kj] == r0 + np.arange(WIN_R)).all()
            s = bcols[:, :, r0:r0 + WIN_R, :].reshape(NH, GRID_W, WIN_R * GRID_W)
            slabs.append(jnp.pad(s, ((0, 0), (0, 0), (a * GRID_W, (NAT_KR - a - WIN_R) * GRID_W)),
                                 constant_values=NEG))
    return jnp.stack(slabs, axis=1).reshape(NH, ntypes, NAT_G * GRID_W, NAT_KR * GRID_W)


def _natten_kernel(ks_ref, ty_ref, q_ref, k_ref, v_ref, kc_ref, vc_ref, bias_ref, o_ref):
    gq, kk = NAT_G * GRID_W, NAT_KR * GRID_W
    kc = kc_ref[...]
    vc = vc_ref[...]
    nt = (((1,), (1,)), ((), ()))

    def body(g, carry):
        q0 = pl.multiple_of(g * gq, gq)
        k0 = pl.multiple_of(ks_ref[g] * GRID_W, GRID_W)
        q = q_ref[pl.ds(q0, gq), :]
        k = k_ref[pl.ds(k0, kk), :]
        v = v_ref[pl.ds(k0, kk), :]
        s_loc = lax.dot_general(q, k, nt, preferred_element_type=F32) + bias_ref[ty_ref[g]]
        s_ctx = lax.dot_general(q, kc, nt, preferred_element_type=F32)
        m = jnp.maximum(jnp.max(s_loc, axis=-1, keepdims=True),
                        jnp.max(s_ctx, axis=-1, keepdims=True))
        p_loc = jnp.exp(s_loc - m)
        p_ctx = jnp.exp(s_ctx - m)
        den = jnp.sum(p_loc, axis=-1, keepdims=True) + jnp.sum(p_ctx, axis=-1, keepdims=True)
        o = (jnp.dot(p_loc.astype(BF16), v, preferred_element_type=F32)
             + jnp.dot(p_ctx.astype(BF16), vc, preferred_element_type=F32))
        o_ref[pl.ds(q0, gq), :] = (o / den).astype(BF16)
        return carry

    lax.fori_loop(0, GRID_H // NAT_G, body, 0, unroll=True)


def _natten(qkv, rpb):
    kstart, type_id, _ = _natten_geometry()
    bias = _natten_bias(rpb)
    _, ntypes, gq, kk = bias.shape
    cb = T_LAT // LC
    grid_spec = pltpu.PrefetchScalarGridSpec(
        num_scalar_prefetch=2,
        grid=(NB, NH),
        in_specs=[
            pl.BlockSpec((S, HD), lambda b, h, *_: (b, h)),
            pl.BlockSpec((S, HD), lambda b, h, *_: (b, NH + h)),
            pl.BlockSpec((S, HD), lambda b, h, *_: (b, 2 * NH + h)),
            pl.BlockSpec((LC, HD), lambda b, h, *_: (cb + b, NH + h)),
            pl.BlockSpec((LC, HD), lambda b, h, *_: (cb + b, 2 * NH + h)),
            pl.BlockSpec((None, ntypes, gq, kk), lambda b, h, *_: (h, 0, 0, 0)),
        ],
        out_specs=pl.BlockSpec((S, HD), lambda b, h, *_: (b, h)),
    )
    return pl.pallas_call(
        _natten_kernel,
        grid_spec=grid_spec,
        out_shape=jax.ShapeDtypeStruct((T_ALL, WA), BF16),
        compiler_params=_cparams(("parallel", "parallel")),
        name="natten",
    )(jnp.asarray(kstart), jnp.asarray(type_id), qkv, qkv, qkv, qkv, qkv, bias)


def _ctx_attn_kernel(q_ref, k_ref, v_ref, att_hbm, o_ref):
    del att_hbm
    s = lax.dot_general(q_ref[...], k_ref[...], (((1,), (1,)), ((), ())),
                        preferred_element_type=F32)
    m = jnp.max(s, axis=-1, keepdims=True)
    p = jnp.exp(s - m)
    den = jnp.sum(p, axis=-1, keepdims=True)
    o = jnp.dot(p.astype(BF16), v_ref[...], preferred_element_type=F32)
    o_ref[...] = (o / den).astype(BF16)


def _ctx_attn(qkv, att):
    cb = T_LAT // LC
    return pl.pallas_call(
        _ctx_attn_kernel,
        grid=(NB, NH),
        in_specs=[
            pl.BlockSpec((LC, HD), lambda b, h: (cb + b, h)),
            pl.BlockSpec((LC, HD), lambda b, h: (cb + b, NH + h)),
            pl.BlockSpec((LC, HD), lambda b, h: (cb + b, 2 * NH + h)),
            pl.BlockSpec(memory_space=pl.ANY),
        ],
        out_specs=pl.BlockSpec((LC, HD), lambda b, h: (cb + b, h)),
        out_shape=jax.ShapeDtypeStruct(att.shape, att.dtype),
        input_output_aliases={3: 0},
        compiler_params=_cparams(("parallel", "parallel")),
        name="ctx_attn",
    )(qkv, qkv, qkv, att)


def _dft_tables(length, parity):
    n2 = 4 * length
    half = length // 2
    r = jnp.arange(half, dtype=jnp.int32)[:, None]
    hi = jnp.arange(half // LANE, dtype=jnp.int32)[None, :]
    lo = jnp.arange(LANE, dtype=jnp.int32)[None, :]

    def cs(phase):
        ang = (phase % n2).astype(F32) * (2.0 * math.pi / n2)
        return jnp.cos(ang), jnp.sin(ang)

    km = cs((2 * r + 1) * (2 * LANE * hi)) + cs((2 * r + 1) * (2 * lo + parity))
    mk = cs((2 * r + parity) * (2 * LANE * hi)) + cs((2 * r + parity) * (2 * lo + 1))
    return km, mk


def _dftgen_kernel(ca_ref, sa_ref, cb_ref, sb_ref, c_ref, s_ref, *, nt, sgn):
    cb = cb_ref[...]
    sb = sb_ref[...]
    for t1 in range(nt):
        ca = ca_ref[:, t1:t1 + 1]
        sa = sa_ref[:, t1:t1 + 1]
        sl = slice(t1 * LANE, (t1 + 1) * LANE)
        c_ref[:, sl] = (ca * cb - sa * sb).astype(BF16)
        s_ref[:, sl] = (sgn * (sa * cb + ca * sb)).astype(BF16)


def _dftgen(tables, length, sgn):
    tr = min(256, length)
    nt = length // LANE
    row = lambda i: (i, 0)
    return pl.pallas_call(
        functools.partial(_dftgen_kernel, nt=nt, sgn=sgn),
        grid=(length // tr,),
        in_specs=[pl.BlockSpec((tr, nt), row), pl.BlockSpec((tr, nt), row),
                  pl.BlockSpec((tr, LANE), row), pl.BlockSpec((tr, LANE), row)],
        out_specs=[pl.BlockSpec((tr, length), row), pl.BlockSpec((tr, length), row)],
        out_shape=[jax.ShapeDtypeStruct((length, length), BF16)] * 2,
        compiler_params=_cparams(("parallel",)),
        name="dft_gen",
    )(*tables)


def _hy_pre_kernel(z0_ref, z1_ref, z2_ref, w_ref, b_ref, u16_ref, u32_ref, x0_ref):
    half = z0_ref.shape[0] // 2

    def conv(z_ref, part):
        ze = z_ref[pl.ds(0, half, stride=2), :]
        zo = z_ref[pl.ds(1, half, stride=2), :]
        row = lax.broadcasted_iota(jnp.int32, ze.shape, 0)
        zo_prev = jnp.where(row == 0, 0.0, pltpu.roll(zo, 1, 0))
        ze_next = jnp.where(row == half - 1, 0.0, pltpu.roll(ze, half - 1, 0))
        w = w_ref[part]
        bias = b_ref[part]
        even = zo_prev * w[0:1, :] + ze * w[1:2, :] + zo * w[2:3, :] + bias
        odd = ze * w[0:1, :] + zo * w[1:2, :] + ze_next * w[2:3, :] + bias
        return even, odd

    x0 = conv(z0_ref, 0)
    x1 = conv(z1_ref, 1)
    v = conv(z2_ref, 2)
    for p in range(2):
        u = v[p] * x1[p]
        x0_ref[p] = x0[p]
        u32_ref[p] = u
        u16_ref[p] = u.astype(BF16)


def _hy_pre(z, short_w, short_b, length, row_off):
    tc = LANE
    nc = WB // tc
    rb = row_off // length
    half = length // 2
    w = short_w.reshape(3, 3, WB).transpose(1, 0, 2)
    bb = short_b.reshape(3, 1, WB)
    out_spec = pl.BlockSpec((2, half, tc), lambda b, c: (0, 0, b * nc + c))
    return pl.pallas_call(
        _hy_pre_kernel,
        grid=(NB, nc),
        in_specs=[
            pl.BlockSpec((length, tc), lambda b, c: (rb + b, c)),
            pl.BlockSpec((length, tc), lambda b, c: (rb + b, nc + c)),
            pl.BlockSpec((length, tc), lambda b, c: (rb + b, 2 * nc + c)),
            pl.BlockSpec((3, 3, tc), lambda b, c: (0, 0, c)),
            pl.BlockSpec((3, 1, tc), lambda b, c: (0, 0, c)),
        ],
        out_specs=[out_spec, out_spec, out_spec],
        out_shape=[jax.ShapeDtypeStruct((2, half, NB * WB), BF16),
                   jax.ShapeDtypeStruct((2, half, NB * WB), F32),
                   jax.ShapeDtypeStruct((2, half, NB * WB), F32)],
        compiler_params=_cparams(("parallel", "parallel")),
        name="hyena_pre",
    )(z, z, z, w, bb)


def _hy_filter_kernel(z_ref, w0_ref, b0_ref, w1_ref, b1_ref, w2_ref, b2_ref, w3_ref, fr_ref,
                      dl_ref, fs_ref, fd_ref):
    hp = lax.Precision.HIGHEST
    z = z_ref[...]
    fr = fr_ref[...]
    h = jnp.sin(fr * (jnp.dot(z, w0_ref[...], precision=hp, preferred_element_type=F32) + b0_ref[...]))
    h = jnp.sin(fr * (jnp.dot(h, w1_ref[...], precision=hp, preferred_element_type=F32) + b1_ref[...]))
    h = jnp.sin(fr * (jnp.dot(h, w2_ref[...], precision=hp, preferred_element_type=F32) + b2_ref[...]))
    hh = jnp.dot(h.astype(BF16), w3_ref[...].astype(BF16), preferred_element_type=F32)
    win = jnp.exp(-z[:, 0:1] * dl_ref[...])
    fwd = hh[:, :WB] * win
    bwd = hh[:, WB:] * win
    row = lax.broadcasted_iota(jnp.int32, bwd.shape, 0) + pl.program_id(0) * z.shape[0]
    bwd = jnp.where(row == 0, 0.0, bwd)
    fs_ref[...] = (fwd + bwd).astype(BF16)
    fd_ref[...] = (bwd - fwd).astype(BF16)


def _hy_filter(length, w0, b0, w1, b1, w2, b2, w3, freq):
    t = jnp.linspace(0.0, 1.0, length, dtype=F32)[:, None]
    bands = (HY_EMB - 1) // 2
    f = jnp.linspace(1e-4, bands - 1, bands, dtype=F32)
    w = 2 * math.pi * jnp.arange(length, dtype=F32)[:, None] / length
    z = jnp.concatenate([t, jnp.cos(f * w), -jnp.sin(f * w)], axis=-1)
    z = jnp.concatenate([z[0::2], z[1::2]], axis=0)
    emb = HY_ORDER
    z = jnp.pad(z, ((0, 0), (0, emb - HY_EMB)))
    w0p = jnp.pad(w0, ((0, emb - HY_EMB), (0, 0)))
    max_decay = math.log(1e-2) / 0.3
    min_decay = math.log(1e-2) / 1.5
    deltas = jnp.abs(jnp.linspace(min_decay, max_decay, WB, dtype=F32))[None, :]
    tt = min(256, length)
    full = lambda shape: pl.BlockSpec(shape, lambda i: (0,) * len(shape))
    row = lambda i: (i, 0)
    vec = lambda a: a.reshape(1, -1)
    return pl.pallas_call(
        _hy_filter_kernel,
        grid=(length // tt,),
        in_specs=[pl.BlockSpec((tt, emb), row),
                  full((emb, HY_ORDER)), full((1, HY_ORDER)),
                  full((HY_ORDER, HY_ORDER)), full((1, HY_ORDER)),
                  full((HY_ORDER, HY_ORDER)), full((1, HY_ORDER)),
                  full((HY_ORDER, 2 * WB)), full((1, HY_ORDER)), full((1, WB))],
        out_specs=[pl.BlockSpec((tt, WB), row), pl.BlockSpec((tt, WB), row)],
        out_shape=[jax.ShapeDtypeStruct((length, WB), BF16)] * 2,
        compiler_params=_cparams(("parallel",)),
        name="hyena_filter",
    )(z, w0p, vec(b0), w1, vec(b1), w2, vec(b2), w3, vec(freq), deltas)


def _dft_tiles(length):
    return min(512, length // 2), 512


def _dft_mat_specs(tr, half):
    return [pl.BlockSpec((tr, half), lambda j, i: (i, 0))] * 4


def _dft_spec_kernel(ce_ref, se_ref, co_ref, so_ref, fs_ref, fd_ref, a_ref, b_ref, *, scale):
    ae = jnp.dot(ce_ref[...], fs_ref[0], preferred_element_type=F32)
    ao = jnp.dot(co_ref[...], fs_ref[1], preferred_element_type=F32)
    be = jnp.dot(se_ref[...], fd_ref[0], preferred_element_type=F32)
    bo = jnp.dot(so_ref[...], fd_ref[1], preferred_element_type=F32)
    a_ref[0] = (ae + ao) * scale
    a_ref[1] = (ae - ao) * scale
    b_ref[0] = (be + bo) * scale
    b_ref[1] = (bo - be) * scale


def _dft_spec(mats, fs, fd, length):
    tr, tc = _dft_tiles(length)
    half = length // 2
    return pl.pallas_call(
        functools.partial(_dft_spec_kernel, scale=1.0 / length),
        grid=(WB // tc, half // tr),
        in_specs=_dft_mat_specs(tr, half) + [pl.BlockSpec((2, half, tc), lambda j, i: (0, 0, j))] * 2,
        out_specs=[pl.BlockSpec((2, tr, tc), lambda j, i: (0, i, j))] * 2,
        out_shape=[jax.ShapeDtypeStruct((2, half, WB), F32)] * 2,
        compiler_params=_cparams(("parallel", "parallel")),
        name="hyena_filter_dft",
    )(*mats, fs, fd)


def _dft_fwd_kernel(ce_ref, se_ref, co_ref, so_ref, u_ref, a_ref, b_ref, re_ref, im_ref):
    ue = u_ref[0]
    uo = u_ref[1]
    pe = jnp.dot(ce_ref[...], ue, preferred_element_type=F32)
    po = jnp.dot(co_ref[...], uo, preferred_element_type=F32)
    qe = jnp.dot(se_ref[...], ue, preferred_element_type=F32)
    qo = jnp.dot(so_ref[...], uo, preferred_element_type=F32)
    for h, (p, q) in enumerate(((pe + po, qe + qo), (pe - po, qo - qe))):
        a = a_ref[h]
        b = b_ref[h]
        re_ref[h] = (p * a + q * b).astype(BF16)
        im_ref[h] = (p * b - q * a).astype(BF16)


def _dft_fwd(mats, u16, a, bq, length):
    tr, tc = _dft_tiles(length)
    half = length // 2
    ncj = WB // tc
    return pl.pallas_call(
        _dft_fwd_kernel,
        grid=(NB * ncj, half // tr),
        in_specs=_dft_mat_specs(tr, half) + [
            pl.BlockSpec((2, half, tc), lambda j, i: (0, 0, j)),
            pl.BlockSpec((2, tr, tc), lambda j, i: (0, i, j % ncj)),
            pl.BlockSpec((2, tr, tc), lambda j, i: (0, i, j % ncj))],
        out_specs=[pl.BlockSpec((2, tr, tc), lambda j, i: (0, i, j))] * 2,
        out_shape=[jax.ShapeDtypeStruct((2, half, NB * WB), BF16)] * 2,
        compiler_params=_cparams(("parallel", "parallel")),
        name="hyena_fwd_dft",
    )(*mats, u16, a, bq)


def _dft_inv_kernel(ce_ref, se_ref, co_ref, so_ref, re_ref, im_ref, u_ref, x0_ref, bias_ref, *rest):
    o_ref, sum_scr, y_scr = rest[-3:]
    tr = ce_ref.shape[0]

    @pl.when(pl.program_id(1) == 0)
    def _():
        re0, re1 = re_ref[0].astype(F32), re_ref[1].astype(F32)
        im0, im1 = im_ref[0].astype(F32), im_ref[1].astype(F32)
        sum_scr[0] = (re0 + re1).astype(BF16)
        sum_scr[1] = (im0 - im1).astype(BF16)
        sum_scr[2] = (re0 - re1).astype(BF16)
        sum_scr[3] = (im0 + im1).astype(BF16)

    ye = (jnp.dot(ce_ref[...], sum_scr[0], preferred_element_type=F32)
          + jnp.dot(se_ref[...], sum_scr[1], preferred_element_type=F32))
    yo = (jnp.dot(co_ref[...], sum_scr[2], preferred_element_type=F32)
          + jnp.dot(so_ref[...], sum_scr[3], preferred_element_type=F32))
    bias = bias_ref[...]
    oe = (ye + u_ref[0] * bias) * x0_ref[0]
    oo = (yo + u_ref[1] * bias) * x0_ref[1]
    for c in range(y_scr.shape[0]):
        lanes = slice(c * LANE, (c + 1) * LANE)
        y_scr[c, pl.ds(0, tr, stride=2), :] = oe[:, lanes]
        y_scr[c, pl.ds(1, tr, stride=2), :] = oo[:, lanes]
        o_ref[:, lanes] = y_scr[c].astype(BF16)


def _dft_inv(mats, re, im, u32, x0, bias, length, row_off, hy=None):
    tr, tc = _dft_tiles(length)
    tr = min(tr, 256)
    half = length // 2
    ncj = WB // tc
    nri = half // tr
    rb = row_off // (2 * tr)
    aliased = hy is not None
    in_specs = _dft_mat_specs(tr, half) + [
        pl.BlockSpec((2, half, tc), lambda j, i: (0, 0, j)),
        pl.BlockSpec((2, half, tc), lambda j, i: (0, 0, j)),
        pl.BlockSpec((2, tr, tc), lambda j, i: (0, i, j)),
        pl.BlockSpec((2, tr, tc), lambda j, i: (0, i, j)),
        pl.BlockSpec((1, tc), lambda j, i: (0, j % ncj))]
    args = [*mats, re, im, u32, x0, bias.reshape(1, WB)]
    if aliased:
        in_specs.append(pl.BlockSpec(memory_space=pl.ANY))
        args.append(hy)
    return pl.pallas_call(
        _dft_inv_kernel,
        grid=(NB * ncj, nri),
        in_specs=in_specs,
        out_specs=pl.BlockSpec((2 * tr, tc), lambda j, i: (rb + (j // ncj) * nri + i, j % ncj)),
        out_shape=jax.ShapeDtypeStruct((T_ALL, WB), BF16),
        scratch_shapes=[pltpu.VMEM((4, half, tc), BF16), pltpu.VMEM((tc // LANE, 2 * tr, LANE), F32)],
        input_output_aliases={len(args) - 1: 0} if aliased else {},
        compiler_params=_cparams(("parallel", "arbitrary")),
        name="hyena_inv_dft",
    )(*args)


def _hyena(z, hy_params, mats, length, row_off, hy=None):
    short_w, short_b, w0, b0, w1, b1, w2, b2, w3, freq, bias = hy_params
    fwd_mats, inv_mats = mats
    half = length // 2
    u16, u32, x0 = _hy_pre(z, short_w, short_b, length, row_off)
    fs, fd = _hy_filter(length, w0, b0, w1, b1, w2, b2, w3, freq)
    a, bq = _dft_spec(fwd_mats, fs.reshape(2, half, WB), fd.reshape(2, half, WB), length)
    re, im = _dft_fwd(fwd_mats, u16, a, bq, length)
    return _dft_inv(inv_mats, re, im, u32, x0, bias, length, row_off, hy)


def _dft_mats(length):
    half = length // 2
    km_e, mk_e = _dft_tables(length, 0)
    km_o, mk_o = _dft_tables(length, 1)
    fwd = _dftgen(km_e, half, 1.0) + _dftgen(km_o, half, 1.0)
    inv = _dftgen(mk_e, half, -1.0) + _dftgen(mk_o, half, -1.0)
    return tuple(fwd), tuple(inv)


def _outproj_kernel(x_ref, mod_ref, a_ref, y_ref, wa_ref, wy_ref, o_ref):
    o = (jnp.dot(a_ref[...], wa_ref[...], preferred_element_type=F32)
         + jnp.dot(y_ref[...], wy_ref[...], preferred_element_type=F32))
    o_ref[...] = x_ref[...] + mod_ref[5:6, :] * o


def _outproj(xs, mods, att, hy, w_out, layer, e, nblk):
    return pl.pallas_call(
        _outproj_kernel,
        grid=(nblk,),
        in_specs=[
            pl.BlockSpec((TM, D), lambda i: (i, 0)),
            pl.BlockSpec((None, None, N_MOD, D), lambda i: (layer, _mod_index(i, TM), 0, 0)),
            pl.BlockSpec((TM, WA), lambda i: (i, 0)),
            pl.BlockSpec((TM, WB), lambda i: (i, 0)),
            pl.BlockSpec((WA, D), lambda i: (0, 0)),
            pl.BlockSpec((WB, D), lambda i: (1, 0)),
        ],
        out_specs=pl.BlockSpec((TM, D), lambda i: (i, 0)),
        out_shape=jax.ShapeDtypeStruct(xs.shape, F32),
        input_output_aliases={0: 0},
        compiler_params=_cparams(("parallel",)),
        name="mix_outproj",
    )(xs, mods, att, hy, w_out, w_out)


CONV_HALO = 16
CONV_RC = 64


def _conf_out_kernel(x_ref, mod_ref, up_ref, uc_ref, un_ref, wdw_ref, bdw_ref, lg_ref, lb_ref,
                     w2_ref, b2_ref, o_ref, ext_scr, cv_scr, ph_scr, *, tm, bps):
    i = pl.program_id(0)
    first = (i % bps) == 0
    last = (i % bps) == bps - 1
    halo = CONV_HALO
    zero = jnp.zeros((halo, D), F32)
    ext_scr[0:halo, :] = jnp.where(first, zero, up_ref[...])
    ext_scr[halo:halo + tm, :] = uc_ref[...]
    ext_scr[halo + tm:2 * halo + tm, :] = jnp.where(last, zero, un_ref[...])
    off = halo - CONV_W // 2

    def col_body(cc, carry):
        c0 = pl.multiple_of(cc * LANE, LANE)
        wv = wdw_ref[:, pl.ds(c0, LANE)]
        bv = bdw_ref[:, pl.ds(c0, LANE)]

        span = ph_scr.shape[1]
        for p in range(1, SUBLANE):
            ph_scr[p] = ext_scr[pl.ds(p, span), pl.ds(c0, LANE)]
        for r0 in range(0, tm, CONV_RC):
            acc = jnp.zeros((CONV_RC, LANE), F32)
            for j in range(CONV_W):
                p = (off + j) % SUBLANE
                base = off + j - p + r0
                if p == 0:
                    rows = ext_scr[pl.ds(base, CONV_RC), pl.ds(c0, LANE)]
                else:
                    rows = ph_scr[p, pl.ds(base, CONV_RC), :]
                acc = acc + rows * wv[j:j + 1, :]
            cv_scr[pl.ds(r0, CONV_RC), pl.ds(c0, LANE)] = acc + bv
        return carry

    lax.fori_loop(0, D // LANE, col_body, 0)

    v = cv_scr[...]
    mu = jnp.mean(v, axis=-1, keepdims=True)
    vc = v - mu
    var = jnp.mean(vc * vc, axis=-1, keepdims=True)
    t = vc * lax.rsqrt(var + LN_EPS) * lg_ref[...] + lb_ref[...]
    t = (t * jax.nn.sigmoid(t)).astype(BF16)
    o = jnp.dot(t, w2_ref[...], preferred_element_type=F32) + b2_ref[...]
    o_ref[...] = x_ref[...] + mod_ref[5:6, :] * o


def _conf_out(xs, mods, u, w_dw, b_dw, ln_g, ln_b, w_pw2, b_pw2, layer, o, tm, row_off, seq, nblk):
    bps = seq // tm
    rb = row_off // tm
    vec = lambda a: a.reshape(-1, 1, D)
    blk = lambda i: (rb + i, 0)
    hpb = tm // CONV_HALO
    prev = lambda i: (jnp.maximum((rb + i) * hpb - 1, 0), 0)
    nxt = lambda i: (jnp.minimum((rb + i + 1) * hpb, T_ALL // CONV_HALO - 1), 0)
    modmap = lambda i: (layer, _mod_index(rb + i, tm), 0, 0)
    return pl.pallas_call(
        functools.partial(_conf_out_kernel, tm=tm, bps=bps),
        grid=(nblk,),
        in_specs=[
            pl.BlockSpec((tm, D), blk),
            pl.BlockSpec((None, None, N_MOD, D), modmap),
            pl.BlockSpec((CONV_HALO, D), prev),
            pl.BlockSpec((tm, D), blk),
            pl.BlockSpec((CONV_HALO, D), nxt),
            pl.BlockSpec((None, CONV_W, D), lambda i: (o, 0, 0)),
            pl.BlockSpec((None, 1, D), lambda i: (o, 0, 0)),
            pl.BlockSpec((None, 1, D), lambda i: (o, 0, 0)),
            pl.BlockSpec((None, 1, D), lambda i: (o, 0, 0)),
            pl.BlockSpec((D, D), lambda i: (0, 0)),
            pl.BlockSpec((None, 1, D), lambda i: (o, 0, 0)),
        ],
        out_specs=pl.BlockSpec((tm, D), blk),
        out_shape=jax.ShapeDtypeStruct(xs.shape, F32),
        scratch_shapes=[pltpu.VMEM((tm + 2 * CONV_HALO, D), F32), pltpu.VMEM((tm, D), F32),
                        pltpu.VMEM((SUBLANE, tm + 2 * CONV_HALO - SUBLANE, LANE), F32)],
        input_output_aliases={0: 0},
        compiler_params=_cparams(("parallel",)),
        name="conf_out",
    )(xs, mods, u, u, u, w_dw, vec(b_dw), vec(ln_g), vec(ln_b), w_pw2, vec(b_pw2))


def _cast_pad_kernel(w_ref, o_ref, *, n, axis):
    v = w_ref[...].astype(BF16)
    if axis == 0:
        o_ref[:n, :] = v
        o_ref[n:, :] = jnp.zeros((o_ref.shape[0] - n, o_ref.shape[1]), BF16)
    else:
        o_ref[:, :n] = v
        o_ref[:, n:] = jnp.zeros((o_ref.shape[0], o_ref.shape[1] - n), BF16)


def _cast_pad(w, layer, axis, padded):
    _, r, c = w.shape
    strip = 256
    if axis == 0:
        in_blk, out_blk, out_shape, n = (None, r, strip), (padded, strip), (padded, c), r
        imap, omap = (lambda j: (layer, 0, j)), (lambda j: (0, j))
        steps = c // strip
    else:
        in_blk, out_blk, out_shape, n = (None, strip, c), (strip, padded), (r, padded), c
        imap, omap = (lambda j: (layer, j, 0)), (lambda j: (j, 0))
        steps = r // strip
    return pl.pallas_call(
        functools.partial(_cast_pad_kernel, n=n, axis=axis),
        grid=(steps,),
        in_specs=[pl.BlockSpec(in_blk, imap)],
        out_specs=pl.BlockSpec(out_blk, omap),
        out_shape=jax.ShapeDtypeStruct(out_shape, BF16),
        compiler_params=_cparams(("parallel",)),
        name="cast_pad",
    )(w)


def kernel(x, c, ctx, c_ctx, ada_w, ada_b, norm_g, ff1_w1, ff1_w3, ff1_w2, ff2_w1, ff2_w3, ff2_w2,
           mix_w_in, mix_w_out, na_rpb, hy_short_w, hy_short_b, hy_w0, hy_b0, hy_w1, hy_b1,
           hy_w2, hy_b2, hy_w3, hy_freq, hy_bias, cv_w_pw1, cv_b_pw1, cv_w_dw, cv_b_dw,
           cv_ln_g, cv_ln_b, cv_w_pw2, cv_b_pw2, final_g):
    xs = jnp.concatenate([x.reshape(T_LAT, D), ctx.reshape(T_CTX, D),
                          jnp.zeros((T_PAD - T_ALL, D), F32)], axis=0)
    cond = jnp.concatenate([c, c_ctx[None, :], jnp.zeros((8 - NB - 1, D), F32)], axis=0)
    mods = _ada(cond, ada_w, ada_b)[:, :NB + 1].reshape(DEPTH, NB + 1, N_MOD, D)

    ff1 = (ff1_w1, ff1_w3, ff1_w2)
    ff2 = (ff2_w1, ff2_w3, ff2_w2)
    wts = (_cast_pad(ff1_w1, 0, 1, DFF_PAD), _cast_pad(ff1_w3, 0, 1, DFF_PAD),
           _cast_pad(ff1_w2, 0, 0, DFF_PAD))
    mats_lat = _dft_mats(S)
    mats_ctx = _dft_mats(LC)

    out = None
    for i in range(DEPTH):
        ctx_full = i < LAST_EVEN
        mixer_w = (mix_w_in, i // 2) if i % 2 == 0 else (cv_w_pw1, i // 2)
        xs, wts, (w_proj,) = _ffn(xs, mods, norm_g[i, 0], *wts, i, 0, i <= LAST_EVEN,
                                  cast_next=(*ff2, i), cast_plain=(mixer_w,))
        w_in = w_pw1 = w_proj
        if i % 2 == 0:
            e = i // 2
            hyp = (hy_short_w[e], hy_short_b[e], hy_w0[e], hy_b0[e], hy_w1[e], hy_b1[e],
                   hy_w2[e], hy_b2[e], hy_w3[e], hy_freq[e], hy_bias[e])
            qkv, z = _inproj(xs, mods, norm_g[i, 1], w_in, i, e, i <= LAST_EVEN)
            att = _natten(qkv, na_rpb[e])
            hy = _hyena(z, hyp, mats_lat, S, 0)
            if ctx_full:
                att = _ctx_attn(qkv, att)
                hy = _hyena(z, hyp, mats_ctx, LC, T_LAT, hy)
            xs = _outproj(xs, mods, att, hy, mix_w_out[e].astype(BF16), i, e,
                          NBLK_ALL if ctx_full else NBLK_LAT)
        else:
            o = i // 2
            u = _pw1(xs, mods, norm_g[i, 1], w_pw1, cv_b_pw1, i, o, ctx_full)
            cv = (cv_w_dw, cv_b_dw, cv_ln_g, cv_ln_b, cv_w_pw2[o].astype(BF16), cv_b_pw2)
            xs = _conf_out(xs, mods, u, *cv, i, o, TM, 0, S, NBLK_LAT)
            if ctx_full:
                xs = _conf_out(xs, mods, u, *cv, i, o, LC, T_LAT, LC, T_CTX // LC)
        if i == DEPTH - 1:
            out = _ffn(xs, mods, norm_g[i, 2], *wts, i, 6, ctx_full, final_g=final_g)
        else:
            xs, wts, _ = _ffn(xs, mods, norm_g[i, 2], *wts, i, 6, ctx_full, cast_next=(*ff1, i + 1))
    return out.reshape(NB, S, D)
```

```python
import functools
import math

import numpy as np
import jax
import jax.numpy as jnp
from jax import lax
from jax.experimental import pallas as pl
from jax.experimental.pallas import tpu as pltpu

F32 = jnp.float32
BF16 = jnp.bfloat16

D = 2048
NB = 2
S = 4096
DEPTH = 4
GRID_W = 64
GRID_H = S // GRID_W
LC = 256
HD = 128
NH = 8
WA = NH * HD
WB = D - WA
WIN_R = 8
WIN_C = 16
DFF = 5504
HY_EMB = 33
HY_ORDER = 64
CONV_W = 31
N_MOD = 9
RMS_EPS = 1e-6
LN_EPS = 1e-5
LAST_EVEN = (DEPTH - 1) - ((DEPTH - 1) % 2)

T_LAT = NB * S
T_CTX = NB * LC
T_ALL = T_LAT + T_CTX
T_PAD = 9216

LANE = 128
SUBLANE = 8
TM = 512
TM_FFN = 1024
TM_PROJ = 1024
PROJ_ROWS = 512
FFN_ROWS = 512
CAST_COL_SLABS = 64
CAST_ROW = 64
TF = 512
DFF_PAD = TF * (-(-DFF // TF))
CAST_ROW_SLABS = DFF_PAD // CAST_ROW
NBLK_LAT = T_LAT // TM
NBLK_ALL = T_ALL // TM
NEG = -1e30
VMEM_LIMIT = 60 * 1024 * 1024


def _cparams(sem, vmem=VMEM_LIMIT):
    return pltpu.CompilerParams(dimension_semantics=sem, vmem_limit_bytes=vmem)


def _mod_index(i, tm):
    return jnp.minimum(i // (S // tm), NB)


def _stream_blocks(with_ctx, tm):
    return pl.cdiv(T_ALL if with_ctx else T_LAT, tm)


def _norm_mod(x, g, shift, scale):
    ms = jnp.mean(x * x, axis=-1, keepdims=True)
    return (x * lax.rsqrt(ms + RMS_EPS)) * g * (1.0 + scale) + shift


def _ada_kernel(s_ref, w_ref, b_ref, o_ref):
    s = s_ref[...]
    s = (s * jax.nn.sigmoid(s)).astype(BF16)
    o_ref[...] = jnp.dot(s, w_ref[...].astype(BF16), preferred_element_type=F32) + b_ref[...]


def _ada(cond, ada_w, ada_b):
    tn = 1024
    n = N_MOD * D
    return pl.pallas_call(
        _ada_kernel,
        grid=(DEPTH, n // tn),
        in_specs=[
            pl.BlockSpec((8, D), lambda l, j: (0, 0)),
            pl.BlockSpec((None, D, tn), lambda l, j: (l, 0, j)),
            pl.BlockSpec((None, 1, tn), lambda l, j: (l, 0, j)),
        ],
        out_specs=pl.BlockSpec((None, 8, tn), lambda l, j: (l, 0, j)),
        out_shape=jax.ShapeDtypeStruct((DEPTH, 8, n), F32),
        compiler_params=_cparams(("parallel", "parallel")),
        name="ada_mod",
    )(cond, ada_w, ada_b.reshape(DEPTH, 1, n))


def _ffn_kernel(x_ref, mod_ref, g_ref, w1_ref, w3_ref, w2_ref, *rest, jbase, nf, final, full_blocks,
                tail_rows, casting, n_plain):
    rest = list(rest)
    fg_ref = rest.pop(0) if final else None
    cast_in = [rest.pop(0) for _ in range(3)] if casting else []
    plain_in = [rest.pop(0) for _ in range(n_plain)]
    o_ref = rest.pop(0)
    cast_out = [rest.pop(0) for _ in range(3)] if casting else []
    plain_out = [rest.pop(0) for _ in range(n_plain)]
    (h_scr,) = rest
    i = pl.program_id(0)
    f = pl.program_id(1)
    tm = h_scr.shape[0]
    s = i * nf + f

    if n_plain:
        @pl.when(s < CAST_COL_SLABS)
        def _():
            for src, dst in zip(plain_in, plain_out):
                dst[...] = src[...].astype(BF16)

    if casting:
        @pl.when(s < CAST_COL_SLABS)
        def _():
            for src, dst in zip(cast_in[:2], cast_out[:2]):
                dst[:, :DFF] = src[...].astype(BF16)
                dst[:, DFF:] = jnp.zeros((dst.shape[0], DFF_PAD - DFF), BF16)

        @pl.when(s < DFF // CAST_ROW)
        def _():
            cast_out[2][...] = cast_in[2][...].astype(BF16)

        @pl.when((s >= DFF // CAST_ROW) & (s < CAST_ROW_SLABS))
        def _():
            cast_out[2][...] = jnp.zeros(cast_out[2].shape, BF16)

    def norm_rows(r, n):
        rows = slice(r, r + n)
        h = _norm_mod(x_ref[rows, :], g_ref[...], mod_ref[jbase:jbase + 1, :],
                      mod_ref[jbase + 1:jbase + 2, :])
        h_scr[rows, :] = h.astype(BF16)

    def chain(r, n, first):
        rows = slice(r, r + n)
        h = h_scr[rows, :]
        a = jnp.dot(h, w1_ref[...], preferred_element_type=F32)
        b = jnp.dot(h, w3_ref[...], preferred_element_type=F32)
        gate = (a * jax.nn.sigmoid(a) * b).astype(BF16)
        part = jnp.dot(gate, w2_ref[...], preferred_element_type=F32)
        if first:
            o_ref[rows, :] = part
        else:
            o_ref[rows, :] += part

    def epilogue(r, n):
        rows = slice(r, r + n)
        y = x_ref[rows, :] + (0.5 * mod_ref[jbase + 2:jbase + 3, :]) * o_ref[rows, :]
        if final:
            ms = jnp.mean(y * y, axis=-1, keepdims=True)
            y = (y * lax.rsqrt(ms + RMS_EPS)) * fg_ref[...]
        o_ref[rows, :] = y

    def step(live):
        chains = [(r, FFN_ROWS) for r in range(0, live, FFN_ROWS)]

        @pl.when(f == 0)
        def _():
            for r, n in chains:
                norm_rows(r, n)
                chain(r, n, True)
            if live < tm:
                o_ref[live:, :] = jnp.zeros((tm - live, o_ref.shape[1]), F32)

        @pl.when((f > 0) & (f < nf - 1))
        def _():
            for r, n in chains:
                chain(r, n, False)

        @pl.when(f == nf - 1)
        def _():
            for r, n in chains:
                chain(r, n, False)
                epilogue(r, n)

    if tail_rows == tm:
        step(tm)
    else:
        @pl.when(i < full_blocks)
        def _():
            step(tm)

        @pl.when(i >= full_blocks)
        def _():
            step(tail_rows)


def _ffn(xs, mods, g, w1, w3, w2, layer, jbase, with_ctx, final_g=None, cast_next=None, cast_plain=()):
    nf = DFF_PAD // TF
    final = final_g is not None
    tm = TM_FFN
    nblk = _stream_blocks(with_ctx, tm)
    in_specs = [
        pl.BlockSpec((tm, D), lambda i, f: (i, 0)),
        pl.BlockSpec((None, None, N_MOD, D),
                     lambda i, f: (layer, _mod_index(i, tm), 0, 0)),
        pl.BlockSpec((1, D), lambda i, f: (0, 0)),
        pl.BlockSpec((D, TF), lambda i, f: (0, f)),
        pl.BlockSpec((D, TF), lambda i, f: (0, f)),
        pl.BlockSpec((TF, D), lambda i, f: (f, 0)),
    ]
    args = [xs, mods, g.reshape(1, D), w1, w3, w2]
    out_specs = [pl.BlockSpec((tm, D), lambda i, f: (i, 0))]
    if final:
        in_specs.append(pl.BlockSpec((1, D), lambda i, f: (0, 0)))
        args.append(final_g.reshape(1, D))
        out_shape = [jax.ShapeDtypeStruct((nblk * tm, D), F32)]
        aliases = {}
    else:
        out_shape = [jax.ShapeDtypeStruct(xs.shape, F32)]
        aliases = {0: 0}
    if cast_next is not None:
        c1, c3, c2, nlayer = cast_next
        assert nblk * nf >= max(CAST_COL_SLABS, CAST_ROW_SLABS)
        step = lambda i, f: i * nf + f
        col_in = pl.BlockSpec((None, D // CAST_COL_SLABS, DFF),
                              lambda i, f: (nlayer, jnp.minimum(step(i, f), CAST_COL_SLABS - 1), 0))
        col_out = pl.BlockSpec((D // CAST_COL_SLABS, DFF_PAD),
                               lambda i, f: (jnp.minimum(step(i, f), CAST_COL_SLABS - 1), 0))
        row_in = pl.BlockSpec((None, CAST_ROW, D),
                              lambda i, f: (nlayer, jnp.minimum(step(i, f), DFF // CAST_ROW - 1), 0))
        row_out = pl.BlockSpec((CAST_ROW, D),
                               lambda i, f: (jnp.minimum(step(i, f), CAST_ROW_SLABS - 1), 0))
        in_specs += [col_in, col_in, row_in]
        args += [c1, c3, c2]
        out_specs += [col_out, col_out, row_out]
        out_shape += [jax.ShapeDtypeStruct((D, DFF_PAD), BF16), jax.ShapeDtypeStruct((D, DFF_PAD), BF16),
                      jax.ShapeDtypeStruct((DFF_PAD, D), BF16)]
    for src, slayer in cast_plain:
        _, r, c = src.shape
        slab = r // CAST_COL_SLABS
        in_specs.append(pl.BlockSpec(
            (None, slab, c), lambda i, f, slayer=slayer: (slayer, jnp.minimum(i * nf + f, CAST_COL_SLABS - 1), 0)))
        args.append(src)
        out_specs.append(pl.BlockSpec((slab, c), lambda i, f: (jnp.minimum(i * nf + f, CAST_COL_SLABS - 1), 0)))
        out_shape.append(jax.ShapeDtypeStruct((r, c), BF16))
    res = pl.pallas_call(
        functools.partial(_ffn_kernel, jbase=jbase, nf=nf, final=final, full_blocks=T_LAT // tm,
                          tail_rows=T_ALL - T_LAT if with_ctx else tm, casting=cast_next is not None,
                          n_plain=len(cast_plain)),
        grid=(nblk, nf),
        in_specs=in_specs,
        out_specs=out_specs,
        out_shape=out_shape,
        scratch_shapes=[pltpu.VMEM((tm, D), BF16)],
        input_output_aliases=aliases,
        compiler_params=_cparams(("arbitrary", "arbitrary")),
        name="ffn",
    )(*args)
    if cast_next is None and not cast_plain:
        return res[0]
    n_next = 3 if cast_next is not None else 0
    return res[0], tuple(res[1:1 + n_next]), tuple(res[1 + n_next:])


def _inproj_kernel(x_ref, mod_ref, g_ref, w_ref, qkv_ref, z_ref, h_scr, *, nq):
    j = pl.program_id(1)
    starts = range(0, h_scr.shape[0], PROJ_ROWS)

    def norm_rows(r):
        rows = slice(r, r + PROJ_ROWS)
        h = _norm_mod(x_ref[rows, :], g_ref[...], mod_ref[3:4, :], mod_ref[4:5, :])
        h_scr[rows, :] = h.astype(BF16)

    def proj(r):
        return jnp.dot(h_scr[r:r + PROJ_ROWS, :], w_ref[...], preferred_element_type=F32)

    @pl.when(j == 0)
    def _():
        for r in starts:
            norm_rows(r)
            qkv_ref[r:r + PROJ_ROWS, :] = (proj(r) * (HD ** -0.5)).astype(BF16)

    @pl.when((j > 0) & (j < nq))
    def _():
        for r in starts:
            qkv_ref[r:r + PROJ_ROWS, :] = proj(r).astype(BF16)

    @pl.when(j >= nq)
    def _():
        for r in starts:
            z_ref[r:r + PROJ_ROWS, :] = proj(r)


def _inproj(xs, mods, g, w_in, layer, e, with_ctx):
    tn = WA
    nq = 3 * WA // tn
    nz = 3 * WB // tn
    tm = TM_PROJ
    return pl.pallas_call(
        functools.partial(_inproj_kernel, nq=nq),
        grid=(_stream_blocks(with_ctx, tm), nq + nz),
        in_specs=[
            pl.BlockSpec((tm, D), lambda i, j: (i, 0)),
            pl.BlockSpec((None, None, N_MOD, D), lambda i, j: (layer, _mod_index(i, tm), 0, 0)),
            pl.BlockSpec((1, D), lambda i, j: (0, 0)),
            pl.BlockSpec((D, tn), lambda i, j: (0, j)),
        ],
        out_specs=[
            pl.BlockSpec((tm, tn), lambda i, j: (i, jnp.minimum(j, nq - 1))),
            pl.BlockSpec((tm, tn), lambda i, j: (i, jnp.maximum(j - nq, 0))),
        ],
        out_shape=[
            jax.ShapeDtypeStruct((T_PAD, 3 * WA), BF16),
            jax.ShapeDtypeStruct((T_PAD, 3 * WB), F32),
        ],
        scratch_shapes=[pltpu.VMEM((tm, D), BF16)],
        compiler_params=_cparams(("parallel", "arbitrary")),
        name="mix_inproj",
    )(xs, mods, g.reshape(1, D), w_in)


def _pw1_kernel(x_ref, mod_ref, g_ref, wa_ref, wg_ref, ba_ref, bg_ref, u_ref, h_scr):
    j = pl.program_id(1)
    starts = range(0, h_scr.shape[0], PROJ_ROWS)

    def norm_rows(r):
        rows = slice(r, r + PROJ_ROWS)
        h = _norm_mod(x_ref[rows, :], g_ref[...], mod_ref[3:4, :], mod_ref[4:5, :])
        h_scr[rows, :] = h.astype(BF16)

    def glu(r):
        rows = slice(r, r + PROJ_ROWS)
        h = h_scr[rows, :]
        a = jnp.dot(h, wa_ref[...], preferred_element_type=F32) + ba_ref[...]
        gt = jnp.dot(h, wg_ref[...], preferred_element_type=F32) + bg_ref[...]
        u_ref[rows, :] = a * jax.nn.sigmoid(gt)

    @pl.when(j == 0)
    def _():
        for r in starts:
            norm_rows(r)
            glu(r)

    @pl.when(j > 0)
    def _():
        for r in starts:
            glu(r)


def _pw1(xs, mods, g, w_pw1, b_pw1, layer, o, with_ctx):
    tn = 512
    nj = D // tn
    tm = TM_PROJ
    b3 = b_pw1.reshape(-1, 1, 2 * D)
    return pl.pallas_call(
        _pw1_kernel,
        grid=(_stream_blocks(with_ctx, tm), nj),
        in_specs=[
            pl.BlockSpec((tm, D), lambda i, j: (i, 0)),
            pl.BlockSpec((None, None, N_MOD, D), lambda i, j: (layer, _mod_index(i, tm), 0, 0)),
            pl.BlockSpec((1, D), lambda i, j: (0, 0)),
            pl.BlockSpec((D, tn), lambda i, j: (0, j)),
            pl.BlockSpec((D, tn), lambda i, j: (0, j + nj)),
            pl.BlockSpec((None, 1, tn), lambda i, j: (o, 0, j)),
            pl.BlockSpec((None, 1, tn), lambda i, j: (o, 0, j + nj)),
        ],
        out_specs=pl.BlockSpec((tm, tn), lambda i, j: (i, j)),
        out_shape=jax.ShapeDtypeStruct((T_PAD, D), F32),
        scratch_shapes=[pltpu.VMEM((tm, D), BF16)],
        compiler_params=_cparams(("parallel", "arbitrary")),
        name="conf_pw1",
    )(xs, mods, g.reshape(1, D), w_pw1, w_pw1, b3, b3)


NAT_G = 8
NAT_KR = {1: 8, 2: 10, 4: 12, 8: 16}[NAT_G]


def _natten_geometry():
    kstart, types, type_id = [], [], []
    gi = np.arange(NAT_G)[:, None]
    kj = np.arange(NAT_KR)[None, :]
    for g in range(GRID_H // NAT_G):
        r0 = g * NAT_G
        ks = min(max(r0 - WIN_R // 2, 0), GRID_H - NAT_KR)
        qrow, krow = r0 + gi, ks + kj
        rs = np.clip(qrow - WIN_R // 2, 0, GRID_H - WIN_R)
        valid = (krow >= rs) & (krow < rs + WIN_R)
        assert (valid.sum(axis=1) == WIN_R).all()
        dr = np.where(valid, krow - qrow + WIN_R - 1, 2 * WIN_R - 1).astype(np.int32)
        for t, d0 in enumerate(types):
            if (d0 == dr).all():
                type_id.append(t)
                break
        else:
            type_id.append(len(types))
            types.append(dr)
        kstart.append(ks)
    return np.asarray(kstart, np.int32), np.asarray(type_id, np.int32), np.stack(types)


def _natten_bias(rpb):
    _, _, dr = _natten_geometry()
    ntypes = dr.shape[0]
    c = np.arange(GRID_W)[:, None]
    kc = np.arange(GRID_W)[None, :]
    cs = np.clip(c - WIN_C // 2, 0, GRID_W - WIN_C)
    cvalid = (kc >= cs) & (kc < cs + WIN_C)
    dc = kc - c + WIN_C - 1
    onehot = ((np.arange(2 * WIN_C - 1)[:, None, None] == dc[None]) & cvalid[None]).astype(np.float32)
    colmask = np.where(cvalid, 0.0, NEG).astype(np.float32)
    bcols = jnp.einsum("hrd,dck->hrck", rpb, jnp.asarray(onehot), precision=lax.Precision.HIGHEST)
    bcols = (bcols + jnp.asarray(colmask)).transpose(0, 2, 1, 3)
    slabs = []
    for ty in range(ntypes):
        for gi in range(NAT_G):
            kj = np.nonzero(dr[ty, gi] != 2 * WIN_R - 1)[0]
            a, r0 = int(kj[0]), int(dr[ty, gi, kj[0]])
            assert (kj == a + np.arange(WIN_R)).all() and (dr[ty, gi, kj] == r0 + np.arange(WIN_R)).all()
            s = bcols[:, :, r0:r0 + WIN_R, :].reshape(NH, GRID_W, WIN_R * GRID_W)
            slabs.append(jnp.pad(s, ((0, 0), (0, 0), (a * GRID_W, (NAT_KR - a - WIN_R) * GRID_W)),
                                 constant_values=NEG))
    return jnp.stack(slabs, axis=1).reshape(NH, ntypes, NAT_G * GRID_W, NAT_KR * GRID_W)


def _natten_kernel(ks_ref, ty_ref, q_ref, k_ref, v_ref, kc_ref, vc_ref, bias_ref, o_ref):
    gq, kk = NAT_G * GRID_W, NAT_KR * GRID_W
    kc = kc_ref[...]
    vc = vc_ref[...]
    nt = (((1,), (1,)), ((), ()))

    def body(g, carry):
        q0 = pl.multiple_of(g * gq, gq)
        k0 = pl.multiple_of(ks_ref[g] * GRID_W, GRID_W)
        q = q_ref[pl.ds(q0, gq), :]
        k = k_ref[pl.ds(k0, kk), :]
        v = v_ref[pl.ds(k0, kk), :]
        s_loc = lax.dot_general(q, k, nt, preferred_element_type=F32) + bias_ref[ty_ref[g]]
        s_ctx = lax.dot_general(q, kc, nt, preferred_element_type=F32)
        m = jnp.maximum(jnp.max(s_loc, axis=-1, keepdims=True),
                        jnp.max(s_ctx, axis=-1, keepdims=True))
        p_loc = jnp.exp(s_loc - m)
        p_ctx = jnp.exp(s_ctx - m)
        den = jnp.sum(p_loc, axis=-1, keepdims=True) + jnp.sum(p_ctx, axis=-1, keepdims=True)
        o = (jnp.dot(p_loc.astype(BF16), v, preferred_element_type=F32)
             + jnp.dot(p_ctx.astype(BF16), vc, preferred_element_type=F32))
        o_ref[pl.ds(q0, gq), :] = (o / den).astype(BF16)
        return carry

    lax.fori_loop(0, GRID_H // NAT_G, body, 0, unroll=True)


def _natten(qkv, rpb):
    kstart, type_id, _ = _natten_geometry()
    bias = _natten_bias(rpb)
    _, ntypes, gq, kk = bias.shape
    cb = T_LAT // LC
    grid_spec = pltpu.PrefetchScalarGridSpec(
        num_scalar_prefetch=2,
        grid=(NB, NH),
        in_specs=[
            pl.BlockSpec((S, HD), lambda b, h, *_: (b, h)),
            pl.BlockSpec((S, HD), lambda b, h, *_: (b, NH + h)),
            pl.BlockSpec((S, HD), lambda b, h, *_: (b, 2 * NH + h)),
            pl.BlockSpec((LC, HD), lambda b, h, *_: (cb + b, NH + h)),
            pl.BlockSpec((LC, HD), lambda b, h, *_: (cb + b, 2 * NH + h)),
            pl.BlockSpec((None, ntypes, gq, kk), lambda b, h, *_: (h, 0, 0, 0)),
        ],
        out_specs=pl.BlockSpec((S, HD), lambda b, h, *_: (b, h)),
    )
    return pl.pallas_call(
        _natten_kernel,
        grid_spec=grid_spec,
        out_shape=jax.ShapeDtypeStruct((T_ALL, WA), BF16),
        compiler_params=_cparams(("parallel", "parallel")),
        name="natten",
    )(jnp.asarray(kstart), jnp.asarray(type_id), qkv, qkv, qkv, qkv, qkv, bias)


def _ctx_attn_kernel(q_ref, k_ref, v_ref, att_hbm, o_ref):
    del att_hbm
    s = lax.dot_general(q_ref[...], k_ref[...], (((1,), (1,)), ((), ())),
                        preferred_element_type=F32)
    m = jnp.max(s, axis=-1, keepdims=True)
    p = jnp.exp(s - m)
    den = jnp.sum(p, axis=-1, keepdims=True)
    o = jnp.dot(p.astype(BF16), v_ref[...], preferred_element_type=F32)
    o_ref[...] = (o / den).astype(BF16)


def _ctx_attn(qkv, att):
    cb = T_LAT // LC
    return pl.pallas_call(
        _ctx_attn_kernel,
        grid=(NB, NH),
        in_specs=[
            pl.BlockSpec((LC, HD), lambda b, h: (cb + b, h)),
            pl.BlockSpec((LC, HD), lambda b, h: (cb + b, NH + h)),
            pl.BlockSpec((LC, HD), lambda b, h: (cb + b, 2 * NH + h)),
            pl.BlockSpec(memory_space=pl.ANY),
        ],
        out_specs=pl.BlockSpec((LC, HD), lambda b, h: (cb + b, h)),
        out_shape=jax.ShapeDtypeStruct(att.shape, att.dtype),
        input_output_aliases={3: 0},
        compiler_params=_cparams(("parallel", "parallel")),
        name="ctx_attn",
    )(qkv, qkv, qkv, att)


def _dft_tables(length, parity):
    n2 = 4 * length
    half = length // 2
    r = jnp.arange(half, dtype=jnp.int32)[:, None]
    hi = jnp.arange(half // LANE, dtype=jnp.int32)[None, :]
    lo = jnp.arange(LANE, dtype=jnp.int32)[None, :]

    def cs(phase):
        ang = (phase % n2).astype(F32) * (2.0 * math.pi / n2)
        return jnp.cos(ang), jnp.sin(ang)

    km = cs((2 * r + 1) * (2 * LANE * hi)) + cs((2 * r + 1) * (2 * lo + parity))
    mk = cs((2 * r + parity) * (2 * LANE * hi)) + cs((2 * r + parity) * (2 * lo + 1))
    return km, mk


def _dftgen_kernel(ca_ref, sa_ref, cb_ref, sb_ref, c_ref, s_ref, *, nt, sgn):
    cb = cb_ref[...]
    sb = sb_ref[...]
    for t1 in range(nt):
        ca = ca_ref[:, t1:t1 + 1]
        sa = sa_ref[:, t1:t1 + 1]
        sl = slice(t1 * LANE, (t1 + 1) * LANE)
        c_ref[:, sl] = (ca * cb - sa * sb).astype(BF16)
        s_ref[:, sl] = (sgn * (sa * cb + ca * sb)).astype(BF16)


def _dftgen(tables, length, sgn):
    tr = min(256, length)
    nt = length // LANE
    row = lambda i: (i, 0)
    return pl.pallas_call(
        functools.partial(_dftgen_kernel, nt=nt, sgn=sgn),
        grid=(length // tr,),
        in_specs=[pl.BlockSpec((tr, nt), row), pl.BlockSpec((tr, nt), row),
                  pl.BlockSpec((tr, LANE), row), pl.BlockSpec((tr, LANE), row)],
        out_specs=[pl.BlockSpec((tr, length), row), pl.BlockSpec((tr, length), row)],
        out_shape=[jax.ShapeDtypeStruct((length, length), BF16)] * 2,
        compiler_params=_cparams(("parallel",)),
        name="dft_gen",
    )(*tables)


def _hy_pre_kernel(z0_ref, z1_ref, z2_ref, w_ref, b_ref, u16_ref, u32_ref, x0_ref):
    half = z0_ref.shape[0] // 2

    def conv(z_ref, part):
        ze = z_ref[pl.ds(0, half, stride=2), :]
        zo = z_ref[pl.ds(1, half, stride=2), :]
        row = lax.broadcasted_iota(jnp.int32, ze.shape, 0)
        zo_prev = jnp.where(row == 0, 0.0, pltpu.roll(zo, 1, 0))
        ze_next = jnp.where(row == half - 1, 0.0, pltpu.roll(ze, half - 1, 0))
        w = w_ref[part]
        bias = b_ref[part]
        even = zo_prev * w[0:1, :] + ze * w[1:2, :] + zo * w[2:3, :] + bias
        odd = ze * w[0:1, :] + zo * w[1:2, :] + ze_next * w[2:3, :] + bias
        return even, odd

    x0 = conv(z0_ref, 0)
    x1 = conv(z1_ref, 1)
    v = conv(z2_ref, 2)
    for p in range(2):
        u = v[p] * x1[p]
        x0_ref[p] = x0[p]
        u32_ref[p] = u
        u16_ref[p] = u.astype(BF16)


def _hy_pre(z, short_w, short_b, length, row_off):
    tc = LANE
    nc = WB // tc
    rb = row_off // length
    half = length // 2
    w = short_w.reshape(3, 3, WB).transpose(1, 0, 2)
    bb = short_b.reshape(3, 1, WB)
    out_spec = pl.BlockSpec((2, half, tc), lambda b, c: (0, 0, b * nc + c))
    return pl.pallas_call(
        _hy_pre_kernel,
        grid=(NB, nc),
        in_specs=[
            pl.BlockSpec((length, tc), lambda b, c: (rb + b, c)),
            pl.BlockSpec((length, tc), lambda b, c: (rb + b, nc + c)),
            pl.BlockSpec((length, tc), lambda b, c: (rb + b, 2 * nc + c)),
            pl.BlockSpec((3, 3, tc), lambda b, c: (0, 0, c)),
            pl.BlockSpec((3, 1, tc), lambda b, c: (0, 0, c)),
        ],
        out_specs=[out_spec, out_spec, out_spec],
        out_shape=[jax.ShapeDtypeStruct((2, half, NB * WB), BF16),
                   jax.ShapeDtypeStruct((2, half, NB * WB), F32),
                   jax.ShapeDtypeStruct((2, half, NB * WB), F32)],
        compiler_params=_cparams(("parallel", "parallel")),
        name="hyena_pre",
    )(z, z, z, w, bb)


def _hy_filter_kernel(z_ref, w0_ref, b0_ref, w1_ref, b1_ref, w2_ref, b2_ref, w3_ref, fr_ref,
                      dl_ref, fs_ref, fd_ref):
    hp = lax.Precision.HIGHEST
    z = z_ref[...]
    fr = fr_ref[...]
    h = jnp.sin(fr * (jnp.dot(z, w0_ref[...], precision=hp, preferred_element_type=F32) + b0_ref[...]))
    h = jnp.sin(fr * (jnp.dot(h, w1_ref[...], precision=hp, preferred_element_type=F32) + b1_ref[...]))
    h = jnp.sin(fr * (jnp.dot(h, w2_ref[...], precision=hp, preferred_element_type=F32) + b2_ref[...]))
    hh = jnp.dot(h.astype(BF16), w3_ref[...].astype(BF16), preferred_element_type=F32)
    win = jnp.exp(-z[:, 0:1] * dl_ref[...])
    fwd = hh[:, :WB] * win
    bwd = hh[:, WB:] * win
    row = lax.broadcasted_iota(jnp.int32, bwd.shape, 0) + pl.program_id(0) * z.shape[0]
    bwd = jnp.where(row == 0, 0.0, bwd)
    fs_ref[...] = (fwd + bwd).astype(BF16)
    fd_ref[...] = (bwd - fwd).astype(BF16)


def _hy_filter(length, w0, b0, w1, b1, w2, b2, w3, freq):
    t = jnp.linspace(0.0, 1.0, length, dtype=F32)[:, None]
    bands = (HY_EMB - 1) // 2
    f = jnp.linspace(1e-4, bands - 1, bands, dtype=F32)
    w = 2 * math.pi * jnp.arange(length, dtype=F32)[:, None] / length
    z = jnp.concatenate([t, jnp.cos(f * w), -jnp.sin(f * w)], axis=-1)
    z = jnp.concatenate([z[0::2], z[1::2]], axis=0)
    emb = HY_ORDER
    z = jnp.pad(z, ((0, 0), (0, emb - HY_EMB)))
    w0p = jnp.pad(w0, ((0, emb - HY_EMB), (0, 0)))
    max_decay = math.log(1e-2) / 0.3
    min_decay = math.log(1e-2) / 1.5
    deltas = jnp.abs(jnp.linspace(min_decay, max_decay, WB, dtype=F32))[None, :]
    tt = min(256, length)
    full = lambda shape: pl.BlockSpec(shape, lambda i: (0,) * len(shape))
    row = lambda i: (i, 0)
    vec = lambda a: a.reshape(1, -1)
    return pl.pallas_call(
        _hy_filter_kernel,
        grid=(length // tt,),
        in_specs=[pl.BlockSpec((tt, emb), row),
                  full((emb, HY_ORDER)), full((1, HY_ORDER)),
                  full((HY_ORDER, HY_ORDER)), full((1, HY_ORDER)),
                  full((HY_ORDER, HY_ORDER)), full((1, HY_ORDER)),
                  full((HY_ORDER, 2 * WB)), full((1, HY_ORDER)), full((1, WB))],
        out_specs=[pl.BlockSpec((tt, WB), row), pl.BlockSpec((tt, WB), row)],
        out_shape=[jax.ShapeDtypeStruct((length, WB), BF16)] * 2,
        compiler_params=_cparams(("parallel",)),
        name="hyena_filter",
    )(z, w0p, vec(b0), w1, vec(b1), w2, vec(b2), w3, vec(freq), deltas)


def _dft_tiles(length):
    return min(512, length // 2), 512


def _dft_mat_specs(tr, half):
    return [pl.BlockSpec((tr, half), lambda j, i: (i, 0))] * 4


def _dft_spec_kernel(ce_ref, se_ref, co_ref, so_ref, fs_ref, fd_ref, a_ref, b_ref, *, scale):
    ae = jnp.dot(ce_ref[...], fs_ref[0], preferred_element_type=F32)
    ao = jnp.dot(co_ref[...], fs_ref[1], preferred_element_type=F32)
    be = jnp.dot(se_ref[...], fd_ref[0], preferred_element_type=F32)
    bo = jnp.dot(so_ref[...], fd_ref[1], preferred_element_type=F32)
    a_ref[0] = (ae + ao) * scale
    a_ref[1] = (ae - ao) * scale
    b_ref[0] = (be + bo) * scale
    b_ref[1] = (bo - be) * scale


def _dft_spec(mats, fs, fd, length):
    tr, tc = _dft_tiles(length)
    half = length // 2
    return pl.pallas_call(
        functools.partial(_dft_spec_kernel, scale=1.0 / length),
        grid=(WB // tc, half // tr),
        in_specs=_dft_mat_specs(tr, half) + [pl.BlockSpec((2, half, tc), lambda j, i: (0, 0, j))] * 2,
        out_specs=[pl.BlockSpec((2, tr, tc), lambda j, i: (0, i, j))] * 2,
        out_shape=[jax.ShapeDtypeStruct((2, half, WB), F32)] * 2,
        compiler_params=_cparams(("parallel", "parallel")),
        name="hyena_filter_dft",
    )(*mats, fs, fd)


def _dft_fwd_kernel(ce_ref, se_ref, co_ref, so_ref, u_ref, a_ref, b_ref, y_ref):
    ue = u_ref[0]
    uo = u_ref[1]
    pe = jnp.dot(ce_ref[...], ue, preferred_element_type=F32)
    po = jnp.dot(co_ref[...], uo, preferred_element_type=F32)
    qe = jnp.dot(se_ref[...], ue, preferred_element_type=F32)
    qo = jnp.dot(so_ref[...], uo, preferred_element_type=F32)
    re, im = [], []
    for h, (p, q) in enumerate(((pe + po, qe + qo), (pe - po, qo - qe))):
        a = a_ref[h]
        b = b_ref[h]
        re.append(p * a + q * b)
        im.append(p * b - q * a)
    y_ref[0] = (re[0] + re[1]).astype(BF16)
    y_ref[1] = (im[0] - im[1]).astype(BF16)
    y_ref[2] = (re[0] - re[1]).astype(BF16)
    y_ref[3] = (im[0] + im[1]).astype(BF16)


def _dft_fwd(mats, u16, a, bq, length):
    tr, tc = _dft_tiles(length)
    half = length // 2
    ncj = WB // tc
    return pl.pallas_call(
        _dft_fwd_kernel,
        grid=(NB * ncj, half // tr),
        in_specs=_dft_mat_specs(tr, half) + [
            pl.BlockSpec((2, half, tc), lambda j, i: (0, 0, j)),
            pl.BlockSpec((2, tr, tc), lambda j, i: (0, i, j % ncj)),
            pl.BlockSpec((2, tr, tc), lambda j, i: (0, i, j % ncj))],
        out_specs=pl.BlockSpec((4, tr, tc), lambda j, i: (0, i, j)),
        out_shape=jax.ShapeDtypeStruct((4, half, NB * WB), BF16),
        compiler_params=_cparams(("parallel", "parallel")),
        name="hyena_fwd_dft",
    )(*mats, u16, a, bq)


def _dft_inv_kernel(ce_ref, se_ref, co_ref, so_ref, y_ref, u_ref, x0_ref, bias_ref, *rest):
    o_ref, y_scr = rest[-2:]
    tr = ce_ref.shape[0]
    ye = (jnp.dot(ce_ref[...], y_ref[0], preferred_element_type=F32)
          + jnp.dot(se_ref[...], y_ref[1], preferred_element_type=F32))
    yo = (jnp.dot(co_ref[...], y_ref[2], preferred_element_type=F32)
          + jnp.dot(so_ref[...], y_ref[3], preferred_element_type=F32))
    bias = bias_ref[...]
    oe = (ye + u_ref[0] * bias) * x0_ref[0]
    oo = (yo + u_ref[1] * bias) * x0_ref[1]
    for c in range(y_scr.shape[0]):
        lanes = slice(c * LANE, (c + 1) * LANE)
        y_scr[c, pl.ds(0, tr, stride=2), :] = oe[:, lanes]
        y_scr[c, pl.ds(1, tr, stride=2), :] = oo[:, lanes]
        o_ref[:, lanes] = y_scr[c].astype(BF16)


def _dft_inv(mats, ysum, u32, x0, bias, length, row_off, hy=None):
    tr, tc = _dft_tiles(length)
    half = length // 2
    ncj = WB // tc
    nri = half // tr
    rb = row_off // (2 * tr)
    aliased = hy is not None
    in_specs = _dft_mat_specs(tr, half) + [
        pl.BlockSpec((4, half, tc), lambda j, i: (0, 0, j)),
        pl.BlockSpec((2, tr, tc), lambda j, i: (0, i, j)),
        pl.BlockSpec((2, tr, tc), lambda j, i: (0, i, j)),
        pl.BlockSpec((1, tc), lambda j, i: (0, j % ncj))]
    args = [*mats, ysum, u32, x0, bias.reshape(1, WB)]
    if aliased:
        in_specs.append(pl.BlockSpec(memory_space=pl.ANY))
        args.append(hy)
    return pl.pallas_call(
        _dft_inv_kernel,
        grid=(NB * ncj, nri),
        in_specs=in_specs,
        out_specs=pl.BlockSpec((2 * tr, tc), lambda j, i: (rb + (j // ncj) * nri + i, j % ncj)),
        out_shape=jax.ShapeDtypeStruct((T_ALL, WB), BF16),
        scratch_shapes=[pltpu.VMEM((tc // LANE, 2 * tr, LANE), F32)],
        input_output_aliases={len(args) - 1: 0} if aliased else {},
        compiler_params=_cparams(("parallel", "parallel")),
        name="hyena_inv_dft",
    )(*args)


def _hyena(z, hy_params, mats, length, row_off, hy=None):
    short_w, short_b, w0, b0, w1, b1, w2, b2, w3, freq, bias = hy_params
    fwd_mats, inv_mats = mats
    half = length // 2
    u16, u32, x0 = _hy_pre(z, short_w, short_b, length, row_off)
    fs, fd = _hy_filter(length, w0, b0, w1, b1, w2, b2, w3, freq)
    a, bq = _dft_spec(fwd_mats, fs.reshape(2, half, WB), fd.reshape(2, half, WB), length)
    ysum = _dft_fwd(fwd_mats, u16, a, bq, length)
    return _dft_inv(inv_mats, ysum, u32, x0, bias, length, row_off, hy)


def _dft_mats(length):
    half = length // 2
    km_e, mk_e = _dft_tables(length, 0)
    km_o, mk_o = _dft_tables(length, 1)
    fwd = _dftgen(km_e, half, 1.0) + _dftgen(km_o, half, 1.0)
    inv = _dftgen(mk_e, half, -1.0) + _dftgen(mk_o, half, -1.0)
    return tuple(fwd), tuple(inv)


def _outproj_kernel(x_ref, mod_ref, a_ref, y_ref, wa_ref, wy_ref, o_ref):
    o = (jnp.dot(a_ref[...], wa_ref[...], preferred_element_type=F32)
         + jnp.dot(y_ref[...], wy_ref[...], preferred_element_type=F32))
    o_ref[...] = x_ref[...] + mod_ref[5:6, :] * o


def _outproj(xs, mods, att, hy, w_out, layer, e, nblk):
    return pl.pallas_call(
        _outproj_kernel,
        grid=(nblk,),
        in_specs=[
            pl.BlockSpec((TM, D), lambda i: (i, 0)),
            pl.BlockSpec((None, None, N_MOD, D), lambda i: (layer, _mod_index(i, TM), 0, 0)),
            pl.BlockSpec((TM, WA), lambda i: (i, 0)),
            pl.BlockSpec((TM, WB), lambda i: (i, 0)),
            pl.BlockSpec((WA, D), lambda i: (0, 0)),
            pl.BlockSpec((WB, D), lambda i: (1, 0)),
        ],
        out_specs=pl.BlockSpec((TM, D), lambda i: (i, 0)),
        out_shape=jax.ShapeDtypeStruct(xs.shape, F32),
        input_output_aliases={0: 0},
        compiler_params=_cparams(("parallel",)),
        name="mix_outproj",
    )(xs, mods, att, hy, w_out, w_out)


CONV_HALO = 16
CONV_RC = 64


def _conf_out_kernel(x_ref, mod_ref, up_ref, uc_ref, un_ref, wdw_ref, bdw_ref, lg_ref, lb_ref,
                     w2_ref, b2_ref, o_ref, ext_scr, cv_scr, ph_scr, *, tm, bps):
    i = pl.program_id(0)
    first = (i % bps) == 0
    last = (i % bps) == bps - 1
    halo = CONV_HALO
    zero = jnp.zeros((halo, D), F32)
    ext_scr[0:halo, :] = jnp.where(first, zero, up_ref[...])
    ext_scr[halo:halo + tm, :] = uc_ref[...]
    ext_scr[halo + tm:2 * halo + tm, :] = jnp.where(last, zero, un_ref[...])
    off = halo - CONV_W // 2

    def col_body(cc, carry):
        c0 = pl.multiple_of(cc * LANE, LANE)
        wv = wdw_ref[:, pl.ds(c0, LANE)]
        bv = bdw_ref[:, pl.ds(c0, LANE)]

        span = ph_scr.shape[1]
        for p in range(1, SUBLANE):
            ph_scr[p] = ext_scr[pl.ds(p, span), pl.ds(c0, LANE)]
        for r0 in range(0, tm, CONV_RC):
            acc = jnp.zeros((CONV_RC, LANE), F32)
            for j in range(CONV_W):
                p = (off + j) % SUBLANE
                base = off + j - p + r0
                if p == 0:
                    rows = ext_scr[pl.ds(base, CONV_RC), pl.ds(c0, LANE)]
                else:
                    rows = ph_scr[p, pl.ds(base, CONV_RC), :]
                acc = acc + rows * wv[j:j + 1, :]
            cv_scr[pl.ds(r0, CONV_RC), pl.ds(c0, LANE)] = acc + bv
        return carry

    lax.fori_loop(0, D // LANE, col_body, 0)

    v = cv_scr[...]
    mu = jnp.mean(v, axis=-1, keepdims=True)
    vc = v - mu
    var = jnp.mean(vc * vc, axis=-1, keepdims=True)
    t = vc * lax.rsqrt(var + LN_EPS) * lg_ref[...] + lb_ref[...]
    t = (t * jax.nn.sigmoid(t)).astype(BF16)
    o = jnp.dot(t, w2_ref[...], preferred_element_type=F32) + b2_ref[...]
    o_ref[...] = x_ref[...] + mod_ref[5:6, :] * o


def _conf_out(xs, mods, u, w_dw, b_dw, ln_g, ln_b, w_pw2, b_pw2, layer, o, tm, row_off, seq, nblk):
    bps = seq // tm
    rb = row_off // tm
    vec = lambda a: a.reshape(-1, 1, D)
    blk = lambda i: (rb + i, 0)
    hpb = tm // CONV_HALO
    prev = lambda i: (jnp.maximum((rb + i) * hpb - 1, 0), 0)
    nxt = lambda i: (jnp.minimum((rb + i + 1) * hpb, T_ALL // CONV_HALO - 1), 0)
    modmap = lambda i: (layer, _mod_index(rb + i, tm), 0, 0)
    return pl.pallas_call(
        functools.partial(_conf_out_kernel, tm=tm, bps=bps),
        grid=(nblk,),
        in_specs=[
            pl.BlockSpec((tm, D), blk),
            pl.BlockSpec((None, None, N_MOD, D), modmap),
            pl.BlockSpec((CONV_HALO, D), prev),
            pl.BlockSpec((tm, D), blk),
            pl.BlockSpec((CONV_HALO, D), nxt),
            pl.BlockSpec((None, CONV_W, D), lambda i: (o, 0, 0)),
            pl.BlockSpec((None, 1, D), lambda i: (o, 0, 0)),
            pl.BlockSpec((None, 1, D), lambda i: (o, 0, 0)),
            pl.BlockSpec((None, 1, D), lambda i: (o, 0, 0)),
            pl.BlockSpec((D, D), lambda i: (0, 0)),
            pl.BlockSpec((None, 1, D), lambda i: (o, 0, 0)),
        ],
        out_specs=pl.BlockSpec((tm, D), blk),
        out_shape=jax.ShapeDtypeStruct(xs.shape, F32),
        scratch_shapes=[pltpu.VMEM((tm + 2 * CONV_HALO, D), F32), pltpu.VMEM((tm, D), F32),
                        pltpu.VMEM((SUBLANE, tm + 2 * CONV_HALO - SUBLANE, LANE), F32)],
        input_output_aliases={0: 0},
        compiler_params=_cparams(("parallel",)),
        name="conf_out",
    )(xs, mods, u, u, u, w_dw, vec(b_dw), vec(ln_g), vec(ln_b), w_pw2, vec(b_pw2))


def _cast_pad_kernel(w_ref, o_ref, *, n, axis):
    v = w_ref[...].astype(BF16)
    if axis == 0:
        o_ref[:n, :] = v
        o_ref[n:, :] = jnp.zeros((o_ref.shape[0] - n, o_ref.shape[1]), BF16)
    else:
        o_ref[:, :n] = v
        o_ref[:, n:] = jnp.zeros((o_ref.shape[0], o_ref.shape[1] - n), BF16)


def _cast_pad(w, layer, axis, padded):
    _, r, c = w.shape
    strip = 256
    if axis == 0:
        in_blk, out_blk, out_shape, n = (None, r, strip), (padded, strip), (padded, c), r
        imap, omap = (lambda j: (layer, 0, j)), (lambda j: (0, j))
        steps = c // strip
    else:
        in_blk, out_blk, out_shape, n = (None, strip, c), (strip, padded), (r, padded), c
        imap, omap = (lambda j: (layer, j, 0)), (lambda j: (j, 0))
        steps = r // strip
    return pl.pallas_call(
        functools.partial(_cast_pad_kernel, n=n, axis=axis),
        grid=(steps,),
        in_specs=[pl.BlockSpec(in_blk, imap)],
        out_specs=pl.BlockSpec(out_blk, omap),
        out_shape=jax.ShapeDtypeStruct(out_shape, BF16),
        compiler_params=_cparams(("parallel",)),
        name="cast_pad",
    )(w)


def kernel(x, c, ctx, c_ctx, ada_w, ada_b, norm_g, ff1_w1, ff1_w3, ff1_w2, ff2_w1, ff2_w3, ff2_w2,
           mix_w_in, mix_w_out, na_rpb, hy_short_w, hy_short_b, hy_w0, hy_b0, hy_w1, hy_b1,
           hy_w2, hy_b2, hy_w3, hy_freq, hy_bias, cv_w_pw1, cv_b_pw1, cv_w_dw, cv_b_dw,
           cv_ln_g, cv_ln_b, cv_w_pw2, cv_b_pw2, final_g):
    xs = jnp.concatenate([x.reshape(T_LAT, D), ctx.reshape(T_CTX, D),
                          jnp.zeros((T_PAD - T_ALL, D), F32)], axis=0)
    cond = jnp.concatenate([c, c_ctx[None, :], jnp.zeros((8 - NB - 1, D), F32)], axis=0)
    mods = _ada(cond, ada_w, ada_b)[:, :NB + 1].reshape(DEPTH, NB + 1, N_MOD, D)

    ff1 = (ff1_w1, ff1_w3, ff1_w2)
    ff2 = (ff2_w1, ff2_w3, ff2_w2)
    wts = (_cast_pad(ff1_w1, 0, 1, DFF_PAD), _cast_pad(ff1_w3, 0, 1, DFF_PAD),
           _cast_pad(ff1_w2, 0, 0, DFF_PAD))
    mats_lat = _dft_mats(S)
    mats_ctx = _dft_mats(LC)

    out = None
    for i in range(DEPTH):
        ctx_full = i < LAST_EVEN
        mixer_w = (mix_w_in, i // 2) if i % 2 == 0 else (cv_w_pw1, i // 2)
        xs, wts, (w_proj,) = _ffn(xs, mods, norm_g[i, 0], *wts, i, 0, i <= LAST_EVEN,
                                  cast_next=(*ff2, i), cast_plain=(mixer_w,))
        if i % 2 == 0:
            e = i // 2
            hyp = (hy_short_w[e], hy_short_b[e], hy_w0[e], hy_b0[e], hy_w1[e], hy_b1[e],
                   hy_w2[e], hy_b2[e], hy_w3[e], hy_freq[e], hy_bias[e])
            qkv, z = _inproj(xs, mods, norm_g[i, 1], w_proj, i, e, i <= LAST_EVEN)
            att = _natten(qkv, na_rpb[e])
            hy = _hyena(z, hyp, mats_lat, S, 0)
            if ctx_full:
                att = _ctx_attn(qkv, att)
                hy = _hyena(z, hyp, mats_ctx, LC, T_LAT, hy)
            xs = _outproj(xs, mods, att, hy, mix_w_out[e].astype(BF16), i, e,
                          NBLK_ALL if ctx_full else NBLK_LAT)
        else:
            o = i // 2
            u = _pw1(xs, mods, norm_g[i, 1], w_proj, cv_b_pw1, i, o, ctx_full)
            cv = (cv_w_dw, cv_b_dw, cv_ln_g, cv_ln_b, cv_w_pw2[o].astype(BF16), cv_b_pw2)
            xs = _conf_out(xs, mods, u, *cv, i, o, TM, 0, S, NBLK_LAT)
            if ctx_full:
                xs = _conf_out(xs, mods, u, *cv, i, o, LC, T_LAT, LC, T_CTX // LC)
        if i == DEPTH - 1:
            out = _ffn(xs, mods, norm_g[i, 2], *wts, i, 6, ctx_full, final_g=final_g)
        else:
            xs, wts, _ = _ffn(xs, mods, norm_g[i, 2], *wts, i, 6, ctx_full, cast_next=(*ff1, i + 1))
    return out.reshape(NB, S, D)
```

```python
import functools
import math

import numpy as np
import jax
import jax.numpy as jnp
from jax import lax
from jax.experimental import pallas as pl
from jax.experimental.pallas import tpu as pltpu

F32 = jnp.float32
BF16 = jnp.bfloat16

D = 2048
NB = 2
S = 4096
DEPTH = 4
GRID_W = 64
GRID_H = S // GRID_W
LC = 256
HD = 128
NH = 8
WA = NH * HD
WB = D - WA
WIN_R = 8
WIN_C = 16
DFF = 5504
HY_EMB = 33
HY_ORDER = 64
CONV_W = 31
N_MOD = 9
RMS_EPS = 1e-6
LN_EPS = 1e-5
LAST_EVEN = (DEPTH - 1) - ((DEPTH - 1) % 2)

T_LAT = NB * S
T_CTX = NB * LC
T_ALL = T_LAT + T_CTX
T_PAD = 9216

LANE = 128
SUBLANE = 8
TM = 512
TM_FFN = 1024
TM_PROJ = 1024
PROJ_ROWS = 512
FFN_ROWS = 512
CAST_COL_SLABS = 64
CAST_ROW = 64
TF = 512
DFF_PAD = TF * (-(-DFF // TF))
CAST_ROW_SLABS = DFF_PAD // CAST_ROW
NBLK_LAT = T_LAT // TM
NBLK_ALL = T_ALL // TM
NEG = -1e30
VMEM_LIMIT = 60 * 1024 * 1024


def _cparams(sem, vmem=VMEM_LIMIT):
    return pltpu.CompilerParams(dimension_semantics=sem, vmem_limit_bytes=vmem)


def _mod_index(i, tm):
    return jnp.minimum(i // (S // tm), NB)


def _stream_blocks(with_ctx, tm):
    return pl.cdiv(T_ALL if with_ctx else T_LAT, tm)


def _norm_mod(x, g, shift, scale):
    ms = jnp.mean(x * x, axis=-1, keepdims=True)
    return (x * lax.rsqrt(ms + RMS_EPS)) * g * (1.0 + scale) + shift


def _ada_kernel(s_ref, w_ref, b_ref, o_ref):
    s = s_ref[...]
    s = (s * jax.nn.sigmoid(s)).astype(BF16)
    o_ref[...] = jnp.dot(s, w_ref[...].astype(BF16), preferred_element_type=F32) + b_ref[...]


def _ada(cond, ada_w, ada_b):
    tn = 1024
    n = N_MOD * D
    return pl.pallas_call(
        _ada_kernel,
        grid=(DEPTH, n // tn),
        in_specs=[
            pl.BlockSpec((8, D), lambda l, j: (0, 0)),
            pl.BlockSpec((None, D, tn), lambda l, j: (l, 0, j)),
            pl.BlockSpec((None, 1, tn), lambda l, j: (l, 0, j)),
        ],
        out_specs=pl.BlockSpec((None, 8, tn), lambda l, j: (l, 0, j)),
        out_shape=jax.ShapeDtypeStruct((DEPTH, 8, n), F32),
        compiler_params=_cparams(("parallel", "parallel")),
        name="ada_mod",
    )(cond, ada_w, ada_b.reshape(DEPTH, 1, n))


def _ffn_kernel(x_ref, mod_ref, g_ref, w1_ref, w3_ref, w2_ref, *rest, jbase, nf, final, full_blocks,
                tail_rows, casting, n_plain):
    rest = list(rest)
    fg_ref = rest.pop(0) if final else None
    cast_in = [rest.pop(0) for _ in range(3)] if casting else []
    plain_in = [rest.pop(0) for _ in range(n_plain)]
    o_ref = rest.pop(0)
    cast_out = [rest.pop(0) for _ in range(3)] if casting else []
    plain_out = [rest.pop(0) for _ in range(n_plain)]
    (h_scr,) = rest
    i = pl.program_id(0)
    f = pl.program_id(1)
    tm = h_scr.shape[0]
    s = i * nf + f

    if n_plain:
        @pl.when(s < CAST_COL_SLABS)
        def _():
            for src, dst in zip(plain_in, plain_out):
                dst[...] = src[...].astype(BF16)

    if casting:
        @pl.when(s < CAST_COL_SLABS)
        def _():
            for src, dst in zip(cast_in[:2], cast_out[:2]):
                dst[:, :DFF] = src[...].astype(BF16)
                dst[:, DFF:] = jnp.zeros((dst.shape[0], DFF_PAD - DFF), BF16)

        @pl.when(s < DFF // CAST_ROW)
        def _():
            cast_out[2][...] = cast_in[2][...].astype(BF16)

        @pl.when((s >= DFF // CAST_ROW) & (s < CAST_ROW_SLABS))
        def _():
            cast_out[2][...] = jnp.zeros(cast_out[2].shape, BF16)

    def norm_rows(r, n):
        rows = slice(r, r + n)
        h = _norm_mod(x_ref[rows, :], g_ref[...], mod_ref[jbase:jbase + 1, :],
                      mod_ref[jbase + 1:jbase + 2, :])
        h_scr[rows, :] = h.astype(BF16)

    def chain(r, n, first):
        rows = slice(r, r + n)
        h = h_scr[rows, :]
        a = jnp.dot(h, w1_ref[...], preferred_element_type=F32)
        b = jnp.dot(h, w3_ref[...], preferred_element_type=F32)
        gate = (a * jax.nn.sigmoid(a) * b).astype(BF16)
        part = jnp.dot(gate, w2_ref[...], preferred_element_type=F32)
        if first:
            o_ref[rows, :] = part
        else:
            o_ref[rows, :] += part

    def epilogue(r, n):
        rows = slice(r, r + n)
        y = x_ref[rows, :] + (0.5 * mod_ref[jbase + 2:jbase + 3, :]) * o_ref[rows, :]
        if final:
            ms = jnp.mean(y * y, axis=-1, keepdims=True)
            y = (y * lax.rsqrt(ms + RMS_EPS)) * fg_ref[...]
        o_ref[rows, :] = y

    def step(live):
        chains = [(r, FFN_ROWS) for r in range(0, live, FFN_ROWS)]

        @pl.when(f == 0)
        def _():
            for r, n in chains:
                norm_rows(r, n)
                chain(r, n, True)
            if live < tm:
                o_ref[live:, :] = jnp.zeros((tm - live, o_ref.shape[1]), F32)

        @pl.when((f > 0) & (f < nf - 1))
        def _():
            for r, n in chains:
                chain(r, n, False)

        @pl.when(f == nf - 1)
        def _():
            for r, n in chains:
                chain(r, n, False)
                epilogue(r, n)

    if tail_rows == tm:
        step(tm)
    else:
        @pl.when(i < full_blocks)
        def _():
            step(tm)

        @pl.when(i >= full_blocks)
        def _():
            step(tail_rows)


def _ffn(xs, mods, g, w1, w3, w2, layer, jbase, with_ctx, final_g=None, cast_next=None, cast_plain=()):
    nf = DFF_PAD // TF
    final = final_g is not None
    tm = TM_FFN
    nblk = _stream_blocks(with_ctx, tm)
    in_specs = [
        pl.BlockSpec((tm, D), lambda i, f: (i, 0)),
        pl.BlockSpec((None, None, N_MOD, D),
                     lambda i, f: (layer, _mod_index(i, tm), 0, 0)),
        pl.BlockSpec((1, D), lambda i, f: (0, 0)),
        pl.BlockSpec((D, TF), lambda i, f: (0, f)),
        pl.BlockSpec((D, TF), lambda i, f: (0, f)),
        pl.BlockSpec((TF, D), lambda i, f: (f, 0)),
    ]
    args = [xs, mods, g.reshape(1, D), w1, w3, w2]
    out_specs = [pl.BlockSpec((tm, D), lambda i, f: (i, 0))]
    if final:
        in_specs.append(pl.BlockSpec((1, D), lambda i, f: (0, 0)))
        args.append(final_g.reshape(1, D))
        out_shape = [jax.ShapeDtypeStruct((nblk * tm, D), F32)]
        aliases = {}
    else:
        out_shape = [jax.ShapeDtypeStruct(xs.shape, F32)]
        aliases = {0: 0}
    if cast_next is not None:
        c1, c3, c2, nlayer = cast_next
        assert nblk * nf >= max(CAST_COL_SLABS, CAST_ROW_SLABS)
        step = lambda i, f: i * nf + f
        col_in = pl.BlockSpec((None, D // CAST_COL_SLABS, DFF),
                              lambda i, f: (nlayer, jnp.minimum(step(i, f), CAST_COL_SLABS - 1), 0))
        col_out = pl.BlockSpec((D // CAST_COL_SLABS, DFF_PAD),
                               lambda i, f: (jnp.minimum(step(i, f), CAST_COL_SLABS - 1), 0))
        row_in = pl.BlockSpec((None, CAST_ROW, D),
                              lambda i, f: (nlayer, jnp.minimum(step(i, f), DFF // CAST_ROW - 1), 0))
        row_out = pl.BlockSpec((CAST_ROW, D),
                               lambda i, f: (jnp.minimum(step(i, f), CAST_ROW_SLABS - 1), 0))
        in_specs += [col_in, col_in, row_in]
        args += [c1, c3, c2]
        out_specs += [col_out, col_out, row_out]
        out_shape += [jax.ShapeDtypeStruct((D, DFF_PAD), BF16), jax.ShapeDtypeStruct((D, DFF_PAD), BF16),
                      jax.ShapeDtypeStruct((DFF_PAD, D), BF16)]
    for src, slayer in cast_plain:
        _, r, c = src.shape
        slab = r // CAST_COL_SLABS
        in_specs.append(pl.BlockSpec(
            (None, slab, c), lambda i, f, slayer=slayer: (slayer, jnp.minimum(i * nf + f, CAST_COL_SLABS - 1), 0)))
        args.append(src)
        out_specs.append(pl.BlockSpec((slab, c), lambda i, f: (jnp.minimum(i * nf + f, CAST_COL_SLABS - 1), 0)))
        out_shape.append(jax.ShapeDtypeStruct((r, c), BF16))
    res = pl.pallas_call(
        functools.partial(_ffn_kernel, jbase=jbase, nf=nf, final=final, full_blocks=T_LAT // tm,
                          tail_rows=T_ALL - T_LAT if with_ctx else tm, casting=cast_next is not None,
                          n_plain=len(cast_plain)),
        grid=(nblk, nf),
        in_specs=in_specs,
        out_specs=out_specs,
        out_shape=out_shape,
        scratch_shapes=[pltpu.VMEM((tm, D), BF16)],
        input_output_aliases=aliases,
        compiler_params=_cparams(("arbitrary", "arbitrary")),
        name="ffn",
    )(*args)
    if cast_next is None and not cast_plain:
        return res[0]
    n_next = 3 if cast_next is not None else 0
    return res[0], tuple(res[1:1 + n_next]), tuple(res[1 + n_next:])


def _inproj_kernel(x_ref, mod_ref, g_ref, w_ref, qkv_ref, z_ref, h_scr, *, nq):
    j = pl.program_id(1)
    starts = range(0, h_scr.shape[0], PROJ_ROWS)

    def norm_rows(r):
        rows = slice(r, r + PROJ_ROWS)
        h = _norm_mod(x_ref[rows, :], g_ref[...], mod_ref[3:4, :], mod_ref[4:5, :])
        h_scr[rows, :] = h.astype(BF16)

    def proj(r):
        return jnp.dot(h_scr[r:r + PROJ_ROWS, :], w_ref[...], preferred_element_type=F32)

    @pl.when(j == 0)
    def _():
        for r in starts:
            norm_rows(r)
            qkv_ref[r:r + PROJ_ROWS, :] = (proj(r) * (HD ** -0.5)).astype(BF16)

    @pl.when((j > 0) & (j < nq))
    def _():
        for r in starts:
            qkv_ref[r:r + PROJ_ROWS, :] = proj(r).astype(BF16)

    @pl.when(j >= nq)
    def _():
        for r in starts:
            z_ref[r:r + PROJ_ROWS, :] = proj(r)


def _inproj(xs, mods, g, w_in, layer, e, with_ctx):
    tn = WA
    nq = 3 * WA // tn
    nz = 3 * WB // tn
    tm = TM_PROJ
    return pl.pallas_call(
        functools.partial(_inproj_kernel, nq=nq),
        grid=(_stream_blocks(with_ctx, tm), nq + nz),
        in_specs=[
            pl.BlockSpec((tm, D), lambda i, j: (i, 0)),
            pl.BlockSpec((None, None, N_MOD, D), lambda i, j: (layer, _mod_index(i, tm), 0, 0)),
            pl.BlockSpec((1, D), lambda i, j: (0, 0)),
            pl.BlockSpec((D, tn), lambda i, j: (0, j)),
        ],
        out_specs=[
            pl.BlockSpec((tm, tn), lambda i, j: (i, jnp.minimum(j, nq - 1))),
            pl.BlockSpec((tm, tn), lambda i, j: (i, jnp.maximum(j - nq, 0))),
        ],
        out_shape=[
            jax.ShapeDtypeStruct((T_PAD, 3 * WA), BF16),
            jax.ShapeDtypeStruct((T_PAD, 3 * WB), F32),
        ],
        scratch_shapes=[pltpu.VMEM((tm, D), BF16)],
        compiler_params=_cparams(("parallel", "arbitrary")),
        name="mix_inproj",
    )(xs, mods, g.reshape(1, D), w_in)


def _pw1_kernel(x_ref, mod_ref, g_ref, wa_ref, wg_ref, ba_ref, bg_ref, u_ref, h_scr):
    j = pl.program_id(1)
    starts = range(0, h_scr.shape[0], PROJ_ROWS)

    def norm_rows(r):
        rows = slice(r, r + PROJ_ROWS)
        h = _norm_mod(x_ref[rows, :], g_ref[...], mod_ref[3:4, :], mod_ref[4:5, :])
        h_scr[rows, :] = h.astype(BF16)

    def glu(r):
        rows = slice(r, r + PROJ_ROWS)
        h = h_scr[rows, :]
        a = jnp.dot(h, wa_ref[...], preferred_element_type=F32) + ba_ref[...]
        gt = jnp.dot(h, wg_ref[...], preferred_element_type=F32) + bg_ref[...]
        u_ref[rows, :] = a * jax.nn.sigmoid(gt)

    @pl.when(j == 0)
    def _():
        for r in starts:
            norm_rows(r)
            glu(r)

    @pl.when(j > 0)
    def _():
        for r in starts:
            glu(r)


def _pw1(xs, mods, g, w_pw1, b_pw1, layer, o, with_ctx):
    tn = 512
    nj = D // tn
    tm = TM_PROJ
    b3 = b_pw1.reshape(-1, 1, 2 * D)
    return pl.pallas_call(
        _pw1_kernel,
        grid=(_stream_blocks(with_ctx, tm), nj),
        in_specs=[
            pl.BlockSpec((tm, D), lambda i, j: (i, 0)),
            pl.BlockSpec((None, None, N_MOD, D), lambda i, j: (layer, _mod_index(i, tm), 0, 0)),
            pl.BlockSpec((1, D), lambda i, j: (0, 0)),
            pl.BlockSpec((D, tn), lambda i, j: (0, j)),
            pl.BlockSpec((D, tn), lambda i, j: (0, j + nj)),
            pl.BlockSpec((None, 1, tn), lambda i, j: (o, 0, j)),
            pl.BlockSpec((None, 1, tn), lambda i, j: (o, 0, j + nj)),
        ],
        out_specs=pl.BlockSpec((tm, tn), lambda i, j: (i, j)),
        out_shape=jax.ShapeDtypeStruct((T_PAD, D), F32),
        scratch_shapes=[pltpu.VMEM((tm, D), BF16)],
        compiler_params=_cparams(("parallel", "arbitrary")),
        name="conf_pw1",
    )(xs, mods, g.reshape(1, D), w_pw1, w_pw1, b3, b3)


NAT_G = 8
NAT_KR = {1: 8, 2: 10, 4: 12, 8: 16}[NAT_G]


def _natten_geometry():
    kstart, types, type_id = [], [], []
    gi = np.arange(NAT_G)[:, None]
    kj = np.arange(NAT_KR)[None, :]
    for g in range(GRID_H // NAT_G):
        r0 = g * NAT_G
        ks = min(max(r0 - WIN_R // 2, 0), GRID_H - NAT_KR)
        qrow, krow = r0 + gi, ks + kj
        rs = np.clip(qrow - WIN_R // 2, 0, GRID_H - WIN_R)
        valid = (krow >= rs) & (krow < rs + WIN_R)
        assert (valid.sum(axis=1) == WIN_R).all()
        dr = np.where(valid, krow - qrow + WIN_R - 1, 2 * WIN_R - 1).astype(np.int32)
        for t, d0 in enumerate(types):
            if (d0 == dr).all():
                type_id.append(t)
                break
        else:
            type_id.append(len(types))
            types.append(dr)
        kstart.append(ks)
    return np.asarray(kstart, np.int32), np.asarray(type_id, np.int32), np.stack(types)


def _natten_bias(rpb):
    _, _, dr = _natten_geometry()
    ntypes = dr.shape[0]
    c = np.arange(GRID_W)[:, None]
    kc = np.arange(GRID_W)[None, :]
    cs = np.clip(c - WIN_C // 2, 0, GRID_W - WIN_C)
    cvalid = (kc >= cs) & (kc < cs + WIN_C)
    dc = kc - c + WIN_C - 1
    onehot = ((np.arange(2 * WIN_C - 1)[:, None, None] == dc[None]) & cvalid[None]).astype(np.float32)
    colmask = np.where(cvalid, 0.0, NEG).astype(np.float32)
    bcols = jnp.einsum("hrd,dck->hrck", rpb, jnp.asarray(onehot), precision=lax.Precision.HIGHEST)
    bcols = (bcols + jnp.asarray(colmask)).transpose(0, 2, 1, 3)
    slabs = []
    for ty in range(ntypes):
        for gi in range(NAT_G):
            kj = np.nonzero(dr[ty, gi] != 2 * WIN_R - 1)[0]
            a, r0 = int(kj[0]), int(dr[ty, gi, kj[0]])
            assert (kj == a + np.arange(WIN_R)).all() and (dr[ty, gi, kj] == r0 + np.arange(WIN_R)).all()
            s = bcols[:, :, r0:r0 + WIN_R, :].reshape(NH, GRID_W, WIN_R * GRID_W)
            left, right = a * GRID_W, (NAT_KR - a - WIN_R) * GRID_W
            pieces = ([jnp.full((NH, GRID_W, left), NEG, F32)] if left else []) + [s] + (
                [jnp.full((NH, GRID_W, right), NEG, F32)] if right else [])
            slabs.append(jnp.concatenate(pieces, axis=2))
    return jnp.concatenate(slabs, axis=1).reshape(NH, ntypes, NAT_G * GRID_W, NAT_KR * GRID_W)


def _natten_kernel(ks_ref, ty_ref, q_ref, k_ref, v_ref, kc_ref, vc_ref, bias_ref, o_ref):
    gq, kk = NAT_G * GRID_W, NAT_KR * GRID_W
    kc = kc_ref[...]
    vc = vc_ref[...]
    nt = (((1,), (1,)), ((), ()))

    def body(g, carry):
        q0 = pl.multiple_of(g * gq, gq)
        k0 = pl.multiple_of(ks_ref[g] * GRID_W, GRID_W)
        q = q_ref[pl.ds(q0, gq), :]
        k = k_ref[pl.ds(k0, kk), :]
        v = v_ref[pl.ds(k0, kk), :]
        s_loc = lax.dot_general(q, k, nt, preferred_element_type=F32) + bias_ref[ty_ref[g]]
        s_ctx = lax.dot_general(q, kc, nt, preferred_element_type=F32)
        m = jnp.maximum(jnp.max(s_loc, axis=-1, keepdims=True),
                        jnp.max(s_ctx, axis=-1, keepdims=True))
        p_loc = jnp.exp(s_loc - m)
        p_ctx = jnp.exp(s_ctx - m)
        den = jnp.sum(p_loc, axis=-1, keepdims=True) + jnp.sum(p_ctx, axis=-1, keepdims=True)
        o = (jnp.dot(p_loc.astype(BF16), v, preferred_element_type=F32)
             + jnp.dot(p_ctx.astype(BF16), vc, preferred_element_type=F32))
        o_ref[pl.ds(q0, gq), :] = (o / den).astype(BF16)
        return carry

    lax.fori_loop(0, GRID_H // NAT_G, body, 0, unroll=True)


def _natten(qkv, rpb):
    kstart, type_id, _ = _natten_geometry()
    bias = _natten_bias(rpb)
    _, ntypes, gq, kk = bias.shape
    cb = T_LAT // LC
    grid_spec = pltpu.PrefetchScalarGridSpec(
        num_scalar_prefetch=2,
        grid=(NB, NH),
        in_specs=[
            pl.BlockSpec((S, HD), lambda b, h, *_: (b, h)),
            pl.BlockSpec((S, HD), lambda b, h, *_: (b, NH + h)),
            pl.BlockSpec((S, HD), lambda b, h, *_: (b, 2 * NH + h)),
            pl.BlockSpec((LC, HD), lambda b, h, *_: (cb + b, NH + h)),
            pl.BlockSpec((LC, HD), lambda b, h, *_: (cb + b, 2 * NH + h)),
            pl.BlockSpec((None, ntypes, gq, kk), lambda b, h, *_: (h, 0, 0, 0)),
        ],
        out_specs=pl.BlockSpec((S, HD), lambda b, h, *_: (b, h)),
    )
    return pl.pallas_call(
        _natten_kernel,
        grid_spec=grid_spec,
        out_shape=jax.ShapeDtypeStruct((T_ALL, WA), BF16),
        compiler_params=_cparams(("parallel", "parallel")),
        name="natten",
    )(jnp.asarray(kstart), jnp.asarray(type_id), qkv, qkv, qkv, qkv, qkv, bias)


def _ctx_attn_kernel(q_ref, k_ref, v_ref, att_hbm, o_ref):
    del att_hbm
    s = lax.dot_general(q_ref[...], k_ref[...], (((1,), (1,)), ((), ())),
                        preferred_element_type=F32)
    m = jnp.max(s, axis=-1, keepdims=True)
    p = jnp.exp(s - m)
    den = jnp.sum(p, axis=-1, keepdims=True)
    o = jnp.dot(p.astype(BF16), v_ref[...], preferred_element_type=F32)
    o_ref[...] = (o / den).astype(BF16)


def _ctx_attn(qkv, att):
    cb = T_LAT // LC
    return pl.pallas_call(
        _ctx_attn_kernel,
        grid=(NB, NH),
        in_specs=[
            pl.BlockSpec((LC, HD), lambda b, h: (cb + b, h)),
            pl.BlockSpec((LC, HD), lambda b, h: (cb + b, NH + h)),
            pl.BlockSpec((LC, HD), lambda b, h: (cb + b, 2 * NH + h)),
            pl.BlockSpec(memory_space=pl.ANY),
        ],
        out_specs=pl.BlockSpec((LC, HD), lambda b, h: (cb + b, h)),
        out_shape=jax.ShapeDtypeStruct(att.shape, att.dtype),
        input_output_aliases={3: 0},
        compiler_params=_cparams(("parallel", "parallel")),
        name="ctx_attn",
    )(qkv, qkv, qkv, att)


def _dft_tables(length, parity):
    n2 = 4 * length
    half = length // 2
    r = jnp.arange(half, dtype=jnp.int32)[:, None]
    hi = jnp.arange(half // LANE, dtype=jnp.int32)[None, :]
    lo = jnp.arange(LANE, dtype=jnp.int32)[None, :]

    def cs(phase):
        ang = (phase % n2).astype(F32) * (2.0 * math.pi / n2)
        return jnp.cos(ang), jnp.sin(ang)

    km = cs((2 * r + 1) * (2 * LANE * hi)) + cs((2 * r + 1) * (2 * lo + parity))
    mk = cs((2 * r + parity) * (2 * LANE * hi)) + cs((2 * r + parity) * (2 * lo + 1))
    return km, mk


def _dftgen_kernel(ca_ref, sa_ref, cb_ref, sb_ref, c_ref, s_ref, *, nt, sgn):
    cb = cb_ref[...]
    sb = sb_ref[...]
    for t1 in range(nt):
        ca = ca_ref[:, t1:t1 + 1]
        sa = sa_ref[:, t1:t1 + 1]
        sl = slice(t1 * LANE, (t1 + 1) * LANE)
        c_ref[:, sl] = (ca * cb - sa * sb).astype(BF16)
        s_ref[:, sl] = (sgn * (sa * cb + ca * sb)).astype(BF16)


def _dftgen(tables, length, sgn):
    tr = min(256, length)
    nt = length // LANE
    row = lambda i: (i, 0)
    return pl.pallas_call(
        functools.partial(_dftgen_kernel, nt=nt, sgn=sgn),
        grid=(length // tr,),
        in_specs=[pl.BlockSpec((tr, nt), row), pl.BlockSpec((tr, nt), row),
                  pl.BlockSpec((tr, LANE), row), pl.BlockSpec((tr, LANE), row)],
        out_specs=[pl.BlockSpec((tr, length), row), pl.BlockSpec((tr, length), row)],
        out_shape=[jax.ShapeDtypeStruct((length, length), BF16)] * 2,
        compiler_params=_cparams(("parallel",)),
        name="dft_gen",
    )(*tables)


def _hy_pre_kernel(z0_ref, z1_ref, z2_ref, w_ref, b_ref, u16_ref, u32_ref, x0_ref):
    half = z0_ref.shape[0] // 2

    def conv(z_ref, part):
        ze = z_ref[pl.ds(0, half, stride=2), :]
        zo = z_ref[pl.ds(1, half, stride=2), :]
        row = lax.broadcasted_iota(jnp.int32, ze.shape, 0)
        zo_prev = jnp.where(row == 0, 0.0, pltpu.roll(zo, 1, 0))
        ze_next = jnp.where(row == half - 1, 0.0, pltpu.roll(ze, half - 1, 0))
        w = w_ref[part]
        bias = b_ref[part]
        even = zo_prev * w[0:1, :] + ze * w[1:2, :] + zo * w[2:3, :] + bias
        odd = ze * w[0:1, :] + zo * w[1:2, :] + ze_next * w[2:3, :] + bias
        return even, odd

    x0 = conv(z0_ref, 0)
    x1 = conv(z1_ref, 1)
    v = conv(z2_ref, 2)
    for p in range(2):
        u = v[p] * x1[p]
        x0_ref[p] = x0[p]
        u32_ref[p] = u
        u16_ref[p] = u.astype(BF16)


def _hy_pre(z, short_w, short_b, length, row_off):
    tc = LANE
    nc = WB // tc
    rb = row_off // length
    half = length // 2
    w = short_w.reshape(3, 3, WB).transpose(1, 0, 2)
    bb = short_b.reshape(3, 1, WB)
    out_spec = pl.BlockSpec((2, half, tc), lambda b, c: (0, 0, b * nc + c))
    return pl.pallas_call(
        _hy_pre_kernel,
        grid=(NB, nc),
        in_specs=[
            pl.BlockSpec((length, tc), lambda b, c: (rb + b, c)),
            pl.BlockSpec((length, tc), lambda b, c: (rb + b, nc + c)),
            pl.BlockSpec((length, tc), lambda b, c: (rb + b, 2 * nc + c)),
            pl.BlockSpec((3, 3, tc), lambda b, c: (0, 0, c)),
            pl.BlockSpec((3, 1, tc), lambda b, c: (0, 0, c)),
        ],
        out_specs=[out_spec, out_spec, out_spec],
        out_shape=[jax.ShapeDtypeStruct((2, half, NB * WB), BF16),
                   jax.ShapeDtypeStruct((2, half, NB * WB), F32),
                   jax.ShapeDtypeStruct((2, half, NB * WB), F32)],
        compiler_params=_cparams(("parallel", "parallel")),
        name="hyena_pre",
    )(z, z, z, w, bb)


def _hy_filter_kernel(z_ref, w0_ref, b0_ref, w1_ref, b1_ref, w2_ref, b2_ref, w3_ref, fr_ref,
                      dl_ref, fs_ref, fd_ref):
    hp = lax.Precision.HIGHEST
    z = z_ref[...]
    fr = fr_ref[...]
    h = jnp.sin(fr * (jnp.dot(z, w0_ref[...], precision=hp, preferred_element_type=F32) + b0_ref[...]))
    h = jnp.sin(fr * (jnp.dot(h, w1_ref[...], precision=hp, preferred_element_type=F32) + b1_ref[...]))
    h = jnp.sin(fr * (jnp.dot(h, w2_ref[...], precision=hp, preferred_element_type=F32) + b2_ref[...]))
    hh = jnp.dot(h.astype(BF16), w3_ref[...].astype(BF16), preferred_element_type=F32)
    win = jnp.exp(-z[:, 0:1] * dl_ref[...])
    fwd = hh[:, :WB] * win
    bwd = hh[:, WB:] * win
    row = lax.broadcasted_iota(jnp.int32, bwd.shape, 0) + pl.program_id(0) * z.shape[0]
    bwd = jnp.where(row == 0, 0.0, bwd)
    fs_ref[...] = (fwd + bwd).astype(BF16)
    fd_ref[...] = (bwd - fwd).astype(BF16)


def _hy_filter(length, w0, b0, w1, b1, w2, b2, w3, freq):
    t = jnp.linspace(0.0, 1.0, length, dtype=F32)[:, None]
    bands = (HY_EMB - 1) // 2
    f = jnp.linspace(1e-4, bands - 1, bands, dtype=F32)
    w = 2 * math.pi * jnp.arange(length, dtype=F32)[:, None] / length
    z = jnp.concatenate([t, jnp.cos(f * w), -jnp.sin(f * w)], axis=-1)
    z = jnp.concatenate([z[0::2], z[1::2]], axis=0)
    emb = HY_ORDER
    z = jnp.pad(z, ((0, 0), (0, emb - HY_EMB)))
    w0p = jnp.pad(w0, ((0, emb - HY_EMB), (0, 0)))
    max_decay = math.log(1e-2) / 0.3
    min_decay = math.log(1e-2) / 1.5
    deltas = jnp.abs(jnp.linspace(min_decay, max_decay, WB, dtype=F32))[None, :]
    tt = min(256, length)
    full = lambda shape: pl.BlockSpec(shape, lambda i: (0,) * len(shape))
    row = lambda i: (i, 0)
    vec = lambda a: a.reshape(1, -1)
    return pl.pallas_call(
        _hy_filter_kernel,
        grid=(length // tt,),
        in_specs=[pl.BlockSpec((tt, emb), row),
                  full((emb, HY_ORDER)), full((1, HY_ORDER)),
                  full((HY_ORDER, HY_ORDER)), full((1, HY_ORDER)),
                  full((HY_ORDER, HY_ORDER)), full((1, HY_ORDER)),
                  full((HY_ORDER, 2 * WB)), full((1, HY_ORDER)), full((1, WB))],
        out_specs=[pl.BlockSpec((tt, WB), row), pl.BlockSpec((tt, WB), row)],
        out_shape=[jax.ShapeDtypeStruct((length, WB), BF16)] * 2,
        compiler_params=_cparams(("parallel",)),
        name="hyena_filter",
    )(z, w0p, vec(b0), w1, vec(b1), w2, vec(b2), w3, vec(freq), deltas)


def _dft_tiles(length):
    return min(512, length // 2), 512


def _dft_mat_specs(tr, half):
    return [pl.BlockSpec((tr, half), lambda j, i: (i, 0))] * 4


def _dft_spec_kernel(ce_ref, se_ref, co_ref, so_ref, fs_ref, fd_ref, a_ref, b_ref, *, scale):
    ae = jnp.dot(ce_ref[...], fs_ref[0], preferred_element_type=F32)
    ao = jnp.dot(co_ref[...], fs_ref[1], preferred_element_type=F32)
    be = jnp.dot(se_ref[...], fd_ref[0], preferred_element_type=F32)
    bo = jnp.dot(so_ref[...], fd_ref[1], preferred_element_type=F32)
    a_ref[0] = (ae + ao) * scale
    a_ref[1] = (ae - ao) * scale
    b_ref[0] = (be + bo) * scale
    b_ref[1] = (bo - be) * scale


def _dft_spec(mats, fs, fd, length):
    tr, tc = _dft_tiles(length)
    half = length // 2
    return pl.pallas_call(
        functools.partial(_dft_spec_kernel, scale=1.0 / length),
        grid=(WB // tc, half // tr),
        in_specs=_dft_mat_specs(tr, half) + [pl.BlockSpec((2, half, tc), lambda j, i: (0, 0, j))] * 2,
        out_specs=[pl.BlockSpec((2, tr, tc), lambda j, i: (0, i, j))] * 2,
        out_shape=[jax.ShapeDtypeStruct((2, half, WB), F32)] * 2,
        compiler_params=_cparams(("parallel", "parallel")),
        name="hyena_filter_dft",
    )(*mats, fs, fd)


def _dft_fwd_kernel(ce_ref, se_ref, co_ref, so_ref, u_ref, a_ref, b_ref, y_ref):
    ue = u_ref[0]
    uo = u_ref[1]
    pe = jnp.dot(ce_ref[...], ue, preferred_element_type=F32)
    po = jnp.dot(co_ref[...], uo, preferred_element_type=F32)
    qe = jnp.dot(se_ref[...], ue, preferred_element_type=F32)
    qo = jnp.dot(so_ref[...], uo, preferred_element_type=F32)
    re, im = [], []
    for h, (p, q) in enumerate(((pe + po, qe + qo), (pe - po, qo - qe))):
        a = a_ref[h]
        b = b_ref[h]
        re.append(p * a + q * b)
        im.append(p * b - q * a)
    y_ref[0] = (re[0] + re[1]).astype(BF16)
    y_ref[1] = (im[0] - im[1]).astype(BF16)
    y_ref[2] = (re[0] - re[1]).astype(BF16)
    y_ref[3] = (im[0] + im[1]).astype(BF16)


def _dft_fwd(mats, u16, a, bq, length):
    tr, tc = _dft_tiles(length)
    half = length // 2
    ncj = WB // tc
    return pl.pallas_call(
        _dft_fwd_kernel,
        grid=(NB * ncj, half // tr),
        in_specs=_dft_mat_specs(tr, half) + [
            pl.BlockSpec((2, half, tc), lambda j, i: (0, 0, j)),
            pl.BlockSpec((2, tr, tc), lambda j, i: (0, i, j % ncj)),
            pl.BlockSpec((2, tr, tc), lambda j, i: (0, i, j % ncj))],
        out_specs=pl.BlockSpec((4, tr, tc), lambda j, i: (0, i, j)),
        out_shape=jax.ShapeDtypeStruct((4, half, NB * WB), BF16),
        compiler_params=_cparams(("parallel", "parallel")),
        name="hyena_fwd_dft",
    )(*mats, u16, a, bq)


def _dft_inv_kernel(ce_ref, se_ref, co_ref, so_ref, y_ref, u_ref, x0_ref, bias_ref, *rest):
    o_ref, y_scr = rest[-2:]
    tr = ce_ref.shape[0]
    ye = (jnp.dot(ce_ref[...], y_ref[0], preferred_element_type=F32)
          + jnp.dot(se_ref[...], y_ref[1], preferred_element_type=F32))
    yo = (jnp.dot(co_ref[...], y_ref[2], preferred_element_type=F32)
          + jnp.dot(so_ref[...], y_ref[3], preferred_element_type=F32))
    bias = bias_ref[...]
    oe = (ye + u_ref[0] * bias) * x0_ref[0]
    oo = (yo + u_ref[1] * bias) * x0_ref[1]
    for c in range(y_scr.shape[0]):
        lanes = slice(c * LANE, (c + 1) * LANE)
        y_scr[c, pl.ds(0, tr, stride=2), :] = oe[:, lanes]
        y_scr[c, pl.ds(1, tr, stride=2), :] = oo[:, lanes]
        o_ref[:, lanes] = y_scr[c].astype(BF16)


def _dft_inv(mats, ysum, u32, x0, bias, length, row_off, hy=None):
    tr, tc = _dft_tiles(length)
    half = length // 2
    ncj = WB // tc
    nri = half // tr
    rb = row_off // (2 * tr)
    aliased = hy is not None
    in_specs = _dft_mat_specs(tr, half) + [
        pl.BlockSpec((4, half, tc), lambda j, i: (0, 0, j)),
        pl.BlockSpec((2, tr, tc), lambda j, i: (0, i, j)),
        pl.BlockSpec((2, tr, tc), lambda j, i: (0, i, j)),
        pl.BlockSpec((1, tc), lambda j, i: (0, j % ncj))]
    args = [*mats, ysum, u32, x0, bias.reshape(1, WB)]
    if aliased:
        in_specs.append(pl.BlockSpec(memory_space=pl.ANY))
        args.append(hy)
    return pl.pallas_call(
        _dft_inv_kernel,
        grid=(NB * ncj, nri),
        in_specs=in_specs,
        out_specs=pl.BlockSpec((2 * tr, tc), lambda j, i: (rb + (j // ncj) * nri + i, j % ncj)),
        out_shape=jax.ShapeDtypeStruct((T_ALL, WB), BF16),
        scratch_shapes=[pltpu.VMEM((tc // LANE, 2 * tr, LANE), F32)],
        input_output_aliases={len(args) - 1: 0} if aliased else {},
        compiler_params=_cparams(("parallel", "parallel")),
        name="hyena_inv_dft",
    )(*args)


def _hyena(z, hy_params, mats, length, row_off, hy=None):
    short_w, short_b, w0, b0, w1, b1, w2, b2, w3, freq, bias = hy_params
    fwd_mats, inv_mats = mats
    half = length // 2
    u16, u32, x0 = _hy_pre(z, short_w, short_b, length, row_off)
    fs, fd = _hy_filter(length, w0, b0, w1, b1, w2, b2, w3, freq)
    a, bq = _dft_spec(fwd_mats, fs.reshape(2, half, WB), fd.reshape(2, half, WB), length)
    ysum = _dft_fwd(fwd_mats, u16, a, bq, length)
    return _dft_inv(inv_mats, ysum, u32, x0, bias, length, row_off, hy)


def _dft_mats(length):
    half = length // 2
    km_e, mk_e = _dft_tables(length, 0)
    km_o, mk_o = _dft_tables(length, 1)
    fwd = _dftgen(km_e, half, 1.0) + _dftgen(km_o, half, 1.0)
    inv = _dftgen(mk_e, half, -1.0) + _dftgen(mk_o, half, -1.0)
    return tuple(fwd), tuple(inv)


def _outproj_kernel(x_ref, mod_ref, a_ref, y_ref, wa_ref, wy_ref, o_ref):
    o = (jnp.dot(a_ref[...], wa_ref[...], preferred_element_type=F32)
         + jnp.dot(y_ref[...], wy_ref[...], preferred_element_type=F32))
    o_ref[...] = x_ref[...] + mod_ref[5:6, :] * o


def _outproj(xs, mods, att, hy, w_out, layer, e, nblk):
    return pl.pallas_call(
        _outproj_kernel,
        grid=(nblk,),
        in_specs=[
            pl.BlockSpec((TM, D), lambda i: (i, 0)),
            pl.BlockSpec((None, None, N_MOD, D), lambda i: (layer, _mod_index(i, TM), 0, 0)),
            pl.BlockSpec((TM, WA), lambda i: (i, 0)),
            pl.BlockSpec((TM, WB), lambda i: (i, 0)),
            pl.BlockSpec((WA, D), lambda i: (0, 0)),
            pl.BlockSpec((WB, D), lambda i: (1, 0)),
        ],
        out_specs=pl.BlockSpec((TM, D), lambda i: (i, 0)),
        out_shape=jax.ShapeDtypeStruct(xs.shape, F32),
        input_output_aliases={0: 0},
        compiler_params=_cparams(("parallel",)),
        name="mix_outproj",
    )(xs, mods, att, hy, w_out, w_out)


CONV_HALO = 16
CONV_RC = 64


def _conf_out_kernel(x_ref, mod_ref, up_ref, uc_ref, un_ref, wdw_ref, bdw_ref, lg_ref, lb_ref,
                     w2_ref, b2_ref, o_ref, ext_scr, cv_scr, ph_scr, *, tm, bps):
    i = pl.program_id(0)
    first = (i % bps) == 0
    last = (i % bps) == bps - 1
    halo = CONV_HALO
    zero = jnp.zeros((halo, D), F32)
    ext_scr[0:halo, :] = jnp.where(first, zero, up_ref[...])
    ext_scr[halo:halo + tm, :] = uc_ref[...]
    ext_scr[halo + tm:2 * halo + tm, :] = jnp.where(last, zero, un_ref[...])
    off = halo - CONV_W // 2

    def col_body(cc, carry):
        c0 = pl.multiple_of(cc * LANE, LANE)
        wv = wdw_ref[:, pl.ds(c0, LANE)]
        bv = bdw_ref[:, pl.ds(c0, LANE)]

        span = ph_scr.shape[1]
        for p in range(1, SUBLANE):
            ph_scr[p] = ext_scr[pl.ds(p, span), pl.ds(c0, LANE)]
        for r0 in range(0, tm, CONV_RC):
            acc = jnp.zeros((CONV_RC, LANE), F32)
            for j in range(CONV_W):
                p = (off + j) % SUBLANE
                base = off + j - p + r0
                if p == 0:
                    rows = ext_scr[pl.ds(base, CONV_RC), pl.ds(c0, LANE)]
                else:
                    rows = ph_scr[p, pl.ds(base, CONV_RC), :]
                acc = acc + rows * wv[j:j + 1, :]
            cv_scr[pl.ds(r0, CONV_RC), pl.ds(c0, LANE)] = acc + bv
        return carry

    lax.fori_loop(0, D // LANE, col_body, 0)

    v = cv_scr[...]
    mu = jnp.mean(v, axis=-1, keepdims=True)
    vc = v - mu
    var = jnp.mean(vc * vc, axis=-1, keepdims=True)
    t = vc * lax.rsqrt(var + LN_EPS) * lg_ref[...] + lb_ref[...]
    t = (t * jax.nn.sigmoid(t)).astype(BF16)
    o = jnp.dot(t, w2_ref[...], preferred_element_type=F32) + b2_ref[...]
    o_ref[...] = x_ref[...] + mod_ref[5:6, :] * o


def _conf_out(xs, mods, u, w_dw, b_dw, ln_g, ln_b, w_pw2, b_pw2, layer, o, tm, row_off, seq, nblk):
    bps = seq // tm
    rb = row_off // tm
    vec = lambda a: a.reshape(-1, 1, D)
    blk = lambda i: (rb + i, 0)
    hpb = tm // CONV_HALO
    prev = lambda i: (jnp.maximum((rb + i) * hpb - 1, 0), 0)
    nxt = lambda i: (jnp.minimum((rb + i + 1) * hpb, T_ALL // CONV_HALO - 1), 0)
    modmap = lambda i: (layer, _mod_index(rb + i, tm), 0, 0)
    return pl.pallas_call(
        functools.partial(_conf_out_kernel, tm=tm, bps=bps),
        grid=(nblk,),
        in_specs=[
            pl.BlockSpec((tm, D), blk),
            pl.BlockSpec((None, None, N_MOD, D), modmap),
            pl.BlockSpec((CONV_HALO, D), prev),
            pl.BlockSpec((tm, D), blk),
            pl.BlockSpec((CONV_HALO, D), nxt),
            pl.BlockSpec((None, CONV_W, D), lambda i: (o, 0, 0)),
            pl.BlockSpec((None, 1, D), lambda i: (o, 0, 0)),
            pl.BlockSpec((None, 1, D), lambda i: (o, 0, 0)),
            pl.BlockSpec((None, 1, D), lambda i: (o, 0, 0)),
            pl.BlockSpec((D, D), lambda i: (0, 0)),
            pl.BlockSpec((None, 1, D), lambda i: (o, 0, 0)),
        ],
        out_specs=pl.BlockSpec((tm, D), blk),
        out_shape=jax.ShapeDtypeStruct(xs.shape, F32),
        scratch_shapes=[pltpu.VMEM((tm + 2 * CONV_HALO, D), F32), pltpu.VMEM((tm, D), F32),
                        pltpu.VMEM((SUBLANE, tm + 2 * CONV_HALO - SUBLANE, LANE), F32)],
        input_output_aliases={0: 0},
        compiler_params=_cparams(("parallel",)),
        name="conf_out",
    )(xs, mods, u, u, u, w_dw, vec(b_dw), vec(ln_g), vec(ln_b), w_pw2, vec(b_pw2))


def _cast_pad_kernel(w_ref, o_ref, *, n, axis):
    v = w_ref[...].astype(BF16)
    if axis == 0:
        o_ref[:n, :] = v
        o_ref[n:, :] = jnp.zeros((o_ref.shape[0] - n, o_ref.shape[1]), BF16)
    else:
        o_ref[:, :n] = v
        o_ref[:, n:] = jnp.zeros((o_ref.shape[0], o_ref.shape[1] - n), BF16)


def _cast_pad(w, layer, axis, padded):
    _, r, c = w.shape
    strip = 256
    if axis == 0:
        in_blk, out_blk, out_shape, n = (None, r, strip), (padded, strip), (padded, c), r
        imap, omap = (lambda j: (layer, 0, j)), (lambda j: (0, j))
        steps = c // strip
    else:
        in_blk, out_blk, out_shape, n = (None, strip, c), (strip, padded), (r, padded), c
        imap, omap = (lambda j: (layer, j, 0)), (lambda j: (j, 0))
        steps = r // strip
    return pl.pallas_call(
        functools.partial(_cast_pad_kernel, n=n, axis=axis),
        grid=(steps,),
        in_specs=[pl.BlockSpec(in_blk, imap)],
        out_specs=pl.BlockSpec(out_blk, omap),
        out_shape=jax.ShapeDtypeStruct(out_shape, BF16),
        compiler_params=_cparams(("parallel",)),
        name="cast_pad",
    )(w)


def kernel(x, c, ctx, c_ctx, ada_w, ada_b, norm_g, ff1_w1, ff1_w3, ff1_w2, ff2_w1, ff2_w3, ff2_w2,
           mix_w_in, mix_w_out, na_rpb, hy_short_w, hy_short_b, hy_w0, hy_b0, hy_w1, hy_b1,
           hy_w2, hy_b2, hy_w3, hy_freq, hy_bias, cv_w_pw1, cv_b_pw1, cv_w_dw, cv_b_dw,
           cv_ln_g, cv_ln_b, cv_w_pw2, cv_b_pw2, final_g):
    xs = jnp.concatenate([x.reshape(T_LAT, D), ctx.reshape(T_CTX, D),
                          jnp.zeros((T_PAD - T_ALL, D), F32)], axis=0)
    cond = jnp.concatenate([c, c_ctx[None, :], jnp.zeros((8 - NB - 1, D), F32)], axis=0)
    mods = _ada(cond, ada_w, ada_b)[:, :NB + 1].reshape(DEPTH, NB + 1, N_MOD, D)

    ff1 = (ff1_w1, ff1_w3, ff1_w2)
    ff2 = (ff2_w1, ff2_w3, ff2_w2)
    wts = (_cast_pad(ff1_w1, 0, 1, DFF_PAD), _cast_pad(ff1_w3, 0, 1, DFF_PAD),
           _cast_pad(ff1_w2, 0, 0, DFF_PAD))
    mats_lat = _dft_mats(S)
    mats_ctx = _dft_mats(LC)

    out = None
    for i in range(DEPTH):
        ctx_full = i < LAST_EVEN
        mixer_w = (mix_w_in, i // 2) if i % 2 == 0 else (cv_w_pw1, i // 2)
        xs, wts, (w_proj,) = _ffn(xs, mods, norm_g[i, 0], *wts, i, 0, i <= LAST_EVEN,
                                  cast_next=(*ff2, i), cast_plain=(mixer_w,))
        if i % 2 == 0:
            e = i // 2
            hyp = (hy_short_w[e], hy_short_b[e], hy_w0[e], hy_b0[e], hy_w1[e], hy_b1[e],
                   hy_w2[e], hy_b2[e], hy_w3[e], hy_freq[e], hy_bias[e])
            qkv, z = _inproj(xs, mods, norm_g[i, 1], w_proj, i, e, i <= LAST_EVEN)
            att = _natten(qkv, na_rpb[e])
            hy = _hyena(z, hyp, mats_lat, S, 0)
            if ctx_full:
                att = _ctx_attn(qkv, att)
                hy = _hyena(z, hyp, mats_ctx, LC, T_LAT, hy)
            xs = _outproj(xs, mods, att, hy, mix_w_out[e].astype(BF16), i, e,
                          NBLK_ALL if ctx_full else NBLK_LAT)
        else:
            o = i // 2
            u = _pw1(xs, mods, norm_g[i, 1], w_proj, cv_b_pw1, i, o, ctx_full)
            cv = (cv_w_dw, cv_b_dw, cv_ln_g, cv_ln_b, cv_w_pw2[o].astype(BF16), cv_b_pw2)
            xs = _conf_out(xs, mods, u, *cv, i, o, TM, 0, S, NBLK_LAT)
            if ctx_full:
                xs = _conf_out(xs, mods, u, *cv, i, o, LC, T_LAT, LC, T_CTX // LC)
        if i == DEPTH - 1:
            out = _ffn(xs, mods, norm_g[i, 2], *wts, i, 6, ctx_full, final_g=final_g)
        else:
            xs, wts, _ = _ffn(xs, mods, norm_g[i, 2], *wts, i, 6, ctx_full, cast_next=(*ff1, i + 1))
    return out.reshape(NB, S, D)
```
